```python
import jax, jax.numpy as jnp
from jax import lax
import numpy as np

D_MODEL = 1024
BATCH = 16
SEQ = 2048
DEPTH = 4

GRID_W = 64
CTX_LEN = 256
CONV_WIDTH = 512
CONV_K = 3
N_HEADS = 8
N_KV_HEADS = 2
GROUP = N_HEADS // N_KV_HEADS
HEAD_DIM = 64
ATTN_WIDTH = N_HEADS * HEAD_DIM
KV_WIDTH = N_KV_HEADS * HEAD_DIM
WINDOW = 128
BLOCK = 128
ROPE_THETA = 10000.0
ROPE_FREQS = HEAD_DIM // 4
EPS = 1e-6
NEG = -1e30
IN_COLS = 4 * CONV_WIDTH + 2 * ATTN_WIDTH + 2 * KV_WIDTH + 2 * D_MODEL

kernel_name = "hybrid_conv_swa_dit_block"


def rms_norm(x, w):
    xf = x.astype(jnp.float32)
    y = xf * lax.rsqrt(jnp.mean(xf * xf, axis=-1, keepdims=True) + EPS)
    return (y * w.astype(jnp.float32)).astype(x.dtype)


def project(h, w_in):
    sizes = (CONV_WIDTH, CONV_WIDTH, CONV_WIDTH, CONV_WIDTH,
             ATTN_WIDTH, KV_WIDTH, KV_WIDTH, ATTN_WIDTH, D_MODEL, D_MODEL)
    points = tuple(int(p) for p in np.cumsum(sizes)[:-1])
    p = jnp.einsum('bsd,de->bse', h, w_in)
    return jnp.split(p, points, axis=-1)


def short_conv(u, w):
    L = u.shape[1]
    pad = CONV_K // 2
    up = jnp.pad(u, ((0, 0), (pad, pad), (0, 0)))
    return sum(up[:, k:k + L] * w[k] for k in range(CONV_K))


def rope_2d(x, cos, sin):
    B_, S_, H_, _ = x.shape
    xr = x.reshape(B_, S_, H_, 2, 2, ROPE_FREQS)
    x1, x2 = xr[..., 0, :], xr[..., 1, :]
    cs = cos[None, :, None]
    sn = sin[None, :, None]
    out = jnp.stack([x1 * cs - x2 * sn, x2 * cs + x1 * sn], axis=-2)
    return out.reshape(B_, S_, H_, HEAD_DIM).astype(x.dtype)


def sink_softmax(scores, sink):
    sk = jnp.broadcast_to(sink.astype(jnp.float32).reshape(N_KV_HEADS, GROUP, 1, 1),
                          scores.shape[:-1] + (1,))
    p = jax.nn.softmax(jnp.concatenate([sk, scores], axis=-1), axis=-1)
    return p[..., 1:]


def latent_attention(q, k, v, k_ctx, v_ctx, sink):
    B_, S_ = q.shape[:2]
    nb = S_ // BLOCK
    scale = HEAD_DIM ** -0.5
    k_pad = jnp.pad(k, ((0, 0), (BLOCK, BLOCK), (0, 0), (0, 0)))
    v_pad = jnp.pad(v, ((0, 0), (BLOCK, BLOCK), (0, 0), (0, 0)))
    qb = jnp.moveaxis(q.reshape(B_, nb, BLOCK, N_KV_HEADS, GROUP, HEAD_DIM), 1, 0)

    def one_block(args):
        i, q_i = args
        start = i * BLOCK
        k_i = lax.dynamic_slice_in_dim(k_pad, start, 3 * BLOCK, axis=1)
        v_i = lax.dynamic_slice_in_dim(v_pad, start, 3 * BLOCK, axis=1)
        s_loc = jnp.einsum('bqkgd,bskd->bkgqs', q_i, k_i).astype(jnp.float32) * scale
        s_ctx = jnp.einsum('bqkgd,bckd->bkgqc', q_i, k_ctx).astype(jnp.float32) * scale
        q_pos = start + jnp.arange(BLOCK)
        k_pos = start - BLOCK + jnp.arange(3 * BLOCK)
        allowed = ((jnp.abs(q_pos[:, None] - k_pos[None, :]) <= WINDOW)
                   & (k_pos[None, :] >= 0) & (k_pos[None, :] < S_))
        s_loc = jnp.where(allowed, s_loc, NEG)
        p = sink_softmax(jnp.concatenate([s_loc, s_ctx], axis=-1), sink).astype(v.dtype)
        o = (jnp.einsum('bkgqs,bskd->bqkgd', p[..., :3 * BLOCK], v_i)
             + jnp.einsum('bkgqc,bckd->bqkgd', p[..., 3 * BLOCK:], v_ctx))
        return o

    out = lax.map(one_block, (jnp.arange(nb), qb))
    return jnp.moveaxis(out, 0, 1).reshape(B_, S_, ATTN_WIDTH)


def context_attention(q_c, k_c, v_c, sink):
    B_, L_ = q_c.shape[:2]
    qg = q_c.reshape(B_, L_, N_KV_HEADS, GROUP, HEAD_DIM)
    s = jnp.einsum('bqkgd,bckd->bkgqc', qg, k_c).astype(jnp.float32) * HEAD_DIM ** -0.5
    p = sink_softmax(s, sink).astype(v_c.dtype)
    return jnp.einsum('bkgqc,bckd->bqkgd', p, v_c).reshape(B_, L_, ATTN_WIDTH)


def mixer_output(parts, attn, conv_w, w_a_out, w_b_out, w_o):
    b_g, c_g, u, z_a, _, _, _, z_b, g_a, g_b = parts
    y_a = (b_g * short_conv(c_g * u, conv_w)) * jax.nn.silu(z_a)
    y_a = jnp.einsum('bsc,cd->bsd', y_a, w_a_out)
    y_b = jnp.einsum('bsc,cd->bsd', attn * jax.nn.silu(z_b), w_b_out)
    m = jax.nn.sigmoid(g_a) * y_a + jax.nn.sigmoid(g_b) * y_b
    return jnp.einsum('bsd,de->bse', m, w_o)


def setup_inputs(seed: int = 0) -> dict:
    key = jax.random.key(seed)
    ks = jax.random.split(key, 14)
    nrm = jax.random.normal
    f32 = jnp.float32
    return {
        "x": nrm(ks[0], (BATCH, SEQ, D_MODEL), f32),
        "c": nrm(ks[1], (BATCH, D_MODEL), f32),
        "ctx": nrm(ks[2], (BATCH, CTX_LEN, D_MODEL), f32),
        "c_ctx": nrm(ks[3], (D_MODEL,), f32),
        "norm_w": 1.0 + 0.02 * nrm(ks[4], (DEPTH, D_MODEL), f32),
        "w_mod": nrm(ks[5], (DEPTH, D_MODEL, 3 * D_MODEL), f32) * (0.5 * D_MODEL ** -0.5),
        "b_mod": 0.01 * nrm(ks[6], (DEPTH, 3 * D_MODEL), f32),
        "w_in": nrm(ks[7], (DEPTH, D_MODEL, IN_COLS), f32) * D_MODEL ** -0.5,
        "conv_w": nrm(ks[8], (DEPTH, CONV_K, CONV_WIDTH), f32) * CONV_K ** -0.5,
        "w_a_out": nrm(ks[9], (DEPTH, CONV_WIDTH, D_MODEL), f32) * CONV_WIDTH ** -0.5,
        "w_b_out": nrm(ks[10], (DEPTH, ATTN_WIDTH, D_MODEL), f32) * ATTN_WIDTH ** -0.5,
        "attn_sink": 0.5 * nrm(ks[11], (DEPTH, N_HEADS), f32),
        "w_o": nrm(ks[12], (DEPTH, D_MODEL, D_MODEL), f32) * D_MODEL ** -0.5,
        "final_norm_w": 1.0 + 0.02 * nrm(ks[13], (D_MODEL,), f32),
    }


def reference(x, c, ctx, c_ctx, norm_w, w_mod, b_mod, w_in, conv_w, w_a_out, w_b_out,
              attn_sink, w_o, final_norm_w):
    B_, S_, _ = x.shape
    L_ = ctx.shape[1]
    ROWS = S_ // GRID_W
    rows = jnp.broadcast_to(jnp.arange(ROWS)[:, None], (ROWS, GRID_W)).reshape(-1)
    cols = jnp.broadcast_to(jnp.arange(GRID_W)[None, :], (ROWS, GRID_W)).reshape(-1)
    inv_freq = ROPE_THETA ** (-jnp.arange(ROPE_FREQS, dtype=jnp.float32) / ROPE_FREQS)
    ang = jnp.stack([rows.astype(jnp.float32)[:, None] * inv_freq,
                     cols.astype(jnp.float32)[:, None] * inv_freq], axis=1)
    cos, sin = jnp.cos(ang), jnp.sin(ang)

    for l in range(DEPTH):
        last = l == DEPTH - 1
        mod_x = jnp.einsum('bd,de->be', jax.nn.silu(c), w_mod[l]) + b_mod[l]
        shift_x, scale_x, gate_x = jnp.split(mod_x[:, None, :], 3, axis=-1)
        mod_c = jnp.einsum('d,de->e', jax.nn.silu(c_ctx), w_mod[l]) + b_mod[l]
        shift_c, scale_c, gate_c = jnp.split(mod_c, 3, axis=-1)

        hx = rms_norm(x, norm_w[l]) * (1.0 + scale_x) + shift_x
        hc = rms_norm(ctx, norm_w[l]) * (1.0 + scale_c) + shift_c
        px = project(hx, w_in[l])
        pc = project(hc, w_in[l])

        q_c = pc[4].reshape(B_, L_, N_HEADS, HEAD_DIM)
        k_c = pc[5].reshape(B_, L_, N_KV_HEADS, HEAD_DIM)
        v_c = pc[6].reshape(B_, L_, N_KV_HEADS, HEAD_DIM)

        q_x = rope_2d(px[4].reshape(B_, S_, N_HEADS, HEAD_DIM), cos, sin)
        k_x = rope_2d(px[5].reshape(B_, S_, N_KV_HEADS, HEAD_DIM), cos, sin)
        v_x = px[6].reshape(B_, S_, N_KV_HEADS, HEAD_DIM)
        attn_x = latent_attention(q_x, k_x, v_x, k_c, v_c, attn_sink[l])
        out_x = mixer_output(px, attn_x, conv_w[l], w_a_out[l], w_b_out[l], w_o[l])

        if not last:
            attn_c = context_attention(q_c, k_c, v_c, attn_sink[l])
            out_c = mixer_output(pc, attn_c, conv_w[l], w_a_out[l], w_b_out[l], w_o[l])
            ctx = ctx + gate_c * out_c
        x = x + gate_x * out_x

    return rms_norm(x, final_norm_w)
```

```python
import functools

import numpy as np
import jax
import jax.numpy as jnp
from jax import lax
from jax.experimental import pallas as pl
from jax.experimental.pallas import tpu as pltpu

D_MODEL = 1024
GRID_W = 64
CONV_WIDTH = 512
CONV_K = 3
N_HEADS = 8
N_KV_HEADS = 2
HEAD_DIM = 64
ATTN_WIDTH = N_HEADS * HEAD_DIM
KV_WIDTH = N_KV_HEADS * HEAD_DIM
BLOCK = 128
ROPE_THETA = 10000.0
ROPE_FREQS = HEAD_DIM // 4
EPS = 1e-6
NEG = -1e30

LANES = 128
SUBLANES = 8
MOD_ROWS = 24
VMEM_LIMIT = 56 * 1024 * 1024

_OFF_B = 0
_OFF_C = _OFF_B + CONV_WIDTH
_OFF_U = _OFF_C + CONV_WIDTH
_OFF_ZA = _OFF_U + CONV_WIDTH
_OFF_Q = _OFF_ZA + CONV_WIDTH
_OFF_K = _OFF_Q + ATTN_WIDTH
_OFF_V = _OFF_K + KV_WIDTH
_OFF_ZB = _OFF_V + KV_WIDTH
_OFF_GA = _OFF_ZB + ATTN_WIDTH
_OFF_GB = _OFF_GA + D_MODEL
IN_COLS = _OFF_GB + D_MODEL

f32 = jnp.float32
bf16 = jnp.bfloat16


def _silu(x):
    return x * jax.nn.sigmoid(x)


def _mod_kernel(c_ref, w_ref, b_ref, o_ref):
    s = _silu(c_ref[...])
    o_ref[0] = jnp.dot(s, w_ref[0], preferred_element_type=f32,
                       precision=lax.Precision.HIGHEST) + b_ref[0]


def _modulation(c_all, w_mod, b_mod):
    depth = w_mod.shape[0]
    n_col = 3 * D_MODEL // D_MODEL
    return pl.pallas_call(
        _mod_kernel,
        grid=(depth, n_col),
        in_specs=[
            pl.BlockSpec((MOD_ROWS, D_MODEL), lambda l, j: (0, 0)),
            pl.BlockSpec((1, D_MODEL, D_MODEL), lambda l, j: (l, 0, j)),
            pl.BlockSpec((1, 1, D_MODEL), lambda l, j: (l, 0, j)),
        ],
        out_specs=pl.BlockSpec((1, MOD_ROWS, D_MODEL), lambda l, j: (l, 0, j)),
        out_shape=jax.ShapeDtypeStruct((depth, MOD_ROWS, 3 * D_MODEL), f32),
        name="modulation",
    )(c_all, w_mod, b_mod.reshape(depth, 1, 3 * D_MODEL))


def _rope(p, cos, sin_lo, sin_hi):
    outs = []
    for j in range(p.shape[1] // LANES):
        x = p[:, j * LANES:(j + 1) * LANES]
        outs.append(x * cos
                    + pltpu.roll(x, LANES - ROPE_FREQS, 1) * sin_lo
                    + pltpu.roll(x, ROPE_FREQS, 1) * sin_hi)
    return outs[0] if len(outs) == 1 else jnp.concatenate(outs, axis=1)


def _proj_kernel(*refs, rope):
    if rope:
        (x_ref, nw_ref, shift_ref, scale_ref, w_ref, cos_ref, slo_ref, shi_ref,
         ba_ref, cu_ref, q_ref, k_ref, v_ref, zb_ref, ga_ref, gb_ref) = refs
    else:
        (x_ref, nw_ref, shift_ref, scale_ref, w_ref,
         ba_ref, cu_ref, q_ref, k_ref, v_ref, zb_ref, ga_ref, gb_ref) = refs
    x = x_ref[...]
    ms = jnp.mean(x * x, axis=-1, keepdims=True)
    y = x * lax.rsqrt(ms + EPS) * nw_ref[...]
    h = (y * (1.0 + scale_ref[0]) + shift_ref[0]).astype(bf16)

    def proj(off, width):
        return jnp.dot(h, w_ref[:, off:off + width], preferred_element_type=f32)

    ba_ref[...] = (proj(_OFF_B, CONV_WIDTH) * _silu(proj(_OFF_ZA, CONV_WIDTH))).astype(bf16)
    cu_ref[...] = (proj(_OFF_C, CONV_WIDTH) * proj(_OFF_U, CONV_WIDTH)).astype(bf16)
    q = proj(_OFF_Q, ATTN_WIDTH)
    kv = proj(_OFF_K, 2 * KV_WIDTH)
    k = kv[:, :KV_WIDTH]
    if rope:
        cos, slo, shi = cos_ref[...], slo_ref[...], shi_ref[...]
        q = _rope(q, cos, slo, shi)
        k = _rope(k, cos, slo, shi)
    q_ref[...] = (q * (HEAD_DIM ** -0.5)).astype(bf16)
    k_ref[...] = k.astype(bf16)
    v_ref[...] = kv[:, KV_WIDTH:].astype(bf16)
    zb_ref[...] = _silu(proj(_OFF_ZB, ATTN_WIDTH)).astype(bf16)
    ga_ref[...] = jax.nn.sigmoid(proj(_OFF_GA, D_MODEL)).astype(bf16)
    gb_ref[...] = jax.nn.sigmoid(proj(_OFF_GB, D_MODEL)).astype(bf16)


def _projection(x2, norm_w, shift, scale, w_in, rope_tabs, *, tm, rows_per_mod, mod_row0):
    n_rows = x2.shape[0]
    grid = (n_rows // tm,)
    if rows_per_mod is None:
        mod_map = lambda t: (mod_row0, 0, 0)
    else:
        tiles_per_mod = rows_per_mod // tm
        mod_map = lambda t: (mod_row0 + t // tiles_per_mod, 0, 0)
    row = lambda w: pl.BlockSpec((tm, w), lambda t: (t, 0))
    in_specs = [
        row(D_MODEL),
        pl.BlockSpec((1, D_MODEL), lambda t: (0, 0)),
        pl.BlockSpec((1, 1, D_MODEL), mod_map),
        pl.BlockSpec((1, 1, D_MODEL), mod_map),
        pl.BlockSpec((D_MODEL, IN_COLS), lambda t: (0, 0)),
    ]
    args = [x2, norm_w.reshape(1, D_MODEL), shift, scale, w_in]
    if rope_tabs is not None:
        tiles_per_mod = rows_per_mod // tm
        in_specs += [pl.BlockSpec((tm, LANES), lambda t: (t % tiles_per_mod, 0))] * 3
        args += list(rope_tabs)
    widths = (CONV_WIDTH, CONV_WIDTH, ATTN_WIDTH, KV_WIDTH, KV_WIDTH, ATTN_WIDTH,
              D_MODEL, D_MODEL)
    return pl.pallas_call(
        functools.partial(_proj_kernel, rope=rope_tabs is not None),
        grid=grid,
        in_specs=in_specs,
        out_specs=[row(w) for w in widths],
        out_shape=[jax.ShapeDtypeStruct((n_rows, w), bf16) for w in widths],
        compiler_params=pltpu.CompilerParams(
            dimension_semantics=("arbitrary",), vmem_limit_bytes=VMEM_LIMIT),
        name="projection_rope" if rope_tabs is not None else "projection",
    )(*args)


def _head_variants(a):
    a32 = a.astype(f32)
    sw = pltpu.roll(a32, HEAD_DIM, 1)
    low = lax.broadcasted_iota(jnp.int32, a32.shape, 1) < HEAD_DIM
    zero = jnp.zeros_like(a32)
    return [jnp.where(low, a32, zero).astype(bf16), jnp.where(low, zero, sw).astype(bf16),
            jnp.where(low, sw, zero).astype(bf16), jnp.where(low, zero, a32).astype(bf16)]


def _mix_kernel(*refs, tq, has_local, final):
    it = iter(refs)
    sink_ref = next(it)
    x_ref, gate_ref, ba_ref, cu_ref, cup_ref, cun_ref = (next(it) for _ in range(6))
    q_ref, zb_ref, ga_ref, gb_ref = (next(it) for _ in range(4))
    if has_local:
        k_ref, kp_ref, kn_ref, v_ref, vp_ref, vn_ref = (next(it) for _ in range(6))
    kc_ref, vc_ref = next(it), next(it)
    cw_ref, wa_ref, wb_ref, wo_ref = (next(it) for _ in range(4))
    if final:
        fw_ref = next(it)
    o_ref = next(it)
    cu_s, yb_s, kc_s, vc_s = (next(it) for _ in range(4))
    if has_local:
        kl_s, vl_s = next(it), next(it)

    i = pl.program_id(1)
    n_i = pl.num_programs(1)
    nblk = tq // BLOCK

    cu_s[SUBLANES:SUBLANES + tq, :] = cu_ref[...].astype(f32)
    halo_zero = jnp.zeros((SUBLANES, CONV_WIDTH), f32)
    cu_s[0:SUBLANES, :] = jnp.where(i > 0, cup_ref[...].astype(f32), halo_zero)
    cu_s[SUBLANES + tq:2 * SUBLANES + tq, :] = jnp.where(
        i < n_i - 1, cun_ref[...].astype(f32), halo_zero)
    cw = cw_ref[...]
    conv = (cu_s[SUBLANES - 1:SUBLANES - 1 + tq, :] * cw[0:1]
            + cu_s[SUBLANES:SUBLANES + tq, :] * cw[1:2]
            + cu_s[SUBLANES + 1:SUBLANES + 1 + tq, :] * cw[2:3])
    ya = (ba_ref[...].astype(f32) * conv).astype(bf16)

    for n, a in enumerate(_head_variants(kc_ref[...])):
        kc_s[n] = a
    for n, a in enumerate(_head_variants(vc_ref[...])):
        vc_s[n] = a
    if has_local:
        kcat = jnp.concatenate([kp_ref[...], k_ref[...], kn_ref[...]], axis=0)
        vcat = jnp.concatenate([vp_ref[...], v_ref[...], vn_ref[...]], axis=0)
        for n, a in enumerate(_head_variants(kcat)):
            kl_s[n] = a
        for n, a in enumerate(_head_variants(vcat)):
            vl_s[n] = a
        qi = lax.broadcasted_iota(jnp.int32, (2 * BLOCK, BLOCK), 0) % BLOCK
        ci = lax.broadcasted_iota(jnp.int32, (2 * BLOCK, BLOCK), 1)

    row_top = lax.broadcasted_iota(jnp.int32, (2 * BLOCK, 1), 0) < BLOCK
    nt = (((1,), (1,)), ((), ()))
    for j in range(nblk):
        rows = slice(j * BLOCK, (j + 1) * BLOCK)
        if has_local:
            far = 2 * BLOCK
            off_prev = jnp.where(i == 0, far, 0) if j == 0 else 0
            off_next = jnp.where(i == n_i - 1, far, 0) if j == nblk - 1 else 0
            m_prev = ci >= qi + off_prev
            m_next = ci + off_next <= qi
            krows = slice(j * BLOCK, (j + 3) * BLOCK)
        for kh in range(N_KV_HEADS):
            q2 = jnp.concatenate(
                [q_ref[rows, (2 * kh) * LANES:(2 * kh + 1) * LANES],
                 q_ref[rows, (2 * kh + 1) * LANES:(2 * kh + 2) * LANES]], axis=0)
            acc = jnp.zeros((2 * BLOCK, LANES), f32)
            for e in range(2):
                var = 2 * kh + e
                h_top = 4 * kh + e
                sk = jnp.where(row_top, sink_ref[h_top], sink_ref[h_top + 2])
                s_ctx = lax.dot_general(q2, kc_s[var], nt, preferred_element_type=f32)
                m = jnp.maximum(jnp.max(s_ctx, axis=1, keepdims=True), sk)
                if has_local:
                    s_loc = lax.dot_general(q2, kl_s[var, krows, :], nt,
                                            preferred_element_type=f32)
                    s_loc = jnp.concatenate(
                        [jnp.where(m_prev, s_loc[:, :BLOCK], NEG),
                         s_loc[:, BLOCK:2 * BLOCK],
                         jnp.where(m_next, s_loc[:, 2 * BLOCK:], NEG)], axis=1)
                    m = jnp.maximum(m, jnp.max(s_loc, axis=1, keepdims=True))
                p_ctx = jnp.exp(s_ctx - m)
                l = jnp.sum(p_ctx, axis=1, keepdims=True) + jnp.exp(sk - m)
                o = jnp.dot(p_ctx.astype(bf16), vc_s[var], preferred_element_type=f32)
                if has_local:
                    p_loc = jnp.exp(s_loc - m)
                    l = l + jnp.sum(p_loc, axis=1, keepdims=True)
                    o = o + jnp.dot(p_loc.astype(bf16), vl_s[var, krows, :],
                                    preferred_element_type=f32)
                acc = acc + o * (1.0 / l)
            for c in range(2):
                lanes = slice((2 * kh + c) * LANES, (2 * kh + c + 1) * LANES)
                yb_s[rows, lanes] = (acc[c * BLOCK:(c + 1) * BLOCK]
                                     * zb_ref[rows, lanes].astype(f32)).astype(bf16)

    y = (jnp.dot(ya, wa_ref[...], preferred_element_type=f32) * ga_ref[...].astype(f32)
         + jnp.dot(yb_s[...], wb_ref[...], preferred_element_type=f32)
         * gb_ref[...].astype(f32))
    out = jnp.dot(y.astype(bf16), wo_ref[...], preferred_element_type=f32)
    xn = x_ref[...] + gate_ref[0] * out
    if final:
        ms = jnp.mean(xn * xn, axis=-1, keepdims=True)
        xn = xn * lax.rsqrt(ms + EPS) * fw_ref[...]
    o_ref[...] = xn


def _mixer(x2, gate, sink, proj_out, kc, vc, conv_w, wa, wb, wo, final_w, *,
           seq, tq, has_local, gate_row0, gate_per_batch):
    ba, cu, q, k, v, zb, ga, gb = proj_out
    n_rows = x2.shape[0]
    batch = n_rows // seq
    ctx_len = kc.shape[0] // batch
    n_i = seq // tq
    final = final_w is not None

    def row(w):
        return pl.BlockSpec((tq, w), lambda b, i: (b * n_i + i, 0))

    def halo(rows_blk, w, side):
        per_tile = tq // rows_blk
        per_seq = seq // rows_blk
        if side < 0:
            return pl.BlockSpec((rows_blk, w), lambda b, i: (
                b * per_seq + jnp.maximum(i * per_tile - 1, 0), 0))
        return pl.BlockSpec((rows_blk, w), lambda b, i: (
            b * per_seq + jnp.minimum((i + 1) * per_tile, per_seq - 1), 0))

    def whole(shape):
        return pl.BlockSpec(shape, lambda b, i: (0,) * len(shape))

    if gate_per_batch:
        gate_spec = pl.BlockSpec((1, 1, D_MODEL), lambda b, i: (gate_row0 + b, 0, 0))
    else:
        gate_spec = pl.BlockSpec((1, 1, D_MODEL), lambda b, i: (gate_row0, 0, 0))

    in_specs = [pl.BlockSpec(memory_space=pltpu.SMEM),
                row(D_MODEL), gate_spec,
                row(CONV_WIDTH), row(CONV_WIDTH),
                halo(SUBLANES, CONV_WIDTH, -1), halo(SUBLANES, CONV_WIDTH, +1),
                row(ATTN_WIDTH), row(ATTN_WIDTH), row(D_MODEL), row(D_MODEL)]
    args = [sink, x2, gate, ba, cu, cu, cu, q, zb, ga, gb]
    if has_local:
        in_specs += [row(KV_WIDTH), halo(BLOCK, KV_WIDTH, -1), halo(BLOCK, KV_WIDTH, +1)] * 2
        args += [k, k, k, v, v, v]
    ctx_spec = pl.BlockSpec((ctx_len, KV_WIDTH), lambda b, i: (b, 0))
    in_specs += [ctx_spec, ctx_spec,
                 whole((CONV_K, CONV_WIDTH)), whole((CONV_WIDTH, D_MODEL)),
                 whole((ATTN_WIDTH, D_MODEL)), whole((D_MODEL, D_MODEL))]
    args += [kc, vc, conv_w, wa, wb, wo]
    if final:
        in_specs.append(whole((1, D_MODEL)))
        args.append(final_w.reshape(1, D_MODEL))

    scratch = [pltpu.VMEM((tq + 2 * SUBLANES, CONV_WIDTH), f32),
               pltpu.VMEM((tq, ATTN_WIDTH), bf16),
               pltpu.VMEM((4, ctx_len, KV_WIDTH), bf16),
               pltpu.VMEM((4, ctx_len, KV_WIDTH), bf16)]
    if has_local:
        scratch += [pltpu.VMEM((4, tq + 2 * BLOCK, KV_WIDTH), bf16)] * 2

    return pl.pallas_call(
        functools.partial(_mix_kernel, tq=tq, has_local=has_local, final=final),
        grid=(batch, n_i),
        in_specs=in_specs,
        out_specs=row(D_MODEL),
        out_shape=jax.ShapeDtypeStruct((n_rows, D_MODEL), f32),
        scratch_shapes=scratch,
        compiler_params=pltpu.CompilerParams(
            dimension_semantics=("arbitrary", "arbitrary"), vmem_limit_bytes=VMEM_LIMIT),
        name="mixer_latent" if has_local else "mixer_context",
    )(*args)


def _rope_tables(seq):
    pos = np.arange(seq)
    inv_freq = jnp.asarray(ROPE_THETA, f32) ** (-jnp.arange(ROPE_FREQS, dtype=f32) / ROPE_FREQS)
    lane = np.arange(LANES)
    use_col = (lane % HEAD_DIM) >= HEAD_DIM // 2
    upper = (lane % (2 * ROPE_FREQS)) >= ROPE_FREQS
    coord = jnp.where(use_col[None, :], (pos % GRID_W)[:, None], (pos // GRID_W)[:, None])
    ang = coord.astype(f32) * inv_freq[lane % ROPE_FREQS][None, :]
    cos, sin = jnp.cos(ang), jnp.sin(ang)
    zero = jnp.zeros_like(sin)
    sin_lo = jnp.where(upper[None, :], zero, -sin)
    sin_hi = jnp.where(upper[None, :], sin, zero)
    return cos, sin_lo, sin_hi


def kernel(x, c, ctx, c_ctx, norm_w, w_mod, b_mod, w_in, conv_w, w_a_out, w_b_out,
           attn_sink, w_o, final_norm_w):
    batch, seq, _ = x.shape
    ctx_len = ctx.shape[1]
    depth = w_in.shape[0]
    assert batch + 1 <= MOD_ROWS and seq % 512 == 0 and ctx_len % BLOCK == 0

    c_all = jnp.zeros((MOD_ROWS, D_MODEL), f32).at[:batch].set(c).at[batch].set(c_ctx)
    mod = _modulation(c_all, w_mod, b_mod).reshape(depth, MOD_ROWS, 3, 1, D_MODEL)
    rope_tabs = _rope_tables(seq)

    w_in_b = w_in.astype(bf16)
    wa_b, wb_b, wo_b = w_a_out.astype(bf16), w_b_out.astype(bf16), w_o.astype(bf16)

    x2 = x.reshape(batch * seq, D_MODEL)
    c2 = ctx.reshape(batch * ctx_len, D_MODEL)
    for l in range(depth):
        last = l == depth - 1
        shift, scale, gate = mod[l, :, 0], mod[l, :, 1], mod[l, :, 2]
        pc = _projection(c2, norm_w[l], shift, scale, w_in_b[l], None,
                         tm=512, rows_per_mod=None, mod_row0=batch)
        px = _projection(x2, norm_w[l], shift, scale, w_in_b[l], rope_tabs,
                         tm=512, rows_per_mod=seq, mod_row0=0)
        kc, vc = pc[3], pc[4]
        x2 = _mixer(x2, gate, attn_sink[l], px, kc, vc, conv_w[l], wa_b[l], wb_b[l], wo_b[l],
                    final_norm_w if last else None,
                    seq=seq, tq=256, has_local=True, gate_row0=0, gate_per_batch=True)
        if not last:
            c2 = _mixer(c2, gate, attn_sink[l], pc, kc, vc, conv_w[l], wa_b[l], wb_b[l],
                        wo_b[l], None,
                        seq=ctx_len, tq=ctx_len, has_local=False, gate_row0=batch,
                        gate_per_batch=False)
    return x2.reshape(batch, seq, D_MODEL)
```

```python
import functools
import math

import numpy as np
import jax
import jax.numpy as jnp
from jax import lax
from jax.experimental import pallas as pl
from jax.experimental.pallas import tpu as pltpu

D_MODEL = 1024
GRID_W = 64
CONV_WIDTH = 512
CONV_K = 3
N_HEADS = 8
N_KV_HEADS = 2
HEAD_DIM = 64
ATTN_WIDTH = N_HEADS * HEAD_DIM
KV_WIDTH = N_KV_HEADS * HEAD_DIM
BLOCK = 128
ROPE_THETA = 10000.0
ROPE_FREQS = HEAD_DIM // 4
EPS = 1e-6
NEG = -1e30
LOG2E = math.log2(math.e)

LANES = 128
SUBLANES = 8
MOD_ROWS = 24
VMEM_LIMIT = 56 * 1024 * 1024
N_SLAB = ATTN_WIDTH // LANES
HEADS_PER_SLAB = LANES // HEAD_DIM

_OFF_B = 0
_OFF_C = _OFF_B + CONV_WIDTH
_OFF_U = _OFF_C + CONV_WIDTH
_OFF_ZA = _OFF_U + CONV_WIDTH
_OFF_Q = _OFF_ZA + CONV_WIDTH
_OFF_K = _OFF_Q + ATTN_WIDTH
_OFF_V = _OFF_K + KV_WIDTH
_OFF_ZB = _OFF_V + KV_WIDTH
_OFF_GA = _OFF_ZB + ATTN_WIDTH
_OFF_GB = _OFF_GA + D_MODEL
IN_COLS = _OFF_GB + D_MODEL

f32 = jnp.float32
bf16 = jnp.bfloat16


def _silu(x):
    return x * jax.nn.sigmoid(x)


def _mod_kernel(c_ref, w_ref, b_ref, o_ref):
    s = _silu(c_ref[...])
    o_ref[0] = jnp.dot(s, w_ref[0], preferred_element_type=f32,
                       precision=lax.Precision.HIGHEST) + b_ref[0]


def _modulation(c_all, w_mod, b_mod):
    depth = w_mod.shape[0]
    n_col = 3 * D_MODEL // D_MODEL
    return pl.pallas_call(
        _mod_kernel,
        grid=(depth, n_col),
        in_specs=[
            pl.BlockSpec((MOD_ROWS, D_MODEL), lambda l, j: (0, 0)),
            pl.BlockSpec((1, D_MODEL, D_MODEL), lambda l, j: (l, 0, j)),
            pl.BlockSpec((1, 1, D_MODEL), lambda l, j: (l, 0, j)),
        ],
        out_specs=pl.BlockSpec((1, MOD_ROWS, D_MODEL), lambda l, j: (l, 0, j)),
        out_shape=jax.ShapeDtypeStruct((depth, MOD_ROWS, 3 * D_MODEL), f32),
        name="modulation",
    )(c_all, w_mod, b_mod.reshape(depth, 1, 3 * D_MODEL))


def _rope(p, cos, sin_lo, sin_hi):
    outs = []
    for j in range(p.shape[1] // LANES):
        x = p[:, j * LANES:(j + 1) * LANES]
        outs.append(x * cos
                    + pltpu.roll(x, LANES - ROPE_FREQS, 1) * sin_lo
                    + pltpu.roll(x, ROPE_FREQS, 1) * sin_hi)
    return outs[0] if len(outs) == 1 else jnp.concatenate(outs, axis=1)


def _with_swapped_halves(a):
    return jnp.concatenate([a, pltpu.roll(a, HEAD_DIM, 1)], axis=1)


def _proj_kernel(*refs, rope):
    if rope:
        (x_ref, nw_ref, shift_ref, scale_ref, w_ref, cos_ref, slo_ref, shi_ref,
         ba_ref, cu_ref, q_ref, k_ref, v_ref, zb_ref, ga_ref, gb_ref) = refs
    else:
        (x_ref, nw_ref, shift_ref, scale_ref, w_ref,
         ba_ref, cu_ref, q_ref, k_ref, v_ref, zb_ref, ga_ref, gb_ref) = refs
    x = x_ref[...]
    ms = jnp.mean(x * x, axis=-1, keepdims=True)
    y = x * lax.rsqrt(ms + EPS) * nw_ref[...]
    h = (y * (1.0 + scale_ref[0]) + shift_ref[0]).astype(bf16)

    def proj(off, width):
        return jnp.dot(h, w_ref[:, off:off + width], preferred_element_type=f32)

    ba_ref[...] = (proj(_OFF_B, CONV_WIDTH) * _silu(proj(_OFF_ZA, CONV_WIDTH))).astype(bf16)
    cu_ref[...] = (proj(_OFF_C, CONV_WIDTH) * proj(_OFF_U, CONV_WIDTH)).astype(bf16)
    q = proj(_OFF_Q, ATTN_WIDTH)
    kv = proj(_OFF_K, 2 * KV_WIDTH)
    k = kv[:, :KV_WIDTH]
    if rope:
        cos, slo, shi = cos_ref[...], slo_ref[...], shi_ref[...]
        q = _rope(q, cos, slo, shi)
        k = _rope(k, cos, slo, shi)
    q_ref[...] = (q * (HEAD_DIM ** -0.5 * LOG2E)).astype(bf16)
    k_ref[...] = _with_swapped_halves(k).astype(bf16)
    v_ref[...] = _with_swapped_halves(kv[:, KV_WIDTH:]).astype(bf16)
    zb_ref[...] = _silu(proj(_OFF_ZB, ATTN_WIDTH)).astype(bf16)
    ga_ref[...] = jax.nn.sigmoid(proj(_OFF_GA, D_MODEL)).astype(bf16)
    gb_ref[...] = jax.nn.sigmoid(proj(_OFF_GB, D_MODEL)).astype(bf16)


def _projection(x2, norm_w, shift, scale, w_in, rope_tabs, *, tm, rows_per_mod, mod_row0):
    n_rows = x2.shape[0]
    grid = (n_rows // tm,)
    if rows_per_mod is None:
        mod_map = lambda t: (mod_row0, 0, 0)
    else:
        tiles_per_mod = rows_per_mod // tm
        mod_map = lambda t: (mod_row0 + t // tiles_per_mod, 0, 0)
    row = lambda w: pl.BlockSpec((tm, w), lambda t: (t, 0))
    in_specs = [
        row(D_MODEL),
        pl.BlockSpec((1, D_MODEL), lambda t: (0, 0)),
        pl.BlockSpec((1, 1, D_MODEL), mod_map),
        pl.BlockSpec((1, 1, D_MODEL), mod_map),
        pl.BlockSpec((D_MODEL, IN_COLS), lambda t: (0, 0)),
    ]
    args = [x2, norm_w.reshape(1, D_MODEL), shift, scale, w_in]
    if rope_tabs is not None:
        tiles_per_mod = rows_per_mod // tm
        in_specs += [pl.BlockSpec((tm, LANES), lambda t: (t % tiles_per_mod, 0))] * 3
        args += list(rope_tabs)
    widths = (CONV_WIDTH, CONV_WIDTH, ATTN_WIDTH, 2 * KV_WIDTH, 2 * KV_WIDTH, ATTN_WIDTH,
              D_MODEL, D_MODEL)
    return pl.pallas_call(
        functools.partial(_proj_kernel, rope=rope_tabs is not None),
        grid=grid,
        in_specs=in_specs,
        out_specs=[row(w) for w in widths],
        out_shape=[jax.ShapeDtypeStruct((n_rows, w), bf16) for w in widths],
        compiler_params=pltpu.CompilerParams(
            dimension_semantics=("arbitrary",), vmem_limit_bytes=VMEM_LIMIT),
        name="projection_rope" if rope_tabs is not None else "projection",
    )(*args)


def _kv_variants(a2):
    a, sw = a2[:, :LANES], a2[:, LANES:]
    low = lax.broadcasted_iota(jnp.int32, a.shape, 1) < HEAD_DIM
    z = jnp.zeros_like(a)
    return [jnp.where(low, a, z), jnp.where(low, z, sw),
            jnp.where(low, sw, z), jnp.where(low, z, a)]


def _mix_kernel(*refs, tq, has_local, final):
    it = iter(refs)
    sink_ref = next(it)
    x_ref, gate_ref, ba_ref, cu_ref, cup_ref, cun_ref = (next(it) for _ in range(6))
    q_ref, zb_ref, ga_ref, gb_ref = (next(it) for _ in range(4))
    if has_local:
        k_ref, kp_ref, kn_ref, v_ref, vp_ref, vn_ref = (next(it) for _ in range(6))
    kc_ref, vc_ref = next(it), next(it)
    cw_ref, wa_ref, wb_ref, wo_ref = (next(it) for _ in range(4))
    if final:
        fw_ref = next(it)
    o_ref = next(it)
    cu_s, yb_s, k_s, vm_s, s_s, p_s, z_s = (next(it) for _ in range(7))
    if has_local:
        bias_s = next(it)

    i = pl.program_id(1)
    n_i = pl.num_programs(1)
    nblk = tq // BLOCK
    ctx_len = kc_ref.shape[0]
    n_cc = ctx_len // LANES
    n_loc = 3 * BLOCK if has_local else 0
    nk = ctx_len + n_loc
    n_slot = vm_s.shape[0]
    units_per_blk = N_SLAB * HEADS_PER_SLAB

    cu_s[SUBLANES:SUBLANES + tq, :] = cu_ref[...].astype(f32)
    halo_zero = jnp.zeros((SUBLANES, CONV_WIDTH), f32)
    cu_s[0:SUBLANES, :] = jnp.where(i > 0, cup_ref[...].astype(f32), halo_zero)
    cu_s[SUBLANES + tq:2 * SUBLANES + tq, :] = jnp.where(
        i < n_i - 1, cun_ref[...].astype(f32), halo_zero)
    cw = cw_ref[...]
    conv = (cu_s[SUBLANES - 1:SUBLANES - 1 + tq, :] * cw[0:1]
            + cu_s[SUBLANES:SUBLANES + tq, :] * cw[1:2]
            + cu_s[SUBLANES + 1:SUBLANES + 1 + tq, :] * cw[2:3])
    ya = (ba_ref[...].astype(f32) * conv).astype(bf16)

    @pl.when(i == 0)
    def _():
        kvar = _kv_variants(kc_ref[...])
        vvar = _kv_variants(vc_ref[...])
        low = lax.broadcasted_iota(jnp.int32, (nk, LANES), 1) < HEAD_DIM
        ones_on = [jnp.where(low, 1.0, 0.0).astype(bf16), jnp.where(low, 0.0, 1.0).astype(bf16)]
        for n in range(4):
            k_s[n, 0:ctx_len, :] = kvar[n]
            if has_local:
                k_s[n, ctx_len + tq + 2 * BLOCK:, :] = kvar[n]
        for slot in range(n_slot):
            for kh in range(N_KV_HEADS):
                for e in range(HEADS_PER_SLAB):
                    vm_s[slot, kh, e * nk:e * nk + ctx_len, 0:LANES] = vvar[2 * kh + e]
                    vm_s[slot, kh, e * nk:(e + 1) * nk, LANES:] = ones_on[e]

    if has_local:
        assert nblk <= 2, "a middle block's keys are not one contiguous window of k_s"
        r0 = ctx_len
        for src in (kp_ref, k_ref, kn_ref):
            for n, a in enumerate(_kv_variants(src[...])):
                k_s[n, r0:r0 + src.shape[0], :] = a
            r0 += src.shape[0]
        for t in range(nblk + 2):
            if t == 0:
                blk = vp_ref[...]
            elif t == nblk + 1:
                blk = vn_ref[...]
            else:
                blk = v_ref[(t - 1) * BLOCK:t * BLOCK, :]
            vvar = _kv_variants(blk)
            for j in range(nblk):
                b = t - j
                if 0 <= b <= 2:
                    for kh in range(N_KV_HEADS):
                        for e in range(HEADS_PER_SLAB):
                            r = e * nk + ctx_len + b * BLOCK
                            vm_s[j % n_slot, kh, r:r + BLOCK, 0:LANES] = vvar[2 * kh + e]
        qi = lax.broadcasted_iota(jnp.int32, (BLOCK, BLOCK), 0)
        ci = lax.broadcasted_iota(jnp.int32, (BLOCK, BLOCK), 1)
        tri_prev = jnp.where(ci >= qi, 0.0, NEG)
        tri_next = jnp.where(ci <= qi, 0.0, NEG)
        bias_s[0] = tri_prev + jnp.where(i == 0, NEG, 0.0)
        bias_s[1] = tri_prev
        bias_s[2] = tri_next
        bias_s[3] = tri_next + jnp.where(i == n_i - 1, NEG, 0.0)

    ctx_names = ["c%d" % n for n in range(n_cc)]
    p_names = ctx_names + (["prev", "mid", "next"] if has_local else [])

    def key_window(j):
        if not has_local:
            return 0, ctx_names
        if j == 0:
            return 0, ctx_names + ["prev", "mid", "next"]
        return ctx_len + j * BLOCK, ["prev", "mid", "next"] + ctx_names

    nt = (((1,), (1,)), ((), ()))

    def stage_a(j, c, e):
        r0, _ = key_window(j)
        q1 = q_ref[j * BLOCK:(j + 1) * BLOCK, c * LANES:(c + 1) * LANES]
        kv_head = c * HEADS_PER_SLAB // (N_HEADS // N_KV_HEADS)
        s_s[j * units_per_blk + c * HEADS_PER_SLAB + e] = lax.dot_general(
            q1, k_s[2 * kv_head + e, r0:r0 + nk, :], nt, preferred_element_type=f32)

    def stage_b(j, c, e):
        u = j * units_per_blk + c * HEADS_PER_SLAB + e
        slab = j * N_SLAB + c
        sk = sink_ref[c * HEADS_PER_SLAB + e] * LOG2E
        _, names = key_window(j)
        pieces = []
        for n, name in enumerate(names):
            piece = s_s[u, :, n * LANES:(n + 1) * LANES]
            if name == "prev":
                piece = piece + bias_s[0 if j == 0 else 1]
            elif name == "next":
                piece = piece + bias_s[3 if j == nblk - 1 else 2]
            pieces.append(piece)
        m = jnp.maximum(jnp.max(functools.reduce(jnp.maximum, pieces), axis=1, keepdims=True),
                        sk)
        for name, piece in zip(names, pieces):
            col = e * nk + p_names.index(name) * LANES
            p_s[slab, :, col:col + LANES] = jnp.exp2(piece - m).astype(bf16)
        z_s[slab, :, e * HEAD_DIM:(e + 1) * HEAD_DIM] = jnp.broadcast_to(
            jnp.exp2(sk - m), (BLOCK, HEAD_DIM))

    def stage_c(j, c):
        slab = j * N_SLAB + c
        kv_head = c * HEADS_PER_SLAB // (N_HEADS // N_KV_HEADS)
        res = jnp.dot(p_s[slab], vm_s[j % n_slot, kv_head], preferred_element_type=f32)
        attn = res[:, :LANES] / (res[:, LANES:] + z_s[slab])
        rows = slice(j * BLOCK, (j + 1) * BLOCK)
        lanes = slice(c * LANES, (c + 1) * LANES)
        yb_s[rows, lanes] = (attn * zb_ref[rows, lanes].astype(f32)).astype(bf16)

    y = None
    for t in range(nblk + 2):
        if t == 1:
            y = jnp.dot(ya, wa_ref[...], preferred_element_type=f32) * ga_ref[...].astype(f32)
        for c in range(N_SLAB):
            for e in range(HEADS_PER_SLAB):
                if t < nblk:
                    stage_a(t, c, e)
                if 0 <= t - 1 < nblk:
                    stage_b(t - 1, c, e)
            if 0 <= t - 2 < nblk:
                stage_c(t - 2, c)

    y = y + (jnp.dot(yb_s[...], wb_ref[...], preferred_element_type=f32)
             * gb_ref[...].astype(f32))
    out = jnp.dot(y.astype(bf16), wo_ref[...], preferred_element_type=f32)
    xn = x_ref[...] + gate_ref[0] * out
    if final:
        ms = jnp.mean(xn * xn, axis=-1, keepdims=True)
        xn = xn * lax.rsqrt(ms + EPS) * fw_ref[...]
    o_ref[...] = xn


def _mixer(x2, gate, sink, proj_out, kc, vc, conv_w, wa, wb, wo, final_w, *,
           seq, tq, has_local, gate_row0, gate_per_batch):
    ba, cu, q, k, v, zb, ga, gb = proj_out
    n_rows = x2.shape[0]
    batch = n_rows // seq
    ctx_len = kc.shape[0] // batch
    n_i = seq // tq
    nblk = tq // BLOCK
    final = final_w is not None

    def row(w):
        return pl.BlockSpec((tq, w), lambda b, i: (b * n_i + i, 0))

    def halo(rows_blk, w, side):
        per_tile = tq // rows_blk
        per_seq = seq // rows_blk
        if side < 0:
            return pl.BlockSpec((rows_blk, w), lambda b, i: (
                b * per_seq + jnp.maximum(i * per_tile - 1, 0), 0))
        return pl.BlockSpec((rows_blk, w), lambda b, i: (
            b * per_seq + jnp.minimum((i + 1) * per_tile, per_seq - 1), 0))

    def whole(shape):
        return pl.BlockSpec(shape, lambda b, i: (0,) * len(shape))

    if gate_per_batch:
        gate_spec = pl.BlockSpec((1, 1, D_MODEL), lambda b, i: (gate_row0 + b, 0, 0))
    else:
        gate_spec = pl.BlockSpec((1, 1, D_MODEL), lambda b, i: (gate_row0, 0, 0))

    kvw = 2 * KV_WIDTH
    in_specs = [pl.BlockSpec(memory_space=pltpu.SMEM),
                row(D_MODEL), gate_spec,
                row(CONV_WIDTH), row(CONV_WIDTH),
                halo(SUBLANES, CONV_WIDTH, -1), halo(SUBLANES, CONV_WIDTH, +1),
                row(ATTN_WIDTH), row(ATTN_WIDTH), row(D_MODEL), row(D_MODEL)]
    args = [sink, x2, gate, ba, cu, cu, cu, q, zb, ga, gb]
    if has_local:
        in_specs += [row(kvw), halo(BLOCK, kvw, -1), halo(BLOCK, kvw, +1)] * 2
        args += [k, k, k, v, v, v]
    ctx_spec = pl.BlockSpec((ctx_len, kvw), lambda b, i: (b, 0))
    in_specs += [ctx_spec, ctx_spec,
                 whole((CONV_K, CONV_WIDTH)), whole((CONV_WIDTH, D_MODEL)),
                 whole((ATTN_WIDTH, D_MODEL)), whole((D_MODEL, D_MODEL))]
    args += [kc, vc, conv_w, wa, wb, wo]
    if final:
        in_specs.append(whole((1, D_MODEL)))
        args.append(final_w.reshape(1, D_MODEL))

    nk = ctx_len + (3 * BLOCK if has_local else 0)
    k_rows = 2 * ctx_len + tq + 2 * BLOCK if has_local else ctx_len
    n_slot = 2 if (has_local and nblk > 1) else 1
    scratch = [pltpu.VMEM((tq + 2 * SUBLANES, CONV_WIDTH), f32),
               pltpu.VMEM((tq, ATTN_WIDTH), bf16),
               pltpu.VMEM((2 * N_KV_HEADS, k_rows, LANES), bf16),
               pltpu.VMEM((n_slot, N_KV_HEADS, HEADS_PER_SLAB * nk, 2 * LANES), bf16),
               pltpu.VMEM((nblk * N_HEADS, BLOCK, nk), f32),
               pltpu.VMEM((nblk * N_SLAB, BLOCK, HEADS_PER_SLAB * nk), bf16),
               pltpu.VMEM((nblk * N_SLAB, BLOCK, LANES), f32)]
    if has_local:
        scratch += [pltpu.VMEM((4, BLOCK, BLOCK), f32)]

    return pl.pallas_call(
        functools.partial(_mix_kernel, tq=tq, has_local=has_local, final=final),
        grid=(batch, n_i),
        in_specs=in_specs,
        out_specs=row(D_MODEL),
        out_shape=jax.ShapeDtypeStruct((n_rows, D_MODEL), f32),
        scratch_shapes=scratch,
        compiler_params=pltpu.CompilerParams(
            dimension_semantics=("arbitrary", "arbitrary"), vmem_limit_bytes=VMEM_LIMIT),
        name="mixer_latent" if has_local else "mixer_context",
    )(*args)


def _rope_tables(seq):
    pos = np.arange(seq)
    inv_freq = jnp.asarray(ROPE_THETA, f32) ** (-jnp.arange(ROPE_FREQS, dtype=f32) / ROPE_FREQS)
    lane = np.arange(LANES)
    use_col = (lane % HEAD_DIM) >= HEAD_DIM // 2
    upper = (lane % (2 * ROPE_FREQS)) >= ROPE_FREQS
    coord = jnp.where(use_col[None, :], (pos % GRID_W)[:, None], (pos // GRID_W)[:, None])
    ang = coord.astype(f32) * inv_freq[lane % ROPE_FREQS][None, :]
    cos, sin = jnp.cos(ang), jnp.sin(ang)
    zero = jnp.zeros_like(sin)
    sin_lo = jnp.where(upper[None, :], zero, -sin)
    sin_hi = jnp.where(upper[None, :], sin, zero)
    return cos, sin_lo, sin_hi


def kernel(x, c, ctx, c_ctx, norm_w, w_mod, b_mod, w_in, conv_w, w_a_out, w_b_out,
           attn_sink, w_o, final_norm_w):
    batch, seq, _ = x.shape
    ctx_len = ctx.shape[1]
    depth = w_in.shape[0]
    assert batch + 1 <= MOD_ROWS and seq % 512 == 0 and ctx_len % BLOCK == 0

    c_all = jnp.zeros((MOD_ROWS, D_MODEL), f32).at[:batch].set(c).at[batch].set(c_ctx)
    mod = _modulation(c_all, w_mod, b_mod).reshape(depth, MOD_ROWS, 3, 1, D_MODEL)
    rope_tabs = _rope_tables(seq)

    w_in_b = w_in.astype(bf16)
    wa_b, wb_b, wo_b = w_a_out.astype(bf16), w_b_out.astype(bf16), w_o.astype(bf16)

    x2 = x.reshape(batch * seq, D_MODEL)
    c2 = ctx.reshape(batch * ctx_len, D_MODEL)
    for l in range(depth):
        last = l == depth - 1
        shift, scale, gate = mod[l, :, 0], mod[l, :, 1], mod[l, :, 2]
        pc = _projection(c2, norm_w[l], shift, scale, w_in_b[l], None,
                         tm=512, rows_per_mod=None, mod_row0=batch)
        px = _projection(x2, norm_w[l], shift, scale, w_in_b[l], rope_tabs,
                         tm=512, rows_per_mod=seq, mod_row0=0)
        kc, vc = pc[3], pc[4]
        x2 = _mixer(x2, gate, attn_sink[l], px, kc, vc, conv_w[l], wa_b[l], wb_b[l], wo_b[l],
                    final_norm_w if last else None,
                    seq=seq, tq=256, has_local=True, gate_row0=0, gate_per_batch=True)
        if not last:
            c2 = _mixer(c2, gate, attn_sink[l], pc, kc, vc, conv_w[l], wa_b[l], wb_b[l],
                        wo_b[l], None,
                        seq=ctx_len, tq=ctx_len, has_local=False, gate_row0=batch,
                        gate_per_batch=False)
    return x2.reshape(batch, seq, D_MODEL)
```

```python
import functools
import math

import numpy as np
import jax
import jax.numpy as jnp
from jax import lax
from jax.experimental import pallas as pl
from jax.experimental.pallas import tpu as pltpu

D_MODEL = 1024
GRID_W = 64
CONV_WIDTH = 512
CONV_K = 3
N_HEADS = 8
N_KV_HEADS = 2
HEAD_DIM = 64
ATTN_WIDTH = N_HEADS * HEAD_DIM
KV_WIDTH = N_KV_HEADS * HEAD_DIM
BLOCK = 128
ROPE_THETA = 10000.0
ROPE_FREQS = HEAD_DIM // 4
EPS = 1e-6
NEG = -1e30
LOG2E = math.log2(math.e)

LANES = 128
SUBLANES = 8
MOD_ROWS = 24
VMEM_LIMIT = 56 * 1024 * 1024
N_SLAB = ATTN_WIDTH // LANES
HEADS_PER_SLAB = LANES // HEAD_DIM

_OFF_B = 0
_OFF_C = _OFF_B + CONV_WIDTH
_OFF_U = _OFF_C + CONV_WIDTH
_OFF_ZA = _OFF_U + CONV_WIDTH
_OFF_Q = _OFF_ZA + CONV_WIDTH
_OFF_K = _OFF_Q + ATTN_WIDTH
_OFF_V = _OFF_K + KV_WIDTH
_OFF_ZB = _OFF_V + KV_WIDTH
_OFF_GA = _OFF_ZB + ATTN_WIDTH
_OFF_GB = _OFF_GA + D_MODEL
IN_COLS = _OFF_GB + D_MODEL

f32 = jnp.float32
bf16 = jnp.bfloat16


def _sigmoid(x):
    return 0.5 * jnp.tanh(0.5 * x) + 0.5


def _silu(x):
    return x * _sigmoid(x)


def _mod_kernel(c_ref, w_ref, b_ref, o_ref):
    s = _silu(c_ref[...])
    o_ref[0] = jnp.dot(s, w_ref[0], preferred_element_type=f32,
                       precision=lax.Precision.HIGHEST) + b_ref[0]


def _modulation(c_all, w_mod, b_mod):
    depth = w_mod.shape[0]
    n_col = 3 * D_MODEL // D_MODEL
    return pl.pallas_call(
        _mod_kernel,
        grid=(depth, n_col),
        in_specs=[
            pl.BlockSpec((MOD_ROWS, D_MODEL), lambda l, j: (0, 0)),
            pl.BlockSpec((1, D_MODEL, D_MODEL), lambda l, j: (l, 0, j)),
            pl.BlockSpec((1, 1, D_MODEL), lambda l, j: (l, 0, j)),
        ],
        out_specs=pl.BlockSpec((1, MOD_ROWS, D_MODEL), lambda l, j: (l, 0, j)),
        out_shape=jax.ShapeDtypeStruct((depth, MOD_ROWS, 3 * D_MODEL), f32),
        name="modulation",
    )(c_all, w_mod, b_mod.reshape(depth, 1, 3 * D_MODEL))


def _rope(p, cos, sin_lo, sin_hi):
    outs = []
    for j in range(p.shape[1] // LANES):
        x = p[:, j * LANES:(j + 1) * LANES]
        outs.append(x * cos
                    + pltpu.roll(x, LANES - ROPE_FREQS, 1) * sin_lo
                    + pltpu.roll(x, ROPE_FREQS, 1) * sin_hi)
    return outs[0] if len(outs) == 1 else jnp.concatenate(outs, axis=1)


def _with_swapped_halves(a):
    return jnp.concatenate([a, pltpu.roll(a, HEAD_DIM, 1)], axis=1)


def _norm_mod(x, nw, scale, shift):
    ms = jnp.mean(x * x, axis=-1, keepdims=True)
    y = x * lax.rsqrt(ms + EPS) * nw
    return (y * (1.0 + scale) + shift).astype(bf16)


def _proj_kernel(*refs, rope, tiles_per_seq):
    it = iter(refs)
    x_ref, xp_ref, xn_ref, nw_ref, shift_ref, scale_ref, w_ref, cw_ref = (
        next(it) for _ in range(8))
    if rope:
        cos_ref, slo_ref, shi_ref = (next(it) for _ in range(3))
    ya_ref, q_ref, k_ref, v_ref, zb_ref, ga_ref, gb_ref = (next(it) for _ in range(7))
    h_s, cu_s = next(it), next(it)
    tm = x_ref.shape[0]
    halo = 2 * SUBLANES

    nw, scale, shift = nw_ref[...], scale_ref[0], shift_ref[0]
    h_s[0:tm, :] = _norm_mod(x_ref[...], nw, scale, shift)
    h_s[tm:tm + halo, :] = _norm_mod(
        jnp.concatenate([xp_ref[...], xn_ref[...]], axis=0), nw, scale, shift)

    def proj(off, width, rows=tm):
        return jnp.dot(h_s[0:rows, :], w_ref[:, off:off + width], preferred_element_type=f32)

    cu = proj(_OFF_C, CONV_WIDTH, tm + halo) * proj(_OFF_U, CONV_WIDTH, tm + halo)
    t_in_seq = pl.program_id(0) % tiles_per_seq
    zero = jnp.zeros((SUBLANES, CONV_WIDTH), f32)
    cu_s[0:SUBLANES, :] = jnp.where(t_in_seq > 0, cu[tm:tm + SUBLANES], zero)
    cu_s[SUBLANES:SUBLANES + tm, :] = cu[0:tm]
    cu_s[SUBLANES + tm:, :] = jnp.where(t_in_seq < tiles_per_seq - 1, cu[tm + SUBLANES:], zero)
    cw = cw_ref[...]
    conv = (cu_s[SUBLANES - 1:SUBLANES - 1 + tm, :] * cw[0:1]
            + cu_s[SUBLANES:SUBLANES + tm, :] * cw[1:2]
            + cu_s[SUBLANES + 1:SUBLANES + 1 + tm, :] * cw[2:3])
    ya_ref[...] = (proj(_OFF_B, CONV_WIDTH) * conv * _silu(proj(_OFF_ZA, CONV_WIDTH))).astype(bf16)

    q = proj(_OFF_Q, ATTN_WIDTH)
    kv = proj(_OFF_K, 2 * KV_WIDTH)
    k = kv[:, :KV_WIDTH]
    if rope:
        cos, slo, shi = cos_ref[...], slo_ref[...], shi_ref[...]
        q = _rope(q, cos, slo, shi)
        k = _rope(k, cos, slo, shi)
    q_ref[...] = (q * (HEAD_DIM ** -0.5 * LOG2E)).astype(bf16)
    k_ref[...] = _with_swapped_halves(k).astype(bf16)
    v_ref[...] = _with_swapped_halves(kv[:, KV_WIDTH:]).astype(bf16)
    zb_ref[...] = _silu(proj(_OFF_ZB, ATTN_WIDTH)).astype(bf16)
    ga_ref[...] = _sigmoid(proj(_OFF_GA, D_MODEL)).astype(bf16)
    gb_ref[...] = _sigmoid(proj(_OFF_GB, D_MODEL)).astype(bf16)


def _projection(x2, norm_w, shift, scale, w_in, conv_w, rope_tabs, *, tm, seq, mod_row0,
                mod_per_seq):
    n_rows = x2.shape[0]
    tiles_per_seq = seq // tm
    halo_per_tile = tm // SUBLANES
    n_halo = n_rows // SUBLANES
    if mod_per_seq:
        mod_map = lambda t: (mod_row0 + t // tiles_per_seq, 0, 0)
    else:
        mod_map = lambda t: (mod_row0, 0, 0)
    row = lambda w: pl.BlockSpec((tm, w), lambda t: (t, 0))
    in_specs = [
        row(D_MODEL),
        pl.BlockSpec((SUBLANES, D_MODEL),
                     lambda t: (jnp.maximum(t * halo_per_tile - 1, 0), 0)),
        pl.BlockSpec((SUBLANES, D_MODEL),
                     lambda t: (jnp.minimum((t + 1) * halo_per_tile, n_halo - 1), 0)),
        pl.BlockSpec((1, D_MODEL), lambda t: (0, 0)),
        pl.BlockSpec((1, 1, D_MODEL), mod_map),
        pl.BlockSpec((1, 1, D_MODEL), mod_map),
        pl.BlockSpec((D_MODEL, IN_COLS), lambda t: (0, 0)),
        pl.BlockSpec((CONV_K, CONV_WIDTH), lambda t: (0, 0)),
    ]
    args = [x2, x2, x2, norm_w.reshape(1, D_MODEL), shift, scale, w_in, conv_w]
    if rope_tabs is not None:
        in_specs += [pl.BlockSpec((tm, LANES), lambda t: (t % tiles_per_seq, 0))] * 3
        args += list(rope_tabs)
    widths = (CONV_WIDTH, ATTN_WIDTH, 2 * KV_WIDTH, 2 * KV_WIDTH, ATTN_WIDTH, D_MODEL, D_MODEL)
    return pl.pallas_call(
        functools.partial(_proj_kernel, rope=rope_tabs is not None,
                          tiles_per_seq=tiles_per_seq),
        grid=(n_rows // tm,),
        in_specs=in_specs,
        out_specs=[row(w) for w in widths],
        out_shape=[jax.ShapeDtypeStruct((n_rows, w), bf16) for w in widths],
        scratch_shapes=[pltpu.VMEM((tm + 2 * SUBLANES, D_MODEL), bf16),
                        pltpu.VMEM((tm + 2 * SUBLANES, CONV_WIDTH), f32)],
        compiler_params=pltpu.CompilerParams(
            dimension_semantics=("arbitrary",), vmem_limit_bytes=VMEM_LIMIT),
        name="projection_rope" if rope_tabs is not None else "projection",
    )(*args)


def _kv_variants(a2):
    a, sw = a2[:, :LANES], a2[:, LANES:]
    low = lax.broadcasted_iota(jnp.int32, a.shape, 1) < HEAD_DIM
    z = jnp.zeros_like(a)
    return [jnp.where(low, a, z), jnp.where(low, z, sw),
            jnp.where(low, sw, z), jnp.where(low, z, a)]


def _mix_kernel(*refs, tq, has_local, final):
    it = iter(refs)
    sink_ref = next(it)
    x_ref, gate_ref, ya_ref, q_ref, zb_ref, ga_ref, gb_ref = (next(it) for _ in range(7))
    if has_local:
        k_ref, kp_ref, kn_ref, v_ref, vp_ref, vn_ref = (next(it) for _ in range(6))
    kc_ref, vc_ref = next(it), next(it)
    wa_ref, wb_ref, wo_ref = (next(it) for _ in range(3))
    if final:
        fw_ref = next(it)
    o_ref = next(it)
    yb_s, k_s, vm_s, s_s, p_s, z_s = (next(it) for _ in range(6))
    if has_local:
        bias_s = next(it)

    i = pl.program_id(1)
    n_i = pl.num_programs(1)
    nblk = tq // BLOCK
    ctx_len = kc_ref.shape[0]
    n_cc = ctx_len // LANES
    n_loc = 3 * BLOCK if has_local else 0
    nk = ctx_len + n_loc
    n_slot = vm_s.shape[0]
    units_per_blk = N_SLAB * HEADS_PER_SLAB

    @pl.when(i == 0)
    def _():
        kvar = _kv_variants(kc_ref[...])
        vvar = _kv_variants(vc_ref[...])
        low = lax.broadcasted_iota(jnp.int32, (nk, LANES), 1) < HEAD_DIM
        ones_on = [jnp.where(low, 1.0, 0.0).astype(bf16), jnp.where(low, 0.0, 1.0).astype(bf16)]
        for n in range(4):
            k_s[n, 0:ctx_len, :] = kvar[n]
            if has_local:
                k_s[n, ctx_len + tq + 2 * BLOCK:, :] = kvar[n]
        for slot in range(n_slot):
            for kh in range(N_KV_HEADS):
                for e in range(HEADS_PER_SLAB):
                    vm_s[slot, kh, e * nk:e * nk + ctx_len, 0:LANES] = vvar[2 * kh + e]
                    vm_s[slot, kh, e * nk:(e + 1) * nk, LANES:] = ones_on[e]

    if has_local:
        assert nblk <= 2, "a middle block's keys are not one contiguous window of k_s"
        r0 = ctx_len
        for src in (kp_ref, k_ref, kn_ref):
            for n, a in enumerate(_kv_variants(src[...])):
                k_s[n, r0:r0 + src.shape[0], :] = a
            r0 += src.shape[0]
        for t in range(nblk + 2):
            if t == 0:
                blk = vp_ref[...]
            elif t == nblk + 1:
                blk = vn_ref[...]
            else:
                blk = v_ref[(t - 1) * BLOCK:t * BLOCK, :]
            vvar = _kv_variants(blk)
            for j in range(nblk):
                b = t - j
                if 0 <= b <= 2:
                    for kh in range(N_KV_HEADS):
                        for e in range(HEADS_PER_SLAB):
                            r = e * nk + ctx_len + b * BLOCK
                            vm_s[j % n_slot, kh, r:r + BLOCK, 0:LANES] = vvar[2 * kh + e]
        qi = lax.broadcasted_iota(jnp.int32, (BLOCK, BLOCK), 0)
        ci = lax.broadcasted_iota(jnp.int32, (BLOCK, BLOCK), 1)
        tri_prev = jnp.where(ci >= qi, 0.0, NEG)
        tri_next = jnp.where(ci <= qi, 0.0, NEG)
        bias_s[0] = tri_prev + jnp.where(i == 0, NEG, 0.0)
        bias_s[1] = tri_prev
        bias_s[2] = tri_next
        bias_s[3] = tri_next + jnp.where(i == n_i - 1, NEG, 0.0)

    ctx_names = ["c%d" % n for n in range(n_cc)]
    p_names = ctx_names + (["prev", "mid", "next"] if has_local else [])

    def key_window(j):
        if not has_local:
            return 0, ctx_names
        if j == 0:
            return 0, ctx_names + ["prev", "mid", "next"]
        return ctx_len + j * BLOCK, ["prev", "mid", "next"] + ctx_names

    nt = (((1,), (1,)), ((), ()))

    def stage_a(j, c, e):
        r0, _ = key_window(j)
        q1 = q_ref[j * BLOCK:(j + 1) * BLOCK, c * LANES:(c + 1) * LANES]
        kv_head = c * HEADS_PER_SLAB // (N_HEADS // N_KV_HEADS)
        s_s[j * units_per_blk + c * HEADS_PER_SLAB + e] = lax.dot_general(
            q1, k_s[2 * kv_head + e, r0:r0 + nk, :], nt, preferred_element_type=f32)

    def stage_b(j, c, e):
        u = j * units_per_blk + c * HEADS_PER_SLAB + e
        slab = j * N_SLAB + c
        sk = sink_ref[c * HEADS_PER_SLAB + e] * LOG2E
        _, names = key_window(j)
        pieces = []
        for n, name in enumerate(names):
            piece = s_s[u, :, n * LANES:(n + 1) * LANES]
            if name == "prev":
                piece = piece + bias_s[0 if j == 0 else 1]
            elif name == "next":
                piece = piece + bias_s[3 if j == nblk - 1 else 2]
            pieces.append(piece)
        m = jnp.maximum(jnp.max(functools.reduce(jnp.maximum, pieces), axis=1, keepdims=True),
                        sk)
        for name, piece in zip(names, pieces):
            col = e * nk + p_names.index(name) * LANES
            p_s[slab, :, col:col + LANES] = jnp.exp2(piece - m).astype(bf16)
        z_s[slab, :, e * HEAD_DIM:(e + 1) * HEAD_DIM] = jnp.broadcast_to(
            jnp.exp2(sk - m), (BLOCK, HEAD_DIM))

    def stage_c(j, c):
        slab = j * N_SLAB + c
        kv_head = c * HEADS_PER_SLAB // (N_HEADS // N_KV_HEADS)
        res = jnp.dot(p_s[slab], vm_s[j % n_slot, kv_head], preferred_element_type=f32)
        attn = res[:, :LANES] / (res[:, LANES:] + z_s[slab])
        rows = slice(j * BLOCK, (j + 1) * BLOCK)
        lanes = slice(c * LANES, (c + 1) * LANES)
        yb_s[rows, lanes] = (attn * zb_ref[rows, lanes].astype(f32)).astype(bf16)

    y = None
    for t in range(nblk + 2):
        if t == 1:
            y = (jnp.dot(ya_ref[...], wa_ref[...], preferred_element_type=f32)
                 * ga_ref[...].astype(f32))
        for c in range(N_SLAB):
            for e in range(HEADS_PER_SLAB):
                if t < nblk:
                    stage_a(t, c, e)
                if 0 <= t - 1 < nblk:
                    stage_b(t - 1, c, e)
            if 0 <= t - 2 < nblk:
                stage_c(t - 2, c)

    y = y + (jnp.dot(yb_s[...], wb_ref[...], preferred_element_type=f32)
             * gb_ref[...].astype(f32))
    out = jnp.dot(y.astype(bf16), wo_ref[...], preferred_element_type=f32)
    xn = x_ref[...] + gate_ref[0] * out
    if final:
        ms = jnp.mean(xn * xn, axis=-1, keepdims=True)
        xn = xn * lax.rsqrt(ms + EPS) * fw_ref[...]
    o_ref[...] = xn


def _mixer(x2, gate, sink, proj_out, kc, vc, wa, wb, wo, final_w, *,
           seq, tq, has_local, gate_row0, gate_per_batch):
    ya, q, k, v, zb, ga, gb = proj_out
    n_rows = x2.shape[0]
    batch = n_rows // seq
    ctx_len = kc.shape[0] // batch
    n_i = seq // tq
    nblk = tq // BLOCK
    final = final_w is not None

    def row(w):
        return pl.BlockSpec((tq, w), lambda b, i: (b * n_i + i, 0))

    def halo(rows_blk, w, side):
        per_tile = tq // rows_blk
        per_seq = seq // rows_blk
        if side < 0:
            return pl.BlockSpec((rows_blk, w), lambda b, i: (
                b * per_seq + jnp.maximum(i * per_tile - 1, 0), 0))
        return pl.BlockSpec((rows_blk, w), lambda b, i: (
            b * per_seq + jnp.minimum((i + 1) * per_tile, per_seq - 1), 0))

    def whole(shape):
        return pl.BlockSpec(shape, lambda b, i: (0,) * len(shape))

    if gate_per_batch:
        gate_spec = pl.BlockSpec((1, 1, D_MODEL), lambda b, i: (gate_row0 + b, 0, 0))
    else:
        gate_spec = pl.BlockSpec((1, 1, D_MODEL), lambda b, i: (gate_row0, 0, 0))

    kvw = 2 * KV_WIDTH
    in_specs = [pl.BlockSpec(memory_space=pltpu.SMEM),
                row(D_MODEL), gate_spec, row(CONV_WIDTH),
                row(ATTN_WIDTH), row(ATTN_WIDTH), row(D_MODEL), row(D_MODEL)]
    args = [sink, x2, gate, ya, q, zb, ga, gb]
    if has_local:
        in_specs += [row(kvw), halo(BLOCK, kvw, -1), halo(BLOCK, kvw, +1)] * 2
        args += [k, k, k, v, v, v]
    ctx_spec = pl.BlockSpec((ctx_len, kvw), lambda b, i: (b, 0))
    in_specs += [ctx_spec, ctx_spec, whole((CONV_WIDTH, D_MODEL)),
                 whole((ATTN_WIDTH, D_MODEL)), whole((D_MODEL, D_MODEL))]
    args += [kc, vc, wa, wb, wo]
    if final:
        in_specs.append(whole((1, D_MODEL)))
        args.append(final_w.reshape(1, D_MODEL))

    nk = ctx_len + (3 * BLOCK if has_local else 0)
    k_rows = 2 * ctx_len + tq + 2 * BLOCK if has_local else ctx_len
    n_slot = 2 if (has_local and nblk > 1) else 1
    scratch = [pltpu.VMEM((tq, ATTN_WIDTH), bf16),
               pltpu.VMEM((2 * N_KV_HEADS, k_rows, LANES), bf16),
               pltpu.VMEM((n_slot, N_KV_HEADS, HEADS_PER_SLAB * nk, 2 * LANES), bf16),
               pltpu.VMEM((nblk * N_HEADS, BLOCK, nk), f32),
               pltpu.VMEM((nblk * N_SLAB, BLOCK, HEADS_PER_SLAB * nk), bf16),
               pltpu.VMEM((nblk * N_SLAB, BLOCK, LANES), f32)]
    if has_local:
        scratch += [pltpu.VMEM((4, BLOCK, BLOCK), f32)]

    return pl.pallas_call(
        functools.partial(_mix_kernel, tq=tq, has_local=has_local, final=final),
        grid=(batch, n_i),
        in_specs=in_specs,
        out_specs=row(D_MODEL),
        out_shape=jax.ShapeDtypeStruct((n_rows, D_MODEL), f32),
        scratch_shapes=scratch,
        compiler_params=pltpu.CompilerParams(
            dimension_semantics=("arbitrary", "arbitrary"), vmem_limit_bytes=VMEM_LIMIT),
        name="mixer_latent" if has_local else "mixer_context",
    )(*args)


def _rope_tables(seq):
    pos = np.arange(seq)
    inv_freq = jnp.asarray(ROPE_THETA, f32) ** (-jnp.arange(ROPE_FREQS, dtype=f32) / ROPE_FREQS)
    lane = np.arange(LANES)
    use_col = (lane % HEAD_DIM) >= HEAD_DIM // 2
    upper = (lane % (2 * ROPE_FREQS)) >= ROPE_FREQS
    coord = jnp.where(use_col[None, :], (pos % GRID_W)[:, None], (pos // GRID_W)[:, None])
    ang = coord.astype(f32) * inv_freq[lane % ROPE_FREQS][None, :]
    cos, sin = jnp.cos(ang), jnp.sin(ang)
    zero = jnp.zeros_like(sin)
    sin_lo = jnp.where(upper[None, :], zero, -sin)
    sin_hi = jnp.where(upper[None, :], sin, zero)
    return cos, sin_lo, sin_hi


def kernel(x, c, ctx, c_ctx, norm_w, w_mod, b_mod, w_in, conv_w, w_a_out, w_b_out,
           attn_sink, w_o, final_norm_w):
    batch, seq, _ = x.shape
    ctx_len = ctx.shape[1]
    depth = w_in.shape[0]
    assert batch + 1 <= MOD_ROWS and seq % 512 == 0 and ctx_len % BLOCK == 0

    c_all = jnp.zeros((MOD_ROWS, D_MODEL), f32).at[:batch].set(c).at[batch].set(c_ctx)
    mod = _modulation(c_all, w_mod, b_mod).reshape(depth, MOD_ROWS, 3, 1, D_MODEL)
    rope_tabs = _rope_tables(seq)

    w_in_b = w_in.astype(bf16)
    wa_b, wb_b, wo_b = w_a_out.astype(bf16), w_b_out.astype(bf16), w_o.astype(bf16)

    x2 = x.reshape(batch * seq, D_MODEL)
    c2 = ctx.reshape(batch * ctx_len, D_MODEL)
    for l in range(depth):
        last = l == depth - 1
        shift, scale, gate = mod[l, :, 0], mod[l, :, 1], mod[l, :, 2]
        pc = _projection(c2, norm_w[l], shift, scale, w_in_b[l], conv_w[l], None,
                         tm=ctx_len, seq=ctx_len, mod_row0=batch, mod_per_seq=False)
        px = _projection(x2, norm_w[l], shift, scale, w_in_b[l], conv_w[l], rope_tabs,
                         tm=512, seq=seq, mod_row0=0, mod_per_seq=True)
        kc, vc = pc[2], pc[3]
        x2 = _mixer(x2, gate, attn_sink[l], px, kc, vc, wa_b[l], wb_b[l], wo_b[l],
                    final_norm_w if last else None,
                    seq=seq, tq=256, has_local=True, gate_row0=0, gate_per_batch=True)
        if not last:
            c2 = _mixer(c2, gate, attn_sink[l], pc, kc, vc, wa_b[l], wb_b[l],
                        wo_b[l], None,
                        seq=ctx_len, tq=ctx_len, has_local=False, gate_row0=batch,
                        gate_per_batch=False)
    return x2.reshape(batch, seq, D_MODEL)
```

```python
import functools
import math

import numpy as np
import jax
import jax.numpy as jnp
from jax import lax
from jax.experimental import pallas as pl
from jax.experimental.pallas import tpu as pltpu

D_MODEL = 1024
GRID_W = 64
CONV_WIDTH = 512
CONV_K = 3
N_HEADS = 8
N_KV_HEADS = 2
HEAD_DIM = 64
ATTN_WIDTH = N_HEADS * HEAD_DIM
KV_WIDTH = N_KV_HEADS * HEAD_DIM
BLOCK = 128
ROPE_THETA = 10000.0
ROPE_FREQS = HEAD_DIM // 4
EPS = 1e-6
NEG = -1e30
LOG2E = math.log2(math.e)

LANES = 128
SUBLANES = 8
MOD_ROWS = 24
VMEM_LIMIT = 56 * 1024 * 1024
N_SLAB = ATTN_WIDTH // LANES
HEADS_PER_SLAB = LANES // HEAD_DIM

_OFF_B = 0
_OFF_C = _OFF_B + CONV_WIDTH
_OFF_U = _OFF_C + CONV_WIDTH
_OFF_ZA = _OFF_U + CONV_WIDTH
_OFF_Q = _OFF_ZA + CONV_WIDTH
_OFF_K = _OFF_Q + ATTN_WIDTH
_OFF_V = _OFF_K + KV_WIDTH
_OFF_ZB = _OFF_V + KV_WIDTH
_OFF_GA = _OFF_ZB + ATTN_WIDTH
_OFF_GB = _OFF_GA + D_MODEL
IN_COLS = _OFF_GB + D_MODEL

f32 = jnp.float32
bf16 = jnp.bfloat16


def _sigmoid(x):
    return 0.5 * jnp.tanh(0.5 * x) + 0.5


def _silu(x):
    return x * _sigmoid(x)


def _mod_kernel(c_ref, w_ref, b_ref, o_ref):
    s = _silu(c_ref[...])
    o_ref[0] = jnp.dot(s, w_ref[0], preferred_element_type=f32,
                       precision=lax.Precision.HIGHEST) + b_ref[0]


def _modulation(c_all, w_mod, b_mod):
    depth = w_mod.shape[0]
    n_col = 3 * D_MODEL // D_MODEL
    return pl.pallas_call(
        _mod_kernel,
        grid=(depth, n_col),
        in_specs=[
            pl.BlockSpec((MOD_ROWS, D_MODEL), lambda l, j: (0, 0)),
            pl.BlockSpec((1, D_MODEL, D_MODEL), lambda l, j: (l, 0, j)),
            pl.BlockSpec((1, 1, D_MODEL), lambda l, j: (l, 0, j)),
        ],
        out_specs=pl.BlockSpec((1, MOD_ROWS, D_MODEL), lambda l, j: (l, 0, j)),
        out_shape=jax.ShapeDtypeStruct((depth, MOD_ROWS, 3 * D_MODEL), f32),
        name="modulation",
    )(c_all, w_mod, b_mod.reshape(depth, 1, 3 * D_MODEL))


def _rope(p, cos, sin_lo, sin_hi):
    outs = []
    for j in range(p.shape[1] // LANES):
        x = p[:, j * LANES:(j + 1) * LANES]
        outs.append(x * cos
                    + pltpu.roll(x, LANES - ROPE_FREQS, 1) * sin_lo
                    + pltpu.roll(x, ROPE_FREQS, 1) * sin_hi)
    return outs[0] if len(outs) == 1 else jnp.concatenate(outs, axis=1)


def _with_swapped_halves(a):
    return jnp.concatenate([a, pltpu.roll(a, HEAD_DIM, 1)], axis=1)


def _norm_mod(x, nw, scale, shift):
    ms = jnp.mean(x * x, axis=-1, keepdims=True)
    y = x * lax.rsqrt(ms + EPS) * nw
    return (y * (1.0 + scale) + shift).astype(bf16)


def _proj_kernel(*refs, rope, tiles_per_seq):
    it = iter(refs)
    x_ref, xp_ref, xn_ref, nw_ref, shift_ref, scale_ref, w_ref, cw_ref = (
        next(it) for _ in range(8))
    if rope:
        cos_ref, slo_ref, shi_ref = (next(it) for _ in range(3))
    ya_ref, q_ref, k_ref, v_ref, zb_ref, ga_ref, gb_ref = (next(it) for _ in range(7))
    h_s, cu_s = next(it), next(it)
    tm = x_ref.shape[0]
    halo = 2 * SUBLANES

    nw, scale, shift = nw_ref[...], scale_ref[0], shift_ref[0]
    h_s[0:tm, :] = _norm_mod(x_ref[...], nw, scale, shift)
    h_s[tm:tm + halo, :] = _norm_mod(
        jnp.concatenate([xp_ref[...], xn_ref[...]], axis=0), nw, scale, shift)

    def proj(off, width, rows=tm):
        return jnp.dot(h_s[0:rows, :], w_ref[:, off:off + width], preferred_element_type=f32)

    cu = proj(_OFF_C, CONV_WIDTH, tm + halo) * proj(_OFF_U, CONV_WIDTH, tm + halo)
    t_in_seq = pl.program_id(0) % tiles_per_seq
    zero = jnp.zeros((SUBLANES, CONV_WIDTH), f32)
    cu_s[0:SUBLANES, :] = jnp.where(t_in_seq > 0, cu[tm:tm + SUBLANES], zero)
    cu_s[SUBLANES:SUBLANES + tm, :] = cu[0:tm]
    cu_s[SUBLANES + tm:, :] = jnp.where(t_in_seq < tiles_per_seq - 1, cu[tm + SUBLANES:], zero)
    cw = cw_ref[...]
    conv = (cu_s[SUBLANES - 1:SUBLANES - 1 + tm, :] * cw[0:1]
            + cu_s[SUBLANES:SUBLANES + tm, :] * cw[1:2]
            + cu_s[SUBLANES + 1:SUBLANES + 1 + tm, :] * cw[2:3])
    ya_ref[...] = (proj(_OFF_B, CONV_WIDTH) * conv * _silu(proj(_OFF_ZA, CONV_WIDTH))).astype(bf16)

    q = proj(_OFF_Q, ATTN_WIDTH)
    kv = proj(_OFF_K, 2 * KV_WIDTH)
    k = kv[:, :KV_WIDTH]
    if rope:
        cos, slo, shi = cos_ref[...], slo_ref[...], shi_ref[...]
        q = _rope(q, cos, slo, shi)
        k = _rope(k, cos, slo, shi)
    q_ref[...] = (q * (HEAD_DIM ** -0.5 * LOG2E)).astype(bf16)
    k_ref[...] = _with_swapped_halves(k).astype(bf16)
    v_ref[...] = _with_swapped_halves(kv[:, KV_WIDTH:]).astype(bf16)
    zb_ref[...] = _silu(proj(_OFF_ZB, ATTN_WIDTH)).astype(bf16)
    ga_ref[...] = _sigmoid(proj(_OFF_GA, D_MODEL)).astype(bf16)
    gb_ref[...] = _sigmoid(proj(_OFF_GB, D_MODEL)).astype(bf16)


def _projection(x2, norm_w, shift, scale, w_in, conv_w, rope_tabs, *, tm, seq, mod_row0,
                mod_per_seq):
    n_rows = x2.shape[0]
    tiles_per_seq = seq // tm
    halo_per_tile = tm // SUBLANES
    n_halo = n_rows // SUBLANES
    if mod_per_seq:
        mod_map = lambda t: (mod_row0 + t // tiles_per_seq, 0, 0)
    else:
        mod_map = lambda t: (mod_row0, 0, 0)
    row = lambda w: pl.BlockSpec((tm, w), lambda t: (t, 0))
    in_specs = [
        row(D_MODEL),
        pl.BlockSpec((SUBLANES, D_MODEL),
                     lambda t: (jnp.maximum(t * halo_per_tile - 1, 0), 0)),
        pl.BlockSpec((SUBLANES, D_MODEL),
                     lambda t: (jnp.minimum((t + 1) * halo_per_tile, n_halo - 1), 0)),
        pl.BlockSpec((1, D_MODEL), lambda t: (0, 0)),
        pl.BlockSpec((1, 1, D_MODEL), mod_map),
        pl.BlockSpec((1, 1, D_MODEL), mod_map),
        pl.BlockSpec((D_MODEL, IN_COLS), lambda t: (0, 0)),
        pl.BlockSpec((CONV_K, CONV_WIDTH), lambda t: (0, 0)),
    ]
    args = [x2, x2, x2, norm_w.reshape(1, D_MODEL), shift, scale, w_in, conv_w]
    if rope_tabs is not None:
        in_specs += [pl.BlockSpec((tm, LANES), lambda t: (t % tiles_per_seq, 0))] * 3
        args += list(rope_tabs)
    widths = (CONV_WIDTH, ATTN_WIDTH, 2 * KV_WIDTH, 2 * KV_WIDTH, ATTN_WIDTH, D_MODEL, D_MODEL)
    return pl.pallas_call(
        functools.partial(_proj_kernel, rope=rope_tabs is not None,
                          tiles_per_seq=tiles_per_seq),
        grid=(n_rows // tm,),
        in_specs=in_specs,
        out_specs=[row(w) for w in widths],
        out_shape=[jax.ShapeDtypeStruct((n_rows, w), bf16) for w in widths],
        scratch_shapes=[pltpu.VMEM((tm + 2 * SUBLANES, D_MODEL), bf16),
                        pltpu.VMEM((tm + 2 * SUBLANES, CONV_WIDTH), f32)],
        compiler_params=pltpu.CompilerParams(
            dimension_semantics=("arbitrary",), vmem_limit_bytes=VMEM_LIMIT),
        name="projection_rope" if rope_tabs is not None else "projection",
    )(*args)


def _kv_variants(a2):
    a, sw = a2[:, :LANES], a2[:, LANES:]
    low = lax.broadcasted_iota(jnp.int32, a.shape, 1) < HEAD_DIM
    z = jnp.zeros_like(a)
    return [jnp.where(low, a, z), jnp.where(low, z, sw),
            jnp.where(low, sw, z), jnp.where(low, z, a)]


def _mix_kernel(*refs, tq, has_local, final):
    it = iter(refs)
    sink_ref = next(it)
    x_ref, gate_ref, ya_ref, q_ref, zb_ref, ga_ref, gb_ref = (next(it) for _ in range(7))
    if has_local:
        k_ref, kp_ref, kn_ref, v_ref, vp_ref, vn_ref = (next(it) for _ in range(6))
    kc_ref, vc_ref = next(it), next(it)
    wa_ref, wb_ref, wo_ref = (next(it) for _ in range(3))
    if final:
        fw_ref = next(it)
    o_ref = next(it)
    y_s, yb_s, k_s, vm_s, s_s, p_s, z_s = (next(it) for _ in range(7))
    if has_local:
        bias_s = next(it)

    i = pl.program_id(1)
    n_i = pl.num_programs(1)
    nblk = tq // BLOCK
    ctx_len = kc_ref.shape[0]
    n_cc = ctx_len // LANES
    n_loc = 3 * BLOCK if has_local else 0
    nk = ctx_len + n_loc
    n_slot = vm_s.shape[0]
    units_per_blk = N_SLAB * HEADS_PER_SLAB

    @pl.when(i == 0)
    def _():
        kvar = _kv_variants(kc_ref[...])
        vvar = _kv_variants(vc_ref[...])
        low = lax.broadcasted_iota(jnp.int32, (nk, LANES), 1) < HEAD_DIM
        ones_on = [jnp.where(low, 1.0, 0.0).astype(bf16), jnp.where(low, 0.0, 1.0).astype(bf16)]
        for n in range(4):
            k_s[n, 0:ctx_len, :] = kvar[n]
        for slot in range(n_slot):
            for kh in range(N_KV_HEADS):
                for e in range(HEADS_PER_SLAB):
                    vm_s[slot, kh, e * nk:e * nk + ctx_len, 0:LANES] = vvar[2 * kh + e]
                    vm_s[slot, kh, e * nk:(e + 1) * nk, LANES:] = ones_on[e]

    def local_block(ref_prev, ref_tile, ref_next, t):
        if t == 0:
            return ref_prev[...]
        if t == nblk + 1:
            return ref_next[...]
        return ref_tile[(t - 1) * BLOCK:t * BLOCK, :]

    def fill_values(j, b):
        vvar = _kv_variants(local_block(vp_ref, v_ref, vn_ref, j + b))
        for kh in range(N_KV_HEADS):
            for e in range(HEADS_PER_SLAB):
                r = e * nk + ctx_len + b * BLOCK
                vm_s[j, kh, r:r + BLOCK, 0:LANES] = vvar[2 * kh + e]

    def fill_keys(t):
        for n, a in enumerate(_kv_variants(local_block(kp_ref, k_ref, kn_ref, t))):
            k_s[n, ctx_len + t * BLOCK:ctx_len + (t + 1) * BLOCK, :] = a

    def fill_bias():
        qi = lax.broadcasted_iota(jnp.int32, (BLOCK, BLOCK), 0)
        ci = lax.broadcasted_iota(jnp.int32, (BLOCK, BLOCK), 1)
        tri_prev = jnp.where(ci >= qi, 0.0, NEG)
        tri_next = jnp.where(ci <= qi, 0.0, NEG)
        bias_s[0] = tri_prev + jnp.where(i == 0, NEG, 0.0)
        bias_s[1] = tri_prev
        bias_s[2] = tri_next
        bias_s[3] = tri_next + jnp.where(i == n_i - 1, NEG, 0.0)

    def conv_branch(n, width):
        cols = slice(n * width, (n + 1) * width)
        y_s[:, cols] = (jnp.dot(ya_ref[...], wa_ref[:, cols], preferred_element_type=f32)
                        * ga_ref[:, cols].astype(f32))

    fills = ([functools.partial(fill_keys, t) for t in range(nblk + 2)] + [fill_bias]
             if has_local else [])
    n_part = 4
    for n in range(n_part):
        conv_branch(n, D_MODEL // n_part)
        for fill in fills[n::n_part]:
            fill()

    names = ["c%d" % n for n in range(n_cc)] + (["prev", "mid", "next"] if has_local else [])
    nt = (((1,), (1,)), ((), ()))

    def stage_a(j, c, e):
        u = j * units_per_blk + c * HEADS_PER_SLAB + e
        q1 = q_ref[j * BLOCK:(j + 1) * BLOCK, c * LANES:(c + 1) * LANES]
        var = 2 * (c * HEADS_PER_SLAB // (N_HEADS // N_KV_HEADS)) + e
        s_s[u, :, 0:ctx_len] = lax.dot_general(
            q1, k_s[var, 0:ctx_len, :], nt, preferred_element_type=f32)
        if has_local:
            r0 = ctx_len + j * BLOCK
            s_s[u, :, ctx_len:] = lax.dot_general(
                q1, k_s[var, r0:r0 + n_loc, :], nt, preferred_element_type=f32)

    def stage_b(j, c, e):
        u = j * units_per_blk + c * HEADS_PER_SLAB + e
        slab = j * N_SLAB + c
        sk = sink_ref[c * HEADS_PER_SLAB + e] * LOG2E
        pieces = []
        for n, name in enumerate(names):
            piece = s_s[u, :, n * LANES:(n + 1) * LANES]
            if name == "prev":
                piece = piece + bias_s[0 if j == 0 else 1]
            elif name == "next":
                piece = piece + bias_s[3 if j == nblk - 1 else 2]
            pieces.append(piece)
        m = jnp.maximum(jnp.max(functools.reduce(jnp.maximum, pieces), axis=1, keepdims=True),
                        sk)
        for n, piece in enumerate(pieces):
            col = e * nk + n * LANES
            p_s[slab, :, col:col + LANES] = jnp.exp2(piece - m).astype(bf16)
        z_s[slab, :, e * HEAD_DIM:(e + 1) * HEAD_DIM] = jnp.broadcast_to(
            jnp.exp2(sk - m), (BLOCK, HEAD_DIM))

    def stage_c(j, c):
        slab = j * N_SLAB + c
        kv_head = c * HEADS_PER_SLAB // (N_HEADS // N_KV_HEADS)
        res = jnp.dot(p_s[slab], vm_s[j if has_local else 0, kv_head],
                      preferred_element_type=f32)
        attn = res[:, :LANES] / (res[:, LANES:] + z_s[slab])
        rows = slice(j * BLOCK, (j + 1) * BLOCK)
        lanes = slice(c * LANES, (c + 1) * LANES)
        yb_s[rows, lanes] = (attn * zb_ref[rows, lanes].astype(f32)).astype(bf16)

    for t in range(nblk + 2):
        for c in range(N_SLAB):
            if has_local and 0 <= t - 1 < nblk and c < 3:
                fill_values(t - 1, c)
            for e in range(HEADS_PER_SLAB):
                if t < nblk:
                    stage_a(t, c, e)
                if 0 <= t - 1 < nblk:
                    stage_b(t - 1, c, e)
            if 0 <= t - 2 < nblk:
                stage_c(t - 2, c)

    y = y_s[...] + (jnp.dot(yb_s[...], wb_ref[...], preferred_element_type=f32)
                    * gb_ref[...].astype(f32))
    y16 = y.astype(bf16)
    if final:
        xn = x_ref[...] + gate_ref[0] * jnp.dot(y16, wo_ref[...], preferred_element_type=f32)
        ms = jnp.mean(xn * xn, axis=-1, keepdims=True)
        o_ref[...] = xn * lax.rsqrt(ms + EPS) * fw_ref[...]
    else:
        width = D_MODEL // n_part
        for n in range(n_part):
            cols = slice(n * width, (n + 1) * width)
            o_ref[:, cols] = x_ref[:, cols] + gate_ref[0, :, cols] * jnp.dot(
                y16, wo_ref[:, cols], preferred_element_type=f32)


def _mixer(x2, gate, sink, proj_out, kc, vc, wa, wb, wo, final_w, *,
           seq, tq, has_local, gate_row0, gate_per_batch):
    ya, q, k, v, zb, ga, gb = proj_out
    n_rows = x2.shape[0]
    batch = n_rows // seq
    ctx_len = kc.shape[0] // batch
    n_i = seq // tq
    nblk = tq // BLOCK
    final = final_w is not None

    def row(w):
        return pl.BlockSpec((tq, w), lambda b, i: (b * n_i + i, 0))

    def halo(rows_blk, w, side):
        per_tile = tq // rows_blk
        per_seq = seq // rows_blk
        if side < 0:
            return pl.BlockSpec((rows_blk, w), lambda b, i: (
                b * per_seq + jnp.maximum(i * per_tile - 1, 0), 0))
        return pl.BlockSpec((rows_blk, w), lambda b, i: (
            b * per_seq + jnp.minimum((i + 1) * per_tile, per_seq - 1), 0))

    def whole(shape):
        return pl.BlockSpec(shape, lambda b, i: (0,) * len(shape))

    if gate_per_batch:
        gate_spec = pl.BlockSpec((1, 1, D_MODEL), lambda b, i: (gate_row0 + b, 0, 0))
    else:
        gate_spec = pl.BlockSpec((1, 1, D_MODEL), lambda b, i: (gate_row0, 0, 0))

    kvw = 2 * KV_WIDTH
    in_specs = [pl.BlockSpec(memory_space=pltpu.SMEM),
                row(D_MODEL), gate_spec, row(CONV_WIDTH),
                row(ATTN_WIDTH), row(ATTN_WIDTH), row(D_MODEL), row(D_MODEL)]
    args = [sink, x2, gate, ya, q, zb, ga, gb]
    if has_local:
        in_specs += [row(kvw), halo(BLOCK, kvw, -1), halo(BLOCK, kvw, +1)] * 2
        args += [k, k, k, v, v, v]
    ctx_spec = pl.BlockSpec((ctx_len, kvw), lambda b, i: (b, 0))
    in_specs += [ctx_spec, ctx_spec, whole((CONV_WIDTH, D_MODEL)),
                 whole((ATTN_WIDTH, D_MODEL)), whole((D_MODEL, D_MODEL))]
    args += [kc, vc, wa, wb, wo]
    if final:
        in_specs.append(whole((1, D_MODEL)))
        args.append(final_w.reshape(1, D_MODEL))

    nk = ctx_len + (3 * BLOCK if has_local else 0)
    k_rows = ctx_len + tq + 2 * BLOCK if has_local else ctx_len
    n_slot = nblk if has_local else 1
    scratch = [pltpu.VMEM((tq, D_MODEL), f32),
               pltpu.VMEM((tq, ATTN_WIDTH), bf16),
               pltpu.VMEM((2 * N_KV_HEADS, k_rows, LANES), bf16),
               pltpu.VMEM((n_slot, N_KV_HEADS, HEADS_PER_SLAB * nk, 2 * LANES), bf16),
               pltpu.VMEM((nblk * N_HEADS, BLOCK, nk), f32),
               pltpu.VMEM((nblk * N_SLAB, BLOCK, HEADS_PER_SLAB * nk), bf16),
               pltpu.VMEM((nblk * N_SLAB, BLOCK, LANES), f32)]
    if has_local:
        scratch += [pltpu.VMEM((4, BLOCK, BLOCK), f32)]

    return pl.pallas_call(
        functools.partial(_mix_kernel, tq=tq, has_local=has_local, final=final),
        grid=(batch, n_i),
        in_specs=in_specs,
        out_specs=row(D_MODEL),
        out_shape=jax.ShapeDtypeStruct((n_rows, D_MODEL), f32),
        scratch_shapes=scratch,
        compiler_params=pltpu.CompilerParams(
            dimension_semantics=("arbitrary", "arbitrary"), vmem_limit_bytes=VMEM_LIMIT),
        name="mixer_latent" if has_local else "mixer_context",
    )(*args)


def _rope_tables(seq):
    pos = np.arange(seq)
    inv_freq = jnp.asarray(ROPE_THETA, f32) ** (-jnp.arange(ROPE_FREQS, dtype=f32) / ROPE_FREQS)
    lane = np.arange(LANES)
    use_col = (lane % HEAD_DIM) >= HEAD_DIM // 2
    upper = (lane % (2 * ROPE_FREQS)) >= ROPE_FREQS
    coord = jnp.where(use_col[None, :], (pos % GRID_W)[:, None], (pos // GRID_W)[:, None])
    ang = coord.astype(f32) * inv_freq[lane % ROPE_FREQS][None, :]
    cos, sin = jnp.cos(ang), jnp.sin(ang)
    zero = jnp.zeros_like(sin)
    sin_lo = jnp.where(upper[None, :], zero, -sin)
    sin_hi = jnp.where(upper[None, :], sin, zero)
    return cos, sin_lo, sin_hi


def kernel(x, c, ctx, c_ctx, norm_w, w_mod, b_mod, w_in, conv_w, w_a_out, w_b_out,
           attn_sink, w_o, final_norm_w):
    batch, seq, _ = x.shape
    ctx_len = ctx.shape[1]
    depth = w_in.shape[0]
    assert batch + 1 <= MOD_ROWS and seq % 512 == 0 and ctx_len % BLOCK == 0

    c_all = jnp.zeros((MOD_ROWS, D_MODEL), f32).at[:batch].set(c).at[batch].set(c_ctx)
    mod = _modulation(c_all, w_mod, b_mod).reshape(depth, MOD_ROWS, 3, 1, D_MODEL)
    rope_tabs = _rope_tables(seq)

    w_in_b = w_in.astype(bf16)
    wa_b, wb_b, wo_b = w_a_out.astype(bf16), w_b_out.astype(bf16), w_o.astype(bf16)

    x2 = x.reshape(batch * seq, D_MODEL)
    c2 = ctx.reshape(batch * ctx_len, D_MODEL)
    for l in range(depth):
        last = l == depth - 1
        shift, scale, gate = mod[l, :, 0], mod[l, :, 1], mod[l, :, 2]
        pc = _projection(c2, norm_w[l], shift, scale, w_in_b[l], conv_w[l], None,
                         tm=ctx_len, seq=ctx_len, mod_row0=batch, mod_per_seq=False)
        px = _projection(x2, norm_w[l], shift, scale, w_in_b[l], conv_w[l], rope_tabs,
                         tm=512, seq=seq, mod_row0=0, mod_per_seq=True)
        kc, vc = pc[2], pc[3]
        x2 = _mixer(x2, gate, attn_sink[l], px, kc, vc, wa_b[l], wb_b[l], wo_b[l],
                    final_norm_w if last else None,
                    seq=seq, tq=512, has_local=True, gate_row0=0, gate_per_batch=True)
        if not last:
            c2 = _mixer(c2, gate, attn_sink[l], pc, kc, vc, wa_b[l], wb_b[l],
                        wo_b[l], None,
                        seq=ctx_len, tq=ctx_len, has_local=False, gate_row0=batch,
                        gate_per_batch=False)
    return x2.reshape(batch, seq, D_MODEL)
```

```python
import functools
import math

import numpy as np
import jax
import jax.numpy as jnp
from jax import lax
from jax.experimental import pallas as pl
from jax.experimental.pallas import tpu as pltpu

D_MODEL = 1024
GRID_W = 64
CONV_WIDTH = 512
CONV_K = 3
N_HEADS = 8
N_KV_HEADS = 2
HEAD_DIM = 64
ATTN_WIDTH = N_HEADS * HEAD_DIM
KV_WIDTH = N_KV_HEADS * HEAD_DIM
BLOCK = 128
ROPE_THETA = 10000.0
ROPE_FREQS = HEAD_DIM // 4
EPS = 1e-6
NEG = -1e30
LOG2E = math.log2(math.e)

LANES = 128
SUBLANES = 8
MOD_ROWS = 24
VMEM_LIMIT = 56 * 1024 * 1024
N_SLAB = ATTN_WIDTH // LANES
HEADS_PER_SLAB = LANES // HEAD_DIM
SLABS_PER_KV = N_SLAB // N_KV_HEADS

_OFF_B = 0
_OFF_C = _OFF_B + CONV_WIDTH
_OFF_U = _OFF_C + CONV_WIDTH
_OFF_ZA = _OFF_U + CONV_WIDTH
_OFF_Q = _OFF_ZA + CONV_WIDTH
_OFF_K = _OFF_Q + ATTN_WIDTH
_OFF_V = _OFF_K + KV_WIDTH
_OFF_ZB = _OFF_V + KV_WIDTH
_OFF_GA = _OFF_ZB + ATTN_WIDTH
_OFF_GB = _OFF_GA + D_MODEL
IN_COLS = _OFF_GB + D_MODEL

f32 = jnp.float32
bf16 = jnp.bfloat16


def _sigmoid(x):
    return 0.5 * jnp.tanh(0.5 * x) + 0.5


def _silu(x):
    return x * _sigmoid(x)


def _mod_kernel(c_ref, w_ref, b_ref, o_ref):
    s = _silu(c_ref[...])
    o_ref[0] = jnp.dot(s, w_ref[0], preferred_element_type=f32,
                       precision=lax.Precision.HIGHEST) + b_ref[0]


def _modulation(c_all, w_mod, b_mod):
    depth = w_mod.shape[0]
    n_col = 3 * D_MODEL // D_MODEL
    return pl.pallas_call(
        _mod_kernel,
        grid=(depth, n_col),
        in_specs=[
            pl.BlockSpec((MOD_ROWS, D_MODEL), lambda l, j: (0, 0)),
            pl.BlockSpec((1, D_MODEL, D_MODEL), lambda l, j: (l, 0, j)),
            pl.BlockSpec((1, 1, D_MODEL), lambda l, j: (l, 0, j)),
        ],
        out_specs=pl.BlockSpec((1, MOD_ROWS, D_MODEL), lambda l, j: (l, 0, j)),
        out_shape=jax.ShapeDtypeStruct((depth, MOD_ROWS, 3 * D_MODEL), f32),
        name="modulation",
    )(c_all, w_mod, b_mod.reshape(depth, 1, 3 * D_MODEL))


def _rope(p, cos, sin_lo, sin_hi):
    outs = []
    for j in range(p.shape[1] // LANES):
        x = p[:, j * LANES:(j + 1) * LANES]
        outs.append(x * cos
                    + pltpu.roll(x, LANES - ROPE_FREQS, 1) * sin_lo
                    + pltpu.roll(x, ROPE_FREQS, 1) * sin_hi)
    return outs[0] if len(outs) == 1 else jnp.concatenate(outs, axis=1)


def _with_swapped_halves(a):
    return jnp.concatenate([a, pltpu.roll(a, HEAD_DIM, 1)], axis=1)


def _norm_mod(x, nw, scale, shift):
    ms = jnp.mean(x * x, axis=-1, keepdims=True)
    y = x * lax.rsqrt(ms + EPS) * nw
    return (y * (1.0 + scale) + shift).astype(bf16)


def _proj_kernel(*refs, rope, tiles_per_seq):
    it = iter(refs)
    x_ref, xp_ref, xn_ref, nw_ref, shift_ref, scale_ref, w_ref, cw_ref = (
        next(it) for _ in range(8))
    if rope:
        cos_ref, slo_ref, shi_ref = (next(it) for _ in range(3))
    ya_ref, q_ref, k_ref, v_ref, zb_ref, ga_ref, gb_ref = (next(it) for _ in range(7))
    h_s, cu_s = next(it), next(it)
    tm = x_ref.shape[0]
    halo = 2 * SUBLANES

    nw, scale, shift = nw_ref[...], scale_ref[0], shift_ref[0]
    h_s[0:tm, :] = _norm_mod(x_ref[...], nw, scale, shift)
    h_s[tm:tm + halo, :] = _norm_mod(
        jnp.concatenate([xp_ref[...], xn_ref[...]], axis=0), nw, scale, shift)

    def proj(off, width, rows=tm):
        return jnp.dot(h_s[0:rows, :], w_ref[:, off:off + width], preferred_element_type=f32)

    cu = proj(_OFF_C, CONV_WIDTH, tm + halo) * proj(_OFF_U, CONV_WIDTH, tm + halo)
    t_in_seq = pl.program_id(0) % tiles_per_seq
    zero = jnp.zeros((SUBLANES, CONV_WIDTH), f32)
    cu_s[0:SUBLANES, :] = jnp.where(t_in_seq > 0, cu[tm:tm + SUBLANES], zero)
    cu_s[SUBLANES:SUBLANES + tm, :] = cu[0:tm]
    cu_s[SUBLANES + tm:, :] = jnp.where(t_in_seq < tiles_per_seq - 1, cu[tm + SUBLANES:], zero)
    cw = cw_ref[...]
    conv = (cu_s[SUBLANES - 1:SUBLANES - 1 + tm, :] * cw[0:1]
            + cu_s[SUBLANES:SUBLANES + tm, :] * cw[1:2]
            + cu_s[SUBLANES + 1:SUBLANES + 1 + tm, :] * cw[2:3])
    ya_ref[...] = (proj(_OFF_B, CONV_WIDTH) * conv * _silu(proj(_OFF_ZA, CONV_WIDTH))).astype(bf16)

    q = proj(_OFF_Q, ATTN_WIDTH)
    kv = proj(_OFF_K, 2 * KV_WIDTH)
    k = kv[:, :KV_WIDTH]
    if rope:
        cos, slo, shi = cos_ref[...], slo_ref[...], shi_ref[...]
        q = _rope(q, cos, slo, shi)
        k = _rope(k, cos, slo, shi)
    q_ref[...] = (q * (HEAD_DIM ** -0.5 * LOG2E)).astype(bf16)
    k_ref[...] = _with_swapped_halves(k).astype(bf16)
    v_ref[...] = _with_swapped_halves(kv[:, KV_WIDTH:]).astype(bf16)
    zb_ref[...] = _silu(proj(_OFF_ZB, ATTN_WIDTH)).astype(bf16)
    ga_ref[...] = _sigmoid(proj(_OFF_GA, D_MODEL)).astype(bf16)
    gb_ref[...] = _sigmoid(proj(_OFF_GB, D_MODEL)).astype(bf16)


def _projection(x2, norm_w, shift, scale, w_in, conv_w, rope_tabs, *, tm, seq, mod_row0,
                mod_per_seq):
    n_rows = x2.shape[0]
    tiles_per_seq = seq // tm
    halo_per_tile = tm // SUBLANES
    n_halo = n_rows // SUBLANES
    if mod_per_seq:
        mod_map = lambda t: (mod_row0 + t // tiles_per_seq, 0, 0)
    else:
        mod_map = lambda t: (mod_row0, 0, 0)
    row = lambda w: pl.BlockSpec((tm, w), lambda t: (t, 0))
    in_specs = [
        row(D_MODEL),
        pl.BlockSpec((SUBLANES, D_MODEL),
                     lambda t: (jnp.maximum(t * halo_per_tile - 1, 0), 0)),
        pl.BlockSpec((SUBLANES, D_MODEL),
                     lambda t: (jnp.minimum((t + 1) * halo_per_tile, n_halo - 1), 0)),
        pl.BlockSpec((1, D_MODEL), lambda t: (0, 0)),
        pl.BlockSpec((1, 1, D_MODEL), mod_map),
        pl.BlockSpec((1, 1, D_MODEL), mod_map),
        pl.BlockSpec((D_MODEL, IN_COLS), lambda t: (0, 0)),
        pl.BlockSpec((CONV_K, CONV_WIDTH), lambda t: (0, 0)),
    ]
    args = [x2, x2, x2, norm_w.reshape(1, D_MODEL), shift, scale, w_in, conv_w]
    if rope_tabs is not None:
        in_specs += [pl.BlockSpec((tm, LANES), lambda t: (t % tiles_per_seq, 0))] * 3
        args += list(rope_tabs)
    widths = (CONV_WIDTH, ATTN_WIDTH, 2 * KV_WIDTH, 2 * KV_WIDTH, ATTN_WIDTH, D_MODEL, D_MODEL)
    return pl.pallas_call(
        functools.partial(_proj_kernel, rope=rope_tabs is not None,
                          tiles_per_seq=tiles_per_seq),
        grid=(n_rows // tm,),
        in_specs=in_specs,
        out_specs=[row(w) for w in widths],
        out_shape=[jax.ShapeDtypeStruct((n_rows, w), bf16) for w in widths],
        scratch_shapes=[pltpu.VMEM((tm + 2 * SUBLANES, D_MODEL), bf16),
                        pltpu.VMEM((tm + 2 * SUBLANES, CONV_WIDTH), f32)],
        compiler_params=pltpu.CompilerParams(
            dimension_semantics=("arbitrary",), vmem_limit_bytes=VMEM_LIMIT),
        name="projection_rope" if rope_tabs is not None else "projection",
    )(*args)


def _kv_variants(a2):
    a, sw = a2[:, :LANES], a2[:, LANES:]
    low = lax.broadcasted_iota(jnp.int32, a.shape, 1) < HEAD_DIM
    z = jnp.zeros_like(a)
    return [jnp.where(low, a, z), jnp.where(low, z, sw),
            jnp.where(low, sw, z), jnp.where(low, z, a)]


def _mix_kernel(*refs, tq, has_local, final):
    it = iter(refs)
    sink_ref = next(it)
    x_ref, gate_ref, ya_ref, q_ref, zb_ref, ga_ref, gb_ref = (next(it) for _ in range(7))
    if has_local:
        k_ref, kp_ref, kn_ref, v_ref, vp_ref, vn_ref = (next(it) for _ in range(6))
    kc_ref, vc_ref = next(it), next(it)
    wa_ref, wb_ref, wo_ref = (next(it) for _ in range(3))
    if final:
        fw_ref = next(it)
    o_ref = next(it)
    y_s, yb_s, k_s, vm_s, s_s, p_s, z_s = (next(it) for _ in range(7))
    if has_local:
        bias_s = next(it)

    i = pl.program_id(1)
    n_i = pl.num_programs(1)
    nblk = tq // BLOCK
    ctx_len = kc_ref.shape[0]
    n_loc = 3 * BLOCK if has_local else 0
    nk = ctx_len + n_loc
    n_slot = vm_s.shape[0]
    rows_a = SLABS_PER_KV * BLOCK

    @pl.when(i == 0)
    def _():
        kvar = _kv_variants(kc_ref[...])
        vvar = _kv_variants(vc_ref[...])
        low = lax.broadcasted_iota(jnp.int32, (nk, LANES), 1) < HEAD_DIM
        ones_on = [jnp.where(low, 1.0, 0.0).astype(bf16), jnp.where(low, 0.0, 1.0).astype(bf16)]
        for n in range(4):
            k_s[n, 0:ctx_len, :] = kvar[n]
        for slot in range(n_slot):
            for kh in range(N_KV_HEADS):
                for e in range(HEADS_PER_SLAB):
                    vm_s[slot, kh, e * nk:e * nk + ctx_len, 0:LANES] = vvar[2 * kh + e]
                    vm_s[slot, kh, e * nk:(e + 1) * nk, LANES:] = ones_on[e]

    def local_block(ref_prev, ref_tile, ref_next, t):
        if t == 0:
            return ref_prev[...]
        if t == nblk + 1:
            return ref_next[...]
        return ref_tile[(t - 1) * BLOCK:t * BLOCK, :]

    def fill_values(j, b):
        vvar = _kv_variants(local_block(vp_ref, v_ref, vn_ref, j + b))
        for kh in range(N_KV_HEADS):
            for e in range(HEADS_PER_SLAB):
                r = e * nk + ctx_len + b * BLOCK
                vm_s[j, kh, r:r + BLOCK, 0:LANES] = vvar[2 * kh + e]

    def fill_keys(t):
        for n, a in enumerate(_kv_variants(local_block(kp_ref, k_ref, kn_ref, t))):
            k_s[n, ctx_len + t * BLOCK:ctx_len + (t + 1) * BLOCK, :] = a

    def fill_bias():
        qi = lax.broadcasted_iota(jnp.int32, (rows_a, BLOCK), 0) % BLOCK
        ci = lax.broadcasted_iota(jnp.int32, (rows_a, BLOCK), 1)
        tri_prev = jnp.where(ci >= qi, 0.0, NEG)
        tri_next = jnp.where(ci <= qi, 0.0, NEG)
        bias_s[0] = tri_prev + jnp.where(i == 0, NEG, 0.0)
        bias_s[1] = tri_prev
        bias_s[2] = tri_next
        bias_s[3] = tri_next + jnp.where(i == n_i - 1, NEG, 0.0)

    def conv_branch(n, width):
        cols = slice(n * width, (n + 1) * width)
        y_s[:, cols] = (jnp.dot(ya_ref[...], wa_ref[:, cols], preferred_element_type=f32)
                        * ga_ref[:, cols].astype(f32))

    fills = ([functools.partial(fill_keys, t) for t in range(nblk + 2)] + [fill_bias]
             if has_local else [])
    n_part = 4
    for n in range(n_part):
        conv_branch(n, D_MODEL // n_part)
        for fill in fills[n::n_part]:
            fill()

    nt = (((1,), (1,)), ((), ()))

    def stage_a(j, kh, e):
        u = (j * N_KV_HEADS + kh) * HEADS_PER_SLAB + e
        rows = slice(j * BLOCK, (j + 1) * BLOCK)
        q2 = jnp.concatenate(
            [q_ref[rows, (kh * SLABS_PER_KV + s) * LANES:(kh * SLABS_PER_KV + s + 1) * LANES]
             for s in range(SLABS_PER_KV)], axis=0)
        var = 2 * kh + e
        s_s[u, :, 0:ctx_len] = lax.dot_general(
            q2, k_s[var, 0:ctx_len, :], nt, preferred_element_type=f32)
        if has_local:
            r0 = ctx_len + j * BLOCK
            s_loc = lax.dot_general(q2, k_s[var, r0:r0 + n_loc, :], nt,
                                    preferred_element_type=f32)
            c0 = ctx_len
            s_s[u, :, c0:c0 + BLOCK] = s_loc[:, :BLOCK] + bias_s[0 if j == 0 else 1]
            s_s[u, :, c0 + BLOCK:c0 + 2 * BLOCK] = s_loc[:, BLOCK:2 * BLOCK]
            s_s[u, :, c0 + 2 * BLOCK:] = s_loc[:, 2 * BLOCK:] + bias_s[3 if j == nblk - 1 else 2]

    def stage_b(j, kh, e):
        u = (j * N_KV_HEADS + kh) * HEADS_PER_SLAB + e
        jk = j * N_KV_HEADS + kh
        for s in range(SLABS_PER_KV):
            rs = slice(s * BLOCK, (s + 1) * BLOCK)
            sk = sink_ref[(kh * SLABS_PER_KV + s) * HEADS_PER_SLAB + e] * LOG2E
            mx = s_s[u, rs, 0:LANES]
            for n in range(1, nk // LANES):
                mx = jnp.maximum(mx, s_s[u, rs, n * LANES:(n + 1) * LANES])
            m = jnp.maximum(jnp.max(mx, axis=1, keepdims=True), sk)
            for n in range(nk // LANES):
                col = e * nk + n * LANES
                p_s[jk, rs, col:col + LANES] = jnp.exp2(
                    s_s[u, rs, n * LANES:(n + 1) * LANES] - m).astype(bf16)
            z_s[jk, rs, e * HEAD_DIM:(e + 1) * HEAD_DIM] = jnp.broadcast_to(
                jnp.exp2(sk - m), (BLOCK, HEAD_DIM))

    def stage_c(j, kh):
        jk = j * N_KV_HEADS + kh
        res = jnp.dot(p_s[jk], vm_s[j if has_local else 0, kh], preferred_element_type=f32)
        attn = res[:, :LANES] / (res[:, LANES:] + z_s[jk])
        rows = slice(j * BLOCK, (j + 1) * BLOCK)
        for s in range(SLABS_PER_KV):
            c = kh * SLABS_PER_KV + s
            lanes = slice(c * LANES, (c + 1) * LANES)
            yb_s[rows, lanes] = (attn[s * BLOCK:(s + 1) * BLOCK]
                                 * zb_ref[rows, lanes].astype(f32)).astype(bf16)

    for t in range(nblk + 2):
        for kh in range(N_KV_HEADS):
            if has_local and 0 <= t - 1 < nblk:
                for b in range(3)[kh::N_KV_HEADS]:
                    fill_values(t - 1, b)
            for e in range(HEADS_PER_SLAB):
                if t < nblk:
                    stage_a(t, kh, e)
                if 0 <= t - 1 < nblk:
                    stage_b(t - 1, kh, e)
            if 0 <= t - 2 < nblk:
                stage_c(t - 2, kh)

    y = y_s[...] + (jnp.dot(yb_s[...], wb_ref[...], preferred_element_type=f32)
                    * gb_ref[...].astype(f32))
    y16 = y.astype(bf16)
    if final:
        xn = x_ref[...] + gate_ref[0] * jnp.dot(y16, wo_ref[...], preferred_element_type=f32)
        ms = jnp.mean(xn * xn, axis=-1, keepdims=True)
        o_ref[...] = xn * lax.rsqrt(ms + EPS) * fw_ref[...]
    else:
        width = D_MODEL // n_part
        for n in range(n_part):
            cols = slice(n * width, (n + 1) * width)
            o_ref[:, cols] = x_ref[:, cols] + gate_ref[0, :, cols] * jnp.dot(
                y16, wo_ref[:, cols], preferred_element_type=f32)


def _mixer(x2, gate, sink, proj_out, kc, vc, wa, wb, wo, final_w, *,
           seq, tq, has_local, gate_row0, gate_per_batch):
    ya, q, k, v, zb, ga, gb = proj_out
    n_rows = x2.shape[0]
    batch = n_rows // seq
    ctx_len = kc.shape[0] // batch
    n_i = seq // tq
    nblk = tq // BLOCK
    final = final_w is not None

    def row(w):
        return pl.BlockSpec((tq, w), lambda b, i: (b * n_i + i, 0))

    def halo(rows_blk, w, side):
        per_tile = tq // rows_blk
        per_seq = seq // rows_blk
        if side < 0:
            return pl.BlockSpec((rows_blk, w), lambda b, i: (
                b * per_seq + jnp.maximum(i * per_tile - 1, 0), 0))
        return pl.BlockSpec((rows_blk, w), lambda b, i: (
            b * per_seq + jnp.minimum((i + 1) * per_tile, per_seq - 1), 0))

    def whole(shape):
        return pl.BlockSpec(shape, lambda b, i: (0,) * len(shape))

    if gate_per_batch:
        gate_spec = pl.BlockSpec((1, 1, D_MODEL), lambda b, i: (gate_row0 + b, 0, 0))
    else:
        gate_spec = pl.BlockSpec((1, 1, D_MODEL), lambda b, i: (gate_row0, 0, 0))

    kvw = 2 * KV_WIDTH
    in_specs = [pl.BlockSpec(memory_space=pltpu.SMEM),
                row(D_MODEL), gate_spec, row(CONV_WIDTH),
                row(ATTN_WIDTH), row(ATTN_WIDTH), row(D_MODEL), row(D_MODEL)]
    args = [sink, x2, gate, ya, q, zb, ga, gb]
    if has_local:
        in_specs += [row(kvw), halo(BLOCK, kvw, -1), halo(BLOCK, kvw, +1)] * 2
        args += [k, k, k, v, v, v]
    ctx_spec = pl.BlockSpec((ctx_len, kvw), lambda b, i: (b, 0))
    in_specs += [ctx_spec, ctx_spec, whole((CONV_WIDTH, D_MODEL)),
                 whole((ATTN_WIDTH, D_MODEL)), whole((D_MODEL, D_MODEL))]
    args += [kc, vc, wa, wb, wo]
    if final:
        in_specs.append(whole((1, D_MODEL)))
        args.append(final_w.reshape(1, D_MODEL))

    nk = ctx_len + (3 * BLOCK if has_local else 0)
    k_rows = ctx_len + tq + 2 * BLOCK if has_local else ctx_len
    n_slot = nblk if has_local else 1
    rows_a = SLABS_PER_KV * BLOCK
    scratch = [pltpu.VMEM((tq, D_MODEL), f32),
               pltpu.VMEM((tq, ATTN_WIDTH), bf16),
               pltpu.VMEM((2 * N_KV_HEADS, k_rows, LANES), bf16),
               pltpu.VMEM((n_slot, N_KV_HEADS, HEADS_PER_SLAB * nk, 2 * LANES), bf16),
               pltpu.VMEM((nblk * N_KV_HEADS * HEADS_PER_SLAB, rows_a, nk), f32),
               pltpu.VMEM((nblk * N_KV_HEADS, rows_a, HEADS_PER_SLAB * nk), bf16),
               pltpu.VMEM((nblk * N_KV_HEADS, rows_a, LANES), f32)]
    if has_local:
        scratch += [pltpu.VMEM((4, rows_a, BLOCK), f32)]

    return pl.pallas_call(
        functools.partial(_mix_kernel, tq=tq, has_local=has_local, final=final),
        grid=(batch, n_i),
        in_specs=in_specs,
        out_specs=row(D_MODEL),
        out_shape=jax.ShapeDtypeStruct((n_rows, D_MODEL), f32),
        scratch_shapes=scratch,
        compiler_params=pltpu.CompilerParams(
            dimension_semantics=("arbitrary", "arbitrary"), vmem_limit_bytes=VMEM_LIMIT),
        name="mixer_latent" if has_local else "mixer_context",
    )(*args)


def _rope_tables(seq):
    pos = np.arange(seq)
    inv_freq = jnp.asarray(ROPE_THETA, f32) ** (-jnp.arange(ROPE_FREQS, dtype=f32) / ROPE_FREQS)
    lane = np.arange(LANES)
    use_col = (lane % HEAD_DIM) >= HEAD_DIM // 2
    upper = (lane % (2 * ROPE_FREQS)) >= ROPE_FREQS
    coord = jnp.where(use_col[None, :], (pos % GRID_W)[:, None], (pos // GRID_W)[:, None])
    ang = coord.astype(f32) * inv_freq[lane % ROPE_FREQS][None, :]
    cos, sin = jnp.cos(ang), jnp.sin(ang)
    zero = jnp.zeros_like(sin)
    sin_lo = jnp.where(upper[None, :], zero, -sin)
    sin_hi = jnp.where(upper[None, :], sin, zero)
    return cos, sin_lo, sin_hi


def kernel(x, c, ctx, c_ctx, norm_w, w_mod, b_mod, w_in, conv_w, w_a_out, w_b_out,
           attn_sink, w_o, final_norm_w):
    batch, seq, _ = x.shape
    ctx_len = ctx.shape[1]
    depth = w_in.shape[0]
    assert batch + 1 <= MOD_ROWS and seq % 512 == 0 and ctx_len % BLOCK == 0

    c_all = jnp.zeros((MOD_ROWS, D_MODEL), f32).at[:batch].set(c).at[batch].set(c_ctx)
    mod = _modulation(c_all, w_mod, b_mod).reshape(depth, MOD_ROWS, 3, 1, D_MODEL)
    rope_tabs = _rope_tables(seq)

    w_in_b = w_in.astype(bf16)
    wa_b, wb_b, wo_b = w_a_out.astype(bf16), w_b_out.astype(bf16), w_o.astype(bf16)

    x2 = x.reshape(batch * seq, D_MODEL)
    c2 = ctx.reshape(batch * ctx_len, D_MODEL)
    for l in range(depth):
        last = l == depth - 1
        shift, scale, gate = mod[l, :, 0], mod[l, :, 1], mod[l, :, 2]
        pc = _projection(c2, norm_w[l], shift, scale, w_in_b[l], conv_w[l], None,
                         tm=ctx_len, seq=ctx_len, mod_row0=batch, mod_per_seq=False)
        px = _projection(x2, norm_w[l], shift, scale, w_in_b[l], conv_w[l], rope_tabs,
                         tm=512, seq=seq, mod_row0=0, mod_per_seq=True)
        kc, vc = pc[2], pc[3]
        x2 = _mixer(x2, gate, attn_sink[l], px, kc, vc, wa_b[l], wb_b[l], wo_b[l],
                    final_norm_w if last else None,
                    seq=seq, tq=512, has_local=True, gate_row0=0, gate_per_batch=True)
        if not last:
            c2 = _mixer(c2, gate, attn_sink[l], pc, kc, vc, wa_b[l], wb_b[l],
                        wo_b[l], None,
                        seq=ctx_len, tq=ctx_len, has_local=False, gate_row0=batch,
                        gate_per_batch=False)
    return x2.reshape(batch, seq, D_MODEL)
```

```python
import functools
import math

import numpy as np
import jax
import jax.numpy as jnp
from jax import lax
from jax.experimental import pallas as pl
from jax.experimental.pallas import tpu as pltpu

D_MODEL = 1024
GRID_W = 64
CONV_WIDTH = 512
CONV_K = 3
N_HEADS = 8
N_KV_HEADS = 2
HEAD_DIM = 64
ATTN_WIDTH = N_HEADS * HEAD_DIM
KV_WIDTH = N_KV_HEADS * HEAD_DIM
BLOCK = 128
ROPE_THETA = 10000.0
ROPE_FREQS = HEAD_DIM // 4
EPS = 1e-6
NEG = -1e30
LOG2E = math.log2(math.e)

LANES = 128
SUBLANES = 8
MOD_ROWS = 24
VMEM_LIMIT = 56 * 1024 * 1024
N_SLAB = ATTN_WIDTH // LANES
HEADS_PER_SLAB = LANES // HEAD_DIM
SLABS_PER_KV = N_SLAB // N_KV_HEADS

_OFF_B = 0
_OFF_C = _OFF_B + CONV_WIDTH
_OFF_U = _OFF_C + CONV_WIDTH
_OFF_ZA = _OFF_U + CONV_WIDTH
_OFF_Q = _OFF_ZA + CONV_WIDTH
_OFF_K = _OFF_Q + ATTN_WIDTH
_OFF_V = _OFF_K + KV_WIDTH
_OFF_ZB = _OFF_V + KV_WIDTH
_OFF_GA = _OFF_ZB + ATTN_WIDTH
_OFF_GB = _OFF_GA + D_MODEL
IN_COLS = _OFF_GB + D_MODEL

f32 = jnp.float32
bf16 = jnp.bfloat16


def _sigmoid(x):
    return 0.5 * jnp.tanh(0.5 * x) + 0.5


def _silu(x):
    return x * _sigmoid(x)


def _mod_kernel(c_ref, w_ref, b_ref, o_ref):
    s = _silu(c_ref[...])
    o_ref[0] = jnp.dot(s, w_ref[0], preferred_element_type=f32,
                       precision=lax.Precision.HIGHEST) + b_ref[0]


def _modulation(c_all, w_mod, b_mod):
    depth = w_mod.shape[0]
    n_col = 3 * D_MODEL // D_MODEL
    return pl.pallas_call(
        _mod_kernel,
        grid=(depth, n_col),
        in_specs=[
            pl.BlockSpec((MOD_ROWS, D_MODEL), lambda l, j: (0, 0)),
            pl.BlockSpec((1, D_MODEL, D_MODEL), lambda l, j: (l, 0, j)),
            pl.BlockSpec((1, 1, D_MODEL), lambda l, j: (l, 0, j)),
        ],
        out_specs=pl.BlockSpec((1, MOD_ROWS, D_MODEL), lambda l, j: (l, 0, j)),
        out_shape=jax.ShapeDtypeStruct((depth, MOD_ROWS, 3 * D_MODEL), f32),
        name="modulation",
    )(c_all, w_mod, b_mod.reshape(depth, 1, 3 * D_MODEL))


def _rope(p, cos, sin_lo, sin_hi):
    outs = []
    for j in range(p.shape[1] // LANES):
        x = p[:, j * LANES:(j + 1) * LANES]
        outs.append(x * cos
                    + pltpu.roll(x, LANES - ROPE_FREQS, 1) * sin_lo
                    + pltpu.roll(x, ROPE_FREQS, 1) * sin_hi)
    return outs[0] if len(outs) == 1 else jnp.concatenate(outs, axis=1)


def _with_swapped_halves(a):
    return jnp.concatenate([a, pltpu.roll(a, HEAD_DIM, 1)], axis=1)


def _norm_mod(x, nw, scale, shift):
    ms = jnp.mean(x * x, axis=-1, keepdims=True)
    y = x * lax.rsqrt(ms + EPS) * nw
    return (y * (1.0 + scale) + shift).astype(bf16)


def _proj_kernel(*refs, rope, tiles_per_seq):
    it = iter(refs)
    x_ref, xp_ref, xn_ref, nw_ref, shift_ref, scale_ref, w_ref, cw_ref = (
        next(it) for _ in range(8))
    if rope:
        cos_ref, slo_ref, shi_ref = (next(it) for _ in range(3))
    ya_ref, q_ref, k_ref, v_ref, zb_ref, ga_ref, gb_ref = (next(it) for _ in range(7))
    h_s, cu_s = next(it), next(it)
    tm = x_ref.shape[0]
    halo = 2 * SUBLANES

    nw, scale, shift = nw_ref[...], scale_ref[...], shift_ref[...]
    h_s[0:tm, :] = _norm_mod(x_ref[...], nw, scale, shift)
    h_s[tm:tm + halo, :] = _norm_mod(
        jnp.concatenate([xp_ref[...], xn_ref[...]], axis=0), nw, scale, shift)

    def proj(off, width, rows=tm):
        return jnp.dot(h_s[0:rows, :], w_ref[:, off:off + width], preferred_element_type=f32)

    cu = proj(_OFF_C, CONV_WIDTH, tm + halo) * proj(_OFF_U, CONV_WIDTH, tm + halo)
    t_in_seq = pl.program_id(0) % tiles_per_seq
    zero = jnp.zeros((SUBLANES, CONV_WIDTH), f32)
    cu_s[0:SUBLANES, :] = jnp.where(t_in_seq > 0, cu[tm:tm + SUBLANES], zero)
    cu_s[SUBLANES:SUBLANES + tm, :] = cu[0:tm]
    cu_s[SUBLANES + tm:, :] = jnp.where(t_in_seq < tiles_per_seq - 1, cu[tm + SUBLANES:], zero)
    cw = cw_ref[...]
    conv = (cu_s[SUBLANES - 1:SUBLANES - 1 + tm, :] * cw[0:1]
            + cu_s[SUBLANES:SUBLANES + tm, :] * cw[1:2]
            + cu_s[SUBLANES + 1:SUBLANES + 1 + tm, :] * cw[2:3])
    ya_ref[...] = (proj(_OFF_B, CONV_WIDTH) * conv * _silu(proj(_OFF_ZA, CONV_WIDTH))).astype(bf16)

    q = proj(_OFF_Q, ATTN_WIDTH)
    kv = proj(_OFF_K, 2 * KV_WIDTH)
    k = kv[:, :KV_WIDTH]
    if rope:
        cos, slo, shi = cos_ref[...], slo_ref[...], shi_ref[...]
        q = _rope(q, cos, slo, shi)
        k = _rope(k, cos, slo, shi)
    q_ref[...] = (q * (HEAD_DIM ** -0.5 * LOG2E)).astype(bf16)
    k_ref[...] = _with_swapped_halves(k).astype(bf16)
    v_ref[...] = _with_swapped_halves(kv[:, KV_WIDTH:]).astype(bf16)
    zb_ref[...] = _silu(proj(_OFF_ZB, ATTN_WIDTH)).astype(bf16)
    part = 2 * LANES
    for n in range(D_MODEL // part):
        ga_ref[:, n * part:(n + 1) * part] = _sigmoid(proj(_OFF_GA + n * part, part)).astype(bf16)
    for n in range(D_MODEL // part):
        gb_ref[:, n * part:(n + 1) * part] = _sigmoid(proj(_OFF_GB + n * part, part)).astype(bf16)


def _mod_spec(layer, row_of, which):
    return pl.BlockSpec((None, None, None, 1, D_MODEL),
                        lambda *g: (layer, row_of(*g), which, 0, 0))


def _layer_spec(layer, shape):
    return pl.BlockSpec((None,) + tuple(shape), lambda *g: (layer,) + (0,) * len(shape))


def _projection(x2, layer, norm_w, mod, w_in, conv_w, rope_tabs, *, tm, seq, mod_row0,
                mod_per_seq):
    n_rows = x2.shape[0]
    tiles_per_seq = seq // tm
    halo_per_tile = tm // SUBLANES
    n_halo = n_rows // SUBLANES
    if mod_per_seq:
        mod_row = lambda t: mod_row0 + t // tiles_per_seq
    else:
        mod_row = lambda t: mod_row0
    row = lambda w: pl.BlockSpec((tm, w), lambda t: (t, 0))
    in_specs = [
        row(D_MODEL),
        pl.BlockSpec((SUBLANES, D_MODEL),
                     lambda t: (jnp.maximum(t * halo_per_tile - 1, 0), 0)),
        pl.BlockSpec((SUBLANES, D_MODEL),
                     lambda t: (jnp.minimum((t + 1) * halo_per_tile, n_halo - 1), 0)),
        _layer_spec(layer, (1, D_MODEL)),
        _mod_spec(layer, mod_row, 0),
        _mod_spec(layer, mod_row, 1),
        _layer_spec(layer, (D_MODEL, IN_COLS)),
        _layer_spec(layer, (CONV_K, CONV_WIDTH)),
    ]
    args = [x2, x2, x2, norm_w.reshape(-1, 1, D_MODEL), mod, mod, w_in, conv_w]
    if rope_tabs is not None:
        in_specs += [pl.BlockSpec((tm, LANES), lambda t: (t % tiles_per_seq, 0))] * 3
        args += list(rope_tabs)
    widths = (CONV_WIDTH, ATTN_WIDTH, 2 * KV_WIDTH, 2 * KV_WIDTH, ATTN_WIDTH, D_MODEL, D_MODEL)
    return pl.pallas_call(
        functools.partial(_proj_kernel, rope=rope_tabs is not None,
                          tiles_per_seq=tiles_per_seq),
        grid=(n_rows // tm,),
        in_specs=in_specs,
        out_specs=[row(w) for w in widths],
        out_shape=[jax.ShapeDtypeStruct((n_rows, w), bf16) for w in widths],
        scratch_shapes=[pltpu.VMEM((tm + 2 * SUBLANES, D_MODEL), bf16),
                        pltpu.VMEM((tm + 2 * SUBLANES, CONV_WIDTH), f32)],
        compiler_params=pltpu.CompilerParams(
            dimension_semantics=("arbitrary",), vmem_limit_bytes=VMEM_LIMIT),
        name="projection_rope" if rope_tabs is not None else "projection",
    )(*args)


def _kv_variants(a2):
    a, sw = a2[:, :LANES], a2[:, LANES:]
    low = lax.broadcasted_iota(jnp.int32, a.shape, 1) < HEAD_DIM
    z = jnp.zeros_like(a)
    return [jnp.where(low, a, z), jnp.where(low, z, sw),
            jnp.where(low, sw, z), jnp.where(low, z, a)]


def _mix_kernel(*refs, layer, tq, has_local, final):
    it = iter(refs)
    sink_ref = next(it)
    x_ref, gate_ref, ya_ref, q_ref, zb_ref, ga_ref, gb_ref = (next(it) for _ in range(7))
    if has_local:
        k_ref, kp_ref, kn_ref, v_ref, vp_ref, vn_ref = (next(it) for _ in range(6))
    kc_ref, vc_ref = next(it), next(it)
    wa_ref, wb_ref, wo_ref = (next(it) for _ in range(3))
    if final:
        fw_ref = next(it)
    o_ref = next(it)
    y_s, yb_s, k_s, vm_s, s_s, p_s, z_s = (next(it) for _ in range(7))
    if has_local:
        bias_s = next(it)

    i = pl.program_id(1)
    n_i = pl.num_programs(1)
    nblk = tq // BLOCK
    ctx_len = kc_ref.shape[0]
    n_loc = 3 * BLOCK if has_local else 0
    nk = ctx_len + n_loc
    n_slot = vm_s.shape[0]
    rows_a = SLABS_PER_KV * BLOCK

    @pl.when(i == 0)
    def _():
        kvar = _kv_variants(kc_ref[...])
        vvar = _kv_variants(vc_ref[...])
        low = lax.broadcasted_iota(jnp.int32, (nk, LANES), 1) < HEAD_DIM
        ones_on = [jnp.where(low, 1.0, 0.0).astype(bf16), jnp.where(low, 0.0, 1.0).astype(bf16)]
        for n in range(4):
            k_s[n, 0:ctx_len, :] = kvar[n]
        for slot in range(n_slot):
            for kh in range(N_KV_HEADS):
                for e in range(HEADS_PER_SLAB):
                    vm_s[slot, kh, e * nk:e * nk + ctx_len, 0:LANES] = vvar[2 * kh + e]
                    vm_s[slot, kh, e * nk:(e + 1) * nk, LANES:] = ones_on[e]

    def local_block(ref_prev, ref_tile, ref_next, t):
        if t == 0:
            return ref_prev[...]
        if t == nblk + 1:
            return ref_next[...]
        return ref_tile[(t - 1) * BLOCK:t * BLOCK, :]

    def fill_values(j, b):
        vvar = _kv_variants(local_block(vp_ref, v_ref, vn_ref, j + b))
        for kh in range(N_KV_HEADS):
            for e in range(HEADS_PER_SLAB):
                r = e * nk + ctx_len + b * BLOCK
                vm_s[j, kh, r:r + BLOCK, 0:LANES] = vvar[2 * kh + e]

    def fill_keys(t):
        for n, a in enumerate(_kv_variants(local_block(kp_ref, k_ref, kn_ref, t))):
            k_s[n, ctx_len + t * BLOCK:ctx_len + (t + 1) * BLOCK, :] = a

    def fill_bias():
        qi = lax.broadcasted_iota(jnp.int32, (rows_a, BLOCK), 0) % BLOCK
        ci = lax.broadcasted_iota(jnp.int32, (rows_a, BLOCK), 1)
        tri_prev = jnp.where(ci >= qi, 0.0, NEG)
        tri_next = jnp.where(ci <= qi, 0.0, NEG)
        bias_s[0] = tri_prev + jnp.where(i == 0, NEG, 0.0)
        bias_s[1] = tri_prev
        bias_s[2] = tri_next
        bias_s[3] = tri_next + jnp.where(i == n_i - 1, NEG, 0.0)

    def conv_branch(n, width):
        cols = slice(n * width, (n + 1) * width)
        y_s[:, cols] = (jnp.dot(ya_ref[...], wa_ref[:, cols], preferred_element_type=f32)
                        * ga_ref[:, cols].astype(f32))

    fills = ([functools.partial(fill_keys, t) for t in range(nblk + 2)] + [fill_bias]
             if has_local else [])
    n_part = 4
    for n in range(n_part):
        conv_branch(n, D_MODEL // n_part)
        for fill in fills[n::n_part]:
            fill()

    nt = (((1,), (1,)), ((), ()))

    def stage_a(j, kh, e):
        u = (j * N_KV_HEADS + kh) * HEADS_PER_SLAB + e
        rows = slice(j * BLOCK, (j + 1) * BLOCK)
        q2 = jnp.concatenate(
            [q_ref[rows, (kh * SLABS_PER_KV + s) * LANES:(kh * SLABS_PER_KV + s + 1) * LANES]
             for s in range(SLABS_PER_KV)], axis=0)
        var = 2 * kh + e
        s_s[u, :, 0:ctx_len] = lax.dot_general(
            q2, k_s[var, 0:ctx_len, :], nt, preferred_element_type=f32)
        if has_local:
            r0 = ctx_len + j * BLOCK
            s_loc = lax.dot_general(q2, k_s[var, r0:r0 + n_loc, :], nt,
                                    preferred_element_type=f32)
            c0 = ctx_len
            s_s[u, :, c0:c0 + BLOCK] = s_loc[:, :BLOCK] + bias_s[0 if j == 0 else 1]
            s_s[u, :, c0 + BLOCK:c0 + 2 * BLOCK] = s_loc[:, BLOCK:2 * BLOCK]
            s_s[u, :, c0 + 2 * BLOCK:] = s_loc[:, 2 * BLOCK:] + bias_s[3 if j == nblk - 1 else 2]

    def stage_b(j, kh, e):
        u = (j * N_KV_HEADS + kh) * HEADS_PER_SLAB + e
        jk = j * N_KV_HEADS + kh
        for s in range(SLABS_PER_KV):
            rs = slice(s * BLOCK, (s + 1) * BLOCK)
            sk = sink_ref[layer, (kh * SLABS_PER_KV + s) * HEADS_PER_SLAB + e] * LOG2E
            mx = s_s[u, rs, 0:LANES]
            for n in range(1, nk // LANES):
                mx = jnp.maximum(mx, s_s[u, rs, n * LANES:(n + 1) * LANES])
            m = jnp.maximum(jnp.max(mx, axis=1, keepdims=True), sk)
            for n in range(nk // LANES):
                col = e * nk + n * LANES
                p_s[jk, rs, col:col + LANES] = jnp.exp2(
                    s_s[u, rs, n * LANES:(n + 1) * LANES] - m).astype(bf16)
            z_s[jk, rs, e * HEAD_DIM:(e + 1) * HEAD_DIM] = jnp.broadcast_to(
                jnp.exp2(sk - m), (BLOCK, HEAD_DIM))

    def stage_c(j, kh):
        jk = j * N_KV_HEADS + kh
        res = jnp.dot(p_s[jk], vm_s[j if has_local else 0, kh], preferred_element_type=f32)
        attn = res[:, :LANES] / (res[:, LANES:] + z_s[jk])
        rows = slice(j * BLOCK, (j + 1) * BLOCK)
        for s in range(SLABS_PER_KV):
            c = kh * SLABS_PER_KV + s
            lanes = slice(c * LANES, (c + 1) * LANES)
            yb_s[rows, lanes] = (attn[s * BLOCK:(s + 1) * BLOCK]
                                 * zb_ref[rows, lanes].astype(f32)).astype(bf16)

    for t in range(nblk + 2):
        for kh in range(N_KV_HEADS):
            if has_local and 0 <= t - 1 < nblk:
                for b in range(3)[kh::N_KV_HEADS]:
                    fill_values(t - 1, b)
            for e in range(HEADS_PER_SLAB):
                if t < nblk:
                    stage_a(t, kh, e)
                if 0 <= t - 1 < nblk:
                    stage_b(t - 1, kh, e)
            if 0 <= t - 2 < nblk:
                stage_c(t - 2, kh)

    y = y_s[...] + (jnp.dot(yb_s[...], wb_ref[...], preferred_element_type=f32)
                    * gb_ref[...].astype(f32))
    y16 = y.astype(bf16)
    if final:
        xn = x_ref[...] + gate_ref[...] * jnp.dot(y16, wo_ref[...], preferred_element_type=f32)
        ms = jnp.mean(xn * xn, axis=-1, keepdims=True)
        o_ref[...] = xn * lax.rsqrt(ms + EPS) * fw_ref[...]
    else:
        width = D_MODEL // n_part
        for n in range(n_part):
            cols = slice(n * width, (n + 1) * width)
            o_ref[:, cols] = x_ref[:, cols] + gate_ref[:, cols] * jnp.dot(
                y16, wo_ref[:, cols], preferred_element_type=f32)


def _mixer(x2, layer, mod, sink, proj_out, kc, vc, wa, wb, wo, final_w, *,
           seq, tq, has_local, gate_row0, gate_per_batch):
    ya, q, k, v, zb, ga, gb = proj_out
    n_rows = x2.shape[0]
    batch = n_rows // seq
    ctx_len = kc.shape[0] // batch
    n_i = seq // tq
    nblk = tq // BLOCK
    final = final_w is not None

    def row(w):
        return pl.BlockSpec((tq, w), lambda b, i: (b * n_i + i, 0))

    def halo(rows_blk, w, side):
        per_tile = tq // rows_blk
        per_seq = seq // rows_blk
        if side < 0:
            return pl.BlockSpec((rows_blk, w), lambda b, i: (
                b * per_seq + jnp.maximum(i * per_tile - 1, 0), 0))
        return pl.BlockSpec((rows_blk, w), lambda b, i: (
            b * per_seq + jnp.minimum((i + 1) * per_tile, per_seq - 1), 0))

    def whole(shape):
        return pl.BlockSpec(shape, lambda b, i: (0,) * len(shape))

    if gate_per_batch:
        gate_spec = _mod_spec(layer, lambda b, i: gate_row0 + b, 2)
    else:
        gate_spec = _mod_spec(layer, lambda b, i: gate_row0, 2)

    kvw = 2 * KV_WIDTH
    in_specs = [pl.BlockSpec(memory_space=pltpu.SMEM),
                row(D_MODEL), gate_spec, row(CONV_WIDTH),
                row(ATTN_WIDTH), row(ATTN_WIDTH), row(D_MODEL), row(D_MODEL)]
    args = [sink, x2, mod, ya, q, zb, ga, gb]
    if has_local:
        in_specs += [row(kvw), halo(BLOCK, kvw, -1), halo(BLOCK, kvw, +1)] * 2
        args += [k, k, k, v, v, v]
    ctx_spec = pl.BlockSpec((ctx_len, kvw), lambda b, i: (b, 0))
    in_specs += [ctx_spec, ctx_spec, _layer_spec(layer, (CONV_WIDTH, D_MODEL)),
                 _layer_spec(layer, (ATTN_WIDTH, D_MODEL)), _layer_spec(layer, (D_MODEL, D_MODEL))]
    args += [kc, vc, wa, wb, wo]
    if final:
        in_specs.append(whole((1, D_MODEL)))
        args.append(final_w.reshape(1, D_MODEL))

    nk = ctx_len + (3 * BLOCK if has_local else 0)
    k_rows = ctx_len + tq + 2 * BLOCK if has_local else ctx_len
    n_slot = nblk if has_local else 1
    rows_a = SLABS_PER_KV * BLOCK
    scratch = [pltpu.VMEM((tq, D_MODEL), f32),
               pltpu.VMEM((tq, ATTN_WIDTH), bf16),
               pltpu.VMEM((2 * N_KV_HEADS, k_rows, LANES), bf16),
               pltpu.VMEM((n_slot, N_KV_HEADS, HEADS_PER_SLAB * nk, 2 * LANES), bf16),
               pltpu.VMEM((nblk * N_KV_HEADS * HEADS_PER_SLAB, rows_a, nk), f32),
               pltpu.VMEM((nblk * N_KV_HEADS, rows_a, HEADS_PER_SLAB * nk), bf16),
               pltpu.VMEM((nblk * N_KV_HEADS, rows_a, LANES), f32)]
    if has_local:
        scratch += [pltpu.VMEM((4, rows_a, BLOCK), f32)]

    return pl.pallas_call(
        functools.partial(_mix_kernel, layer=layer, tq=tq, has_local=has_local, final=final),
        grid=(batch, n_i),
        in_specs=in_specs,
        out_specs=row(D_MODEL),
        out_shape=jax.ShapeDtypeStruct((n_rows, D_MODEL), f32),
        scratch_shapes=scratch,
        compiler_params=pltpu.CompilerParams(
            dimension_semantics=("arbitrary", "arbitrary"), vmem_limit_bytes=VMEM_LIMIT),
        name="mixer_latent" if has_local else "mixer_context",
    )(*args)


def _rope_tables(seq):
    pos = np.arange(seq)
    inv_freq = jnp.asarray(ROPE_THETA, f32) ** (-jnp.arange(ROPE_FREQS, dtype=f32) / ROPE_FREQS)
    lane = np.arange(LANES)
    use_col = (lane % HEAD_DIM) >= HEAD_DIM // 2
    upper = (lane % (2 * ROPE_FREQS)) >= ROPE_FREQS
    coord = jnp.where(use_col[None, :], (pos % GRID_W)[:, None], (pos // GRID_W)[:, None])
    ang = coord.astype(f32) * inv_freq[lane % ROPE_FREQS][None, :]
    cos, sin = jnp.cos(ang), jnp.sin(ang)
    zero = jnp.zeros_like(sin)
    sin_lo = jnp.where(upper[None, :], zero, -sin)
    sin_hi = jnp.where(upper[None, :], sin, zero)
    return cos, sin_lo, sin_hi


def kernel(x, c, ctx, c_ctx, norm_w, w_mod, b_mod, w_in, conv_w, w_a_out, w_b_out,
           attn_sink, w_o, final_norm_w):
    batch, seq, _ = x.shape
    ctx_len = ctx.shape[1]
    depth = w_in.shape[0]
    assert batch + 1 <= MOD_ROWS and seq % 512 == 0 and ctx_len % BLOCK == 0

    c_all = jnp.zeros((MOD_ROWS, D_MODEL), f32).at[:batch].set(c).at[batch].set(c_ctx)
    mod = _modulation(c_all, w_mod, b_mod).reshape(depth, MOD_ROWS, 3, 1, D_MODEL)
    rope_tabs = _rope_tables(seq)

    w_in_b = w_in.astype(bf16)
    wa_b, wb_b, wo_b = w_a_out.astype(bf16), w_b_out.astype(bf16), w_o.astype(bf16)

    x2 = x.reshape(batch * seq, D_MODEL)
    c2 = ctx.reshape(batch * ctx_len, D_MODEL)
    for l in range(depth):
        last = l == depth - 1
        pc = _projection(c2, l, norm_w, mod, w_in_b, conv_w, None,
                         tm=ctx_len, seq=ctx_len, mod_row0=batch, mod_per_seq=False)
        px = _projection(x2, l, norm_w, mod, w_in_b, conv_w, rope_tabs,
                         tm=512, seq=seq, mod_row0=0, mod_per_seq=True)
        kc, vc = pc[2], pc[3]
        x2 = _mixer(x2, l, mod, attn_sink, px, kc, vc, wa_b, wb_b, wo_b,
                    final_norm_w if last else None,
                    seq=seq, tq=512, has_local=True, gate_row0=0, gate_per_batch=True)
        if not last:
            c2 = _mixer(c2, l, mod, attn_sink, pc, kc, vc, wa_b, wb_b, wo_b, None,
                        seq=ctx_len, tq=ctx_len, has_local=False, gate_row0=batch,
                        gate_per_batch=False)
    return x2.reshape(batch, seq, D_MODEL)
```

```python
import functools
import math

import numpy as np
import jax
import jax.numpy as jnp
from jax import lax
from jax.experimental import pallas as pl
from jax.experimental.pallas import tpu as pltpu

D_MODEL = 1024
GRID_W = 64
CONV_WIDTH = 512
CONV_K = 3
N_HEADS = 8
N_KV_HEADS = 2
HEAD_DIM = 64
ATTN_WIDTH = N_HEADS * HEAD_DIM
KV_WIDTH = N_KV_HEADS * HEAD_DIM
BLOCK = 128
ROPE_THETA = 10000.0
ROPE_FREQS = HEAD_DIM // 4
EPS = 1e-6
NEG = -1e30
LOG2E = math.log2(math.e)

LANES = 128
SUBLANES = 8
MOD_ROWS = 24
VMEM_LIMIT = 56 * 1024 * 1024
N_SLAB = ATTN_WIDTH // LANES
HEADS_PER_SLAB = LANES // HEAD_DIM
SLABS_PER_KV = N_SLAB // N_KV_HEADS

_OFF_B = 0
_OFF_C = _OFF_B + CONV_WIDTH
_OFF_U = _OFF_C + CONV_WIDTH
_OFF_ZA = _OFF_U + CONV_WIDTH
_OFF_Q = _OFF_ZA + CONV_WIDTH
_OFF_K = _OFF_Q + ATTN_WIDTH
_OFF_V = _OFF_K + KV_WIDTH
_OFF_ZB = _OFF_V + KV_WIDTH
_OFF_GA = _OFF_ZB + ATTN_WIDTH
_OFF_GB = _OFF_GA + D_MODEL
IN_COLS = _OFF_GB + D_MODEL

f32 = jnp.float32
bf16 = jnp.bfloat16


def _sigmoid(x):
    return 0.5 * jnp.tanh(0.5 * x) + 0.5


def _silu(x):
    return x * _sigmoid(x)


def _mod_kernel(c_ref, w_ref, b_ref, o_ref):
    s = _silu(c_ref[...])
    o_ref[0] = jnp.dot(s, w_ref[0], preferred_element_type=f32,
                       precision=lax.Precision.HIGHEST) + b_ref[0]


def _modulation(c_all, w_mod, b_mod):
    depth = w_mod.shape[0]
    n_col = 3 * D_MODEL // D_MODEL
    return pl.pallas_call(
        _mod_kernel,
        grid=(depth, n_col),
        in_specs=[
            pl.BlockSpec((MOD_ROWS, D_MODEL), lambda l, j: (0, 0)),
            pl.BlockSpec((1, D_MODEL, D_MODEL), lambda l, j: (l, 0, j)),
            pl.BlockSpec((1, 1, D_MODEL), lambda l, j: (l, 0, j)),
        ],
        out_specs=pl.BlockSpec((1, MOD_ROWS, D_MODEL), lambda l, j: (l, 0, j)),
        out_shape=jax.ShapeDtypeStruct((depth, MOD_ROWS, 3 * D_MODEL), f32),
        name="modulation",
    )(c_all, w_mod, b_mod.reshape(depth, 1, 3 * D_MODEL))


def _rope(p, cos, sin_lo, sin_hi):
    outs = []
    for j in range(p.shape[1] // LANES):
        x = p[:, j * LANES:(j + 1) * LANES]
        outs.append(x * cos
                    + pltpu.roll(x, LANES - ROPE_FREQS, 1) * sin_lo
                    + pltpu.roll(x, ROPE_FREQS, 1) * sin_hi)
    return outs[0] if len(outs) == 1 else jnp.concatenate(outs, axis=1)


def _with_swapped_halves(a):
    return jnp.concatenate([a, pltpu.roll(a, HEAD_DIM, 1)], axis=1)


def _norm_mod(x, nw, scale, shift):
    ms = jnp.mean(x * x, axis=-1, keepdims=True)
    y = x * lax.rsqrt(ms + EPS) * nw
    return (y * (1.0 + scale) + shift).astype(bf16)


def _proj_kernel(*refs, rope, tiles_per_seq):
    it = iter(refs)
    x1_ref, xp1_ref, xn1_ref, shift1_ref, scale1_ref = (next(it) for _ in range(5))
    x0_ref, xp0_ref, xn0_ref, shift0_ref, scale0_ref = (next(it) for _ in range(5))
    nw_ref, w_ref, cw_ref = (next(it) for _ in range(3))
    if rope:
        cos_ref, slo_ref, shi_ref = (next(it) for _ in range(3))
    ya_ref, q_ref, k_ref, v_ref, zb_ref, ga_ref, gb_ref = (next(it) for _ in range(7))
    h_s = (next(it), next(it))
    cu_s = next(it)
    tm = x1_ref.shape[0]
    halo = 2 * SUBLANES
    n_part = 4
    part_rows = tm // n_part
    t = pl.program_id(0)
    slot = lax.rem(t, 2)
    nw = nw_ref[...]

    def fill_rows(dst, part, x_ref, scale_ref, shift_ref, after=0.0):
        rows = slice(part * part_rows, (part + 1) * part_rows)
        dst[rows, :] = _norm_mod(x_ref[rows, :] + after, nw, scale_ref[...], shift_ref[...])

    def fill_halo(dst, xp_ref, xn_ref, scale_ref, shift_ref, after=0.0):
        dst[tm:tm + halo, :] = _norm_mod(
            jnp.concatenate([xp_ref[...], xn_ref[...]], axis=0) + after, nw, scale_ref[...],
            shift_ref[...])

    @pl.when(t == 0)
    def _():
        for part in range(n_part):
            fill_rows(h_s[0], part, x0_ref, scale0_ref, shift0_ref)
        fill_halo(h_s[0], xp0_ref, xn0_ref, scale0_ref, shift0_ref)

    def body(cur, nxt):
        next_fills = [functools.partial(fill_rows, nxt, part, x1_ref, scale1_ref, shift1_ref)
                      for part in range(n_part)]
        next_fills.append(functools.partial(fill_halo, nxt, xp1_ref, xn1_ref, scale1_ref,
                                            shift1_ref))

        def proj(off, width, rows=tm, fill=False):
            out = jnp.dot(cur[0:rows, :], w_ref[:, off:off + width],
                          preferred_element_type=f32)
            if fill and next_fills:
                next_fills.pop(0)(after=jnp.clip(out[0:1, 0:1], -1.0, 1.0) * 0.0)
            return out

        cu = proj(_OFF_C, CONV_WIDTH, tm + halo) * proj(_OFF_U, CONV_WIDTH, tm + halo)
        t_in_seq = t % tiles_per_seq
        zero = jnp.zeros((SUBLANES, CONV_WIDTH), f32)
        cu_s[0:SUBLANES, :] = jnp.where(t_in_seq > 0, cu[tm:tm + SUBLANES], zero)
        cu_s[SUBLANES:SUBLANES + tm, :] = cu[0:tm]
        cu_s[SUBLANES + tm:, :] = jnp.where(t_in_seq < tiles_per_seq - 1,
                                            cu[tm + SUBLANES:], zero)
        cw = cw_ref[...]
        conv = (cu_s[SUBLANES - 1:SUBLANES - 1 + tm, :] * cw[0:1]
                + cu_s[SUBLANES:SUBLANES + tm, :] * cw[1:2]
                + cu_s[SUBLANES + 1:SUBLANES + 1 + tm, :] * cw[2:3])
        ya_ref[...] = (proj(_OFF_B, CONV_WIDTH) * conv
                       * _silu(proj(_OFF_ZA, CONV_WIDTH))).astype(bf16)

        q = proj(_OFF_Q, ATTN_WIDTH)
        kv = proj(_OFF_K, 2 * KV_WIDTH)
        k = kv[:, :KV_WIDTH]
        if rope:
            cos, slo, shi = cos_ref[...], slo_ref[...], shi_ref[...]
            q = _rope(q, cos, slo, shi)
            k = _rope(k, cos, slo, shi)
        q_ref[...] = (q * (HEAD_DIM ** -0.5 * LOG2E)).astype(bf16)
        k_ref[...] = _with_swapped_halves(k).astype(bf16)
        v_ref[...] = _with_swapped_halves(kv[:, KV_WIDTH:]).astype(bf16)
        zb_ref[...] = _silu(proj(_OFF_ZB, ATTN_WIDTH)).astype(bf16)
        part = 2 * LANES
        for n in range(D_MODEL // part):
            ga_ref[:, n * part:(n + 1) * part] = _sigmoid(
                proj(_OFF_GA + n * part, part, fill=True)).astype(bf16)
        for n in range(D_MODEL // part):
            gb_ref[:, n * part:(n + 1) * part] = _sigmoid(
                proj(_OFF_GB + n * part, part, fill=True)).astype(bf16)
        assert not next_fills

    for parity in range(2):
        pl.when(slot == parity)(functools.partial(body, h_s[parity], h_s[1 - parity]))


def _mod_spec(layer, row_of, which):
    return pl.BlockSpec((None, None, None, 1, D_MODEL),
                        lambda *g: (layer, row_of(*g), which, 0, 0))


def _layer_spec(layer, shape):
    return pl.BlockSpec((None,) + tuple(shape), lambda *g: (layer,) + (0,) * len(shape))


def _projection(x2, layer, norm_w, mod, w_in, conv_w, rope_tabs, *, tm, seq, mod_row0,
                mod_per_seq):
    n_rows = x2.shape[0]
    n_tiles = n_rows // tm
    tiles_per_seq = seq // tm
    halo_per_tile = tm // SUBLANES
    n_halo = n_rows // SUBLANES
    if mod_per_seq:
        mod_row = lambda t: mod_row0 + t // tiles_per_seq
    else:
        mod_row = lambda t: mod_row0
    row = lambda w: pl.BlockSpec((tm, w), lambda t: (t, 0))

    def source_specs(tile_of):
        return [
            pl.BlockSpec((tm, D_MODEL), lambda t: (tile_of(t), 0)),
            pl.BlockSpec((SUBLANES, D_MODEL),
                         lambda t: (jnp.maximum(tile_of(t) * halo_per_tile - 1, 0), 0)),
            pl.BlockSpec((SUBLANES, D_MODEL),
                         lambda t: (jnp.minimum((tile_of(t) + 1) * halo_per_tile, n_halo - 1), 0)),
            _mod_spec(layer, lambda t: mod_row(tile_of(t)), 0),
            _mod_spec(layer, lambda t: mod_row(tile_of(t)), 1),
        ]

    in_specs = (source_specs(lambda t: jnp.minimum(t + 1, n_tiles - 1))
                + source_specs(lambda t: 0)
                + [_layer_spec(layer, (1, D_MODEL)),
                   _layer_spec(layer, (D_MODEL, IN_COLS)),
                   _layer_spec(layer, (CONV_K, CONV_WIDTH))])
    args = [x2, x2, x2, mod, mod] * 2 + [norm_w.reshape(-1, 1, D_MODEL), w_in, conv_w]
    if rope_tabs is not None:
        in_specs += [pl.BlockSpec((tm, LANES), lambda t: (t % tiles_per_seq, 0))] * 3
        args += list(rope_tabs)
    widths = (CONV_WIDTH, ATTN_WIDTH, 2 * KV_WIDTH, 2 * KV_WIDTH, ATTN_WIDTH, D_MODEL, D_MODEL)
    return pl.pallas_call(
        functools.partial(_proj_kernel, rope=rope_tabs is not None,
                          tiles_per_seq=tiles_per_seq),
        grid=(n_tiles,),
        in_specs=in_specs,
        out_specs=[row(w) for w in widths],
        out_shape=[jax.ShapeDtypeStruct((n_rows, w), bf16) for w in widths],
        scratch_shapes=[pltpu.VMEM((tm + 2 * SUBLANES, D_MODEL), bf16),
                        pltpu.VMEM((tm + 2 * SUBLANES, D_MODEL), bf16),
                        pltpu.VMEM((tm + 2 * SUBLANES, CONV_WIDTH), f32)],
        compiler_params=pltpu.CompilerParams(
            dimension_semantics=("arbitrary",), vmem_limit_bytes=VMEM_LIMIT),
        name="projection_rope" if rope_tabs is not None else "projection",
    )(*args)


def _kv_variants(a2):
    a, sw = a2[:, :LANES], a2[:, LANES:]
    low = lax.broadcasted_iota(jnp.int32, a.shape, 1) < HEAD_DIM
    z = jnp.zeros_like(a)
    return [jnp.where(low, a, z), jnp.where(low, z, sw),
            jnp.where(low, sw, z), jnp.where(low, z, a)]


def _mix_kernel(*refs, layer, tq, has_local, final):
    it = iter(refs)
    sink_ref = next(it)
    x_ref, gate_ref, ya_ref, q_ref, zb_ref, ga_ref, gb_ref = (next(it) for _ in range(7))
    if has_local:
        k_ref, kp_ref, kn_ref, v_ref, vp_ref, vn_ref = (next(it) for _ in range(6))
    kc_ref, vc_ref = next(it), next(it)
    wa_ref, wb_ref, wo_ref = (next(it) for _ in range(3))
    if final:
        fw_ref = next(it)
    o_ref = next(it)
    y_s, yb_s, k_s, vm_s, s_s, p_s, z_s = (next(it) for _ in range(7))
    if has_local:
        bias_s = next(it)

    i = pl.program_id(1)
    n_i = pl.num_programs(1)
    nblk = tq // BLOCK
    ctx_len = kc_ref.shape[0]
    n_loc = 3 * BLOCK if has_local else 0
    nk = ctx_len + n_loc
    n_slot = vm_s.shape[0]
    rows_a = SLABS_PER_KV * BLOCK

    @pl.when(i == 0)
    def _():
        kvar = _kv_variants(kc_ref[...])
        vvar = _kv_variants(vc_ref[...])
        low = lax.broadcasted_iota(jnp.int32, (nk, LANES), 1) < HEAD_DIM
        ones_on = [jnp.where(low, 1.0, 0.0).astype(bf16), jnp.where(low, 0.0, 1.0).astype(bf16)]
        for n in range(4):
            k_s[n, 0:ctx_len, :] = kvar[n]
        for slot in range(n_slot):
            for kh in range(N_KV_HEADS):
                for e in range(HEADS_PER_SLAB):
                    vm_s[slot, kh, e * nk:e * nk + ctx_len, 0:LANES] = vvar[2 * kh + e]
                    vm_s[slot, kh, e * nk:(e + 1) * nk, LANES:] = ones_on[e]

    def local_block(ref_prev, ref_tile, ref_next, t):
        if t == 0:
            return ref_prev[...]
        if t == nblk + 1:
            return ref_next[...]
        return ref_tile[(t - 1) * BLOCK:t * BLOCK, :]

    def fill_values(j, b):
        vvar = _kv_variants(local_block(vp_ref, v_ref, vn_ref, j + b))
        for kh in range(N_KV_HEADS):
            for e in range(HEADS_PER_SLAB):
                r = e * nk + ctx_len + b * BLOCK
                vm_s[j, kh, r:r + BLOCK, 0:LANES] = vvar[2 * kh + e]

    def fill_keys(t):
        for n, a in enumerate(_kv_variants(local_block(kp_ref, k_ref, kn_ref, t))):
            k_s[n, ctx_len + t * BLOCK:ctx_len + (t + 1) * BLOCK, :] = a

    def fill_bias():
        qi = lax.broadcasted_iota(jnp.int32, (rows_a, BLOCK), 0) % BLOCK
        ci = lax.broadcasted_iota(jnp.int32, (rows_a, BLOCK), 1)
        tri_prev = jnp.where(ci >= qi, 0.0, NEG)
        tri_next = jnp.where(ci <= qi, 0.0, NEG)
        bias_s[0] = tri_prev + jnp.where(i == 0, NEG, 0.0)
        bias_s[1] = tri_prev
        bias_s[2] = tri_next
        bias_s[3] = tri_next + jnp.where(i == n_i - 1, NEG, 0.0)

    def conv_branch(n, width):
        cols = slice(n * width, (n + 1) * width)
        y_s[:, cols] = (jnp.dot(ya_ref[...], wa_ref[:, cols], preferred_element_type=f32)
                        * ga_ref[:, cols].astype(f32))

    fills = ([functools.partial(fill_keys, t) for t in range(nblk + 2)] + [fill_bias]
             if has_local else [])
    n_part = 4
    for n in range(n_part):
        conv_branch(n, D_MODEL // n_part)
        for fill in fills[n::n_part]:
            fill()

    nt = (((1,), (1,)), ((), ()))

    def stage_a(j, kh, e):
        u = (j * N_KV_HEADS + kh) * HEADS_PER_SLAB + e
        rows = slice(j * BLOCK, (j + 1) * BLOCK)
        q2 = jnp.concatenate(
            [q_ref[rows, (kh * SLABS_PER_KV + s) * LANES:(kh * SLABS_PER_KV + s + 1) * LANES]
             for s in range(SLABS_PER_KV)], axis=0)
        var = 2 * kh + e
        s_s[u, :, 0:ctx_len] = lax.dot_general(
            q2, k_s[var, 0:ctx_len, :], nt, preferred_element_type=f32)
        if has_local:
            r0 = ctx_len + j * BLOCK
            s_loc = lax.dot_general(q2, k_s[var, r0:r0 + n_loc, :], nt,
                                    preferred_element_type=f32)
            c0 = ctx_len
            s_s[u, :, c0:c0 + BLOCK] = s_loc[:, :BLOCK] + bias_s[0 if j == 0 else 1]
            s_s[u, :, c0 + BLOCK:c0 + 2 * BLOCK] = s_loc[:, BLOCK:2 * BLOCK]
            s_s[u, :, c0 + 2 * BLOCK:] = s_loc[:, 2 * BLOCK:] + bias_s[3 if j == nblk - 1 else 2]

    def stage_b(j, kh, e):
        u = (j * N_KV_HEADS + kh) * HEADS_PER_SLAB + e
        jk = j * N_KV_HEADS + kh
        for s in range(SLABS_PER_KV):
            rs = slice(s * BLOCK, (s + 1) * BLOCK)
            sk = sink_ref[layer, (kh * SLABS_PER_KV + s) * HEADS_PER_SLAB + e] * LOG2E
            mx = s_s[u, rs, 0:LANES]
            for n in range(1, nk // LANES):
                mx = jnp.maximum(mx, s_s[u, rs, n * LANES:(n + 1) * LANES])
            m = jnp.maximum(jnp.max(mx, axis=1, keepdims=True), sk)
            for n in range(nk // LANES):
                col = e * nk + n * LANES
                p_s[jk, rs, col:col + LANES] = jnp.exp2(
                    s_s[u, rs, n * LANES:(n + 1) * LANES] - m).astype(bf16)
            z_s[jk, rs, e * HEAD_DIM:(e + 1) * HEAD_DIM] = jnp.broadcast_to(
                jnp.exp2(sk - m), (BLOCK, HEAD_DIM))

    def stage_c(j, kh):
        jk = j * N_KV_HEADS + kh
        res = jnp.dot(p_s[jk], vm_s[j if has_local else 0, kh], preferred_element_type=f32)
        attn = res[:, :LANES] / (res[:, LANES:] + z_s[jk])
        rows = slice(j * BLOCK, (j + 1) * BLOCK)
        for s in range(SLABS_PER_KV):
            c = kh * SLABS_PER_KV + s
            lanes = slice(c * LANES, (c + 1) * LANES)
            yb_s[rows, lanes] = (attn[s * BLOCK:(s + 1) * BLOCK]
                                 * zb_ref[rows, lanes].astype(f32)).astype(bf16)

    for t in range(nblk + 2):
        for kh in range(N_KV_HEADS):
            if has_local and 0 <= t - 1 < nblk:
                for b in range(3)[kh::N_KV_HEADS]:
                    fill_values(t - 1, b)
            for e in range(HEADS_PER_SLAB):
                if t < nblk:
                    stage_a(t, kh, e)
                if 0 <= t - 1 < nblk:
                    stage_b(t - 1, kh, e)
            if 0 <= t - 2 < nblk:
                stage_c(t - 2, kh)

    y = y_s[...] + (jnp.dot(yb_s[...], wb_ref[...], preferred_element_type=f32)
                    * gb_ref[...].astype(f32))
    y16 = y.astype(bf16)
    if final:
        xn = x_ref[...] + gate_ref[...] * jnp.dot(y16, wo_ref[...], preferred_element_type=f32)
        ms = jnp.mean(xn * xn, axis=-1, keepdims=True)
        o_ref[...] = xn * lax.rsqrt(ms + EPS) * fw_ref[...]
    else:
        width = D_MODEL // n_part
        for n in range(n_part):
            cols = slice(n * width, (n + 1) * width)
            o_ref[:, cols] = x_ref[:, cols] + gate_ref[:, cols] * jnp.dot(
                y16, wo_ref[:, cols], preferred_element_type=f32)


def _mixer(x2, layer, mod, sink, proj_out, kc, vc, wa, wb, wo, final_w, *,
           seq, tq, has_local, gate_row0, gate_per_batch):
    ya, q, k, v, zb, ga, gb = proj_out
    n_rows = x2.shape[0]
    batch = n_rows // seq
    ctx_len = kc.shape[0] // batch
    n_i = seq // tq
    nblk = tq // BLOCK
    final = final_w is not None

    def row(w):
        return pl.BlockSpec((tq, w), lambda b, i: (b * n_i + i, 0))

    def halo(rows_blk, w, side):
        per_tile = tq // rows_blk
        per_seq = seq // rows_blk
        if side < 0:
            return pl.BlockSpec((rows_blk, w), lambda b, i: (
                b * per_seq + jnp.maximum(i * per_tile - 1, 0), 0))
        return pl.BlockSpec((rows_blk, w), lambda b, i: (
            b * per_seq + jnp.minimum((i + 1) * per_tile, per_seq - 1), 0))

    def whole(shape):
        return pl.BlockSpec(shape, lambda b, i: (0,) * len(shape))

    if gate_per_batch:
        gate_spec = _mod_spec(layer, lambda b, i: gate_row0 + b, 2)
    else:
        gate_spec = _mod_spec(layer, lambda b, i: gate_row0, 2)

    kvw = 2 * KV_WIDTH
    in_specs = [pl.BlockSpec(memory_space=pltpu.SMEM),
                row(D_MODEL), gate_spec, row(CONV_WIDTH),
                row(ATTN_WIDTH), row(ATTN_WIDTH), row(D_MODEL), row(D_MODEL)]
    args = [sink, x2, mod, ya, q, zb, ga, gb]
    if has_local:
        in_specs += [row(kvw), halo(BLOCK, kvw, -1), halo(BLOCK, kvw, +1)] * 2
        args += [k, k, k, v, v, v]
    ctx_spec = pl.BlockSpec((ctx_len, kvw), lambda b, i: (b, 0))
    in_specs += [ctx_spec, ctx_spec, _layer_spec(layer, (CONV_WIDTH, D_MODEL)),
                 _layer_spec(layer, (ATTN_WIDTH, D_MODEL)), _layer_spec(layer, (D_MODEL, D_MODEL))]
    args += [kc, vc, wa, wb, wo]
    if final:
        in_specs.append(whole((1, D_MODEL)))
        args.append(final_w.reshape(1, D_MODEL))

    nk = ctx_len + (3 * BLOCK if has_local else 0)
    k_rows = ctx_len + tq + 2 * BLOCK if has_local else ctx_len
    n_slot = nblk if has_local else 1
    rows_a = SLABS_PER_KV * BLOCK
    scratch = [pltpu.VMEM((tq, D_MODEL), f32),
               pltpu.VMEM((tq, ATTN_WIDTH), bf16),
               pltpu.VMEM((2 * N_KV_HEADS, k_rows, LANES), bf16),
               pltpu.VMEM((n_slot, N_KV_HEADS, HEADS_PER_SLAB * nk, 2 * LANES), bf16),
               pltpu.VMEM((nblk * N_KV_HEADS * HEADS_PER_SLAB, rows_a, nk), f32),
               pltpu.VMEM((nblk * N_KV_HEADS, rows_a, HEADS_PER_SLAB * nk), bf16),
               pltpu.VMEM((nblk * N_KV_HEADS, rows_a, LANES), f32)]
    if has_local:
        scratch += [pltpu.VMEM((4, rows_a, BLOCK), f32)]

    return pl.pallas_call(
        functools.partial(_mix_kernel, layer=layer, tq=tq, has_local=has_local, final=final),
        grid=(batch, n_i),
        in_specs=in_specs,
        out_specs=row(D_MODEL),
        out_shape=jax.ShapeDtypeStruct((n_rows, D_MODEL), f32),
        scratch_shapes=scratch,
        compiler_params=pltpu.CompilerParams(
            dimension_semantics=("arbitrary", "arbitrary"), vmem_limit_bytes=VMEM_LIMIT),
        name="mixer_latent" if has_local else "mixer_context",
    )(*args)


def _rope_tables(seq):
    pos = np.arange(seq)
    inv_freq = jnp.asarray(ROPE_THETA, f32) ** (-jnp.arange(ROPE_FREQS, dtype=f32) / ROPE_FREQS)
    lane = np.arange(LANES)
    use_col = (lane % HEAD_DIM) >= HEAD_DIM // 2
    upper = (lane % (2 * ROPE_FREQS)) >= ROPE_FREQS
    coord = jnp.where(use_col[None, :], (pos % GRID_W)[:, None], (pos // GRID_W)[:, None])
    ang = coord.astype(f32) * inv_freq[lane % ROPE_FREQS][None, :]
    cos, sin = jnp.cos(ang), jnp.sin(ang)
    zero = jnp.zeros_like(sin)
    sin_lo = jnp.where(upper[None, :], zero, -sin)
    sin_hi = jnp.where(upper[None, :], sin, zero)
    return cos, sin_lo, sin_hi


def kernel(x, c, ctx, c_ctx, norm_w, w_mod, b_mod, w_in, conv_w, w_a_out, w_b_out,
           attn_sink, w_o, final_norm_w):
    batch, seq, _ = x.shape
    ctx_len = ctx.shape[1]
    depth = w_in.shape[0]
    assert batch + 1 <= MOD_ROWS and seq % 512 == 0 and ctx_len % BLOCK == 0

    c_all = jnp.zeros((MOD_ROWS, D_MODEL), f32).at[:batch].set(c).at[batch].set(c_ctx)
    mod = _modulation(c_all, w_mod, b_mod).reshape(depth, MOD_ROWS, 3, 1, D_MODEL)
    rope_tabs = _rope_tables(seq)

    w_in_b = w_in.astype(bf16)
    wa_b, wb_b, wo_b = w_a_out.astype(bf16), w_b_out.astype(bf16), w_o.astype(bf16)

    x2 = x.reshape(batch * seq, D_MODEL)
    c2 = ctx.reshape(batch * ctx_len, D_MODEL)
    for l in range(depth):
        last = l == depth - 1
        pc = _projection(c2, l, norm_w, mod, w_in_b, conv_w, None,
                         tm=ctx_len, seq=ctx_len, mod_row0=batch, mod_per_seq=False)
        px = _projection(x2, l, norm_w, mod, w_in_b, conv_w, rope_tabs,
                         tm=512, seq=seq, mod_row0=0, mod_per_seq=True)
        kc, vc = pc[2], pc[3]
        x2 = _mixer(x2, l, mod, attn_sink, px, kc, vc, wa_b, wb_b, wo_b,
                    final_norm_w if last else None,
                    seq=seq, tq=512, has_local=True, gate_row0=0, gate_per_batch=True)
        if not last:
            c2 = _mixer(c2, l, mod, attn_sink, pc, kc, vc, wa_b, wb_b, wo_b, None,
                        seq=ctx_len, tq=ctx_len, has_local=False, gate_row0=batch,
                        gate_per_batch=False)
    return x2.reshape(batch, seq, D_MODEL)
```

```python
import functools
import math

import numpy as np
import jax
import jax.numpy as jnp
from jax import lax
from jax.experimental import pallas as pl
from jax.experimental.pallas import tpu as pltpu

D_MODEL = 1024
GRID_W = 64
CONV_WIDTH = 512
CONV_K = 3
N_HEADS = 8
N_KV_HEADS = 2
HEAD_DIM = 64
ATTN_WIDTH = N_HEADS * HEAD_DIM
KV_WIDTH = N_KV_HEADS * HEAD_DIM
BLOCK = 128
ROPE_THETA = 10000.0
ROPE_FREQS = HEAD_DIM // 4
EPS = 1e-6
NEG = -1e30
LOG2E = math.log2(math.e)

LANES = 128
SUBLANES = 8
MOD_ROWS = 24
VMEM_LIMIT = 56 * 1024 * 1024
N_SLAB = ATTN_WIDTH // LANES
HEADS_PER_SLAB = LANES // HEAD_DIM
SLABS_PER_KV = N_SLAB // N_KV_HEADS

_OFF_B = 0
_OFF_C = _OFF_B + CONV_WIDTH
_OFF_U = _OFF_C + CONV_WIDTH
_OFF_ZA = _OFF_U + CONV_WIDTH
_OFF_Q = _OFF_ZA + CONV_WIDTH
_OFF_K = _OFF_Q + ATTN_WIDTH
_OFF_V = _OFF_K + KV_WIDTH
_OFF_ZB = _OFF_V + KV_WIDTH
_OFF_GA = _OFF_ZB + ATTN_WIDTH
_OFF_GB = _OFF_GA + D_MODEL
IN_COLS = _OFF_GB + D_MODEL

f32 = jnp.float32
bf16 = jnp.bfloat16


def _sigmoid(x):
    return 0.5 * jnp.tanh(0.5 * x) + 0.5


def _silu(x):
    return x * _sigmoid(x)


def _mod_kernel(c_ref, w_ref, b_ref, o_ref):
    s = _silu(c_ref[...])
    o_ref[0] = jnp.dot(s, w_ref[0], preferred_element_type=f32,
                       precision=lax.Precision.HIGHEST) + b_ref[0]


def _modulation(c_all, w_mod, b_mod):
    depth = w_mod.shape[0]
    n_col = 3 * D_MODEL // D_MODEL
    return pl.pallas_call(
        _mod_kernel,
        grid=(depth, n_col),
        in_specs=[
            pl.BlockSpec((MOD_ROWS, D_MODEL), lambda l, j: (0, 0)),
            pl.BlockSpec((1, D_MODEL, D_MODEL), lambda l, j: (l, 0, j)),
            pl.BlockSpec((1, 1, D_MODEL), lambda l, j: (l, 0, j)),
        ],
        out_specs=pl.BlockSpec((1, MOD_ROWS, D_MODEL), lambda l, j: (l, 0, j)),
        out_shape=jax.ShapeDtypeStruct((depth, MOD_ROWS, 3 * D_MODEL), f32),
        name="modulation",
    )(c_all, w_mod, b_mod.reshape(depth, 1, 3 * D_MODEL))


def _rope(p, cos, sin_lo, sin_hi):
    outs = []
    for j in range(p.shape[1] // LANES):
        x = p[:, j * LANES:(j + 1) * LANES]
        outs.append(x * cos
                    + pltpu.roll(x, LANES - ROPE_FREQS, 1) * sin_lo
                    + pltpu.roll(x, ROPE_FREQS, 1) * sin_hi)
    return outs[0] if len(outs) == 1 else jnp.concatenate(outs, axis=1)


def _with_swapped_halves(a):
    return jnp.concatenate([a, pltpu.roll(a, HEAD_DIM, 1)], axis=1)


def _norm_mod(x, nw, scale, shift):
    ms = jnp.mean(x * x, axis=-1, keepdims=True)
    y = x * lax.rsqrt(ms + EPS) * nw
    return (y * (1.0 + scale) + shift).astype(bf16)


def _proj_kernel(*refs, rope, tiles_per_seq):
    it = iter(refs)
    x_ref, xp_ref, xn_ref, nw_ref, shift_ref, scale_ref, w_ref, cw_ref = (
        next(it) for _ in range(8))
    if rope:
        cos_ref, slo_ref, shi_ref = (next(it) for _ in range(3))
    ya_ref, q_ref, k_ref, v_ref, zb_ref, ga_ref, gb_ref = (next(it) for _ in range(7))
    h_s, cu_s = next(it), next(it)
    tm = x_ref.shape[0]
    halo = 2 * SUBLANES

    nw, scale, shift = nw_ref[...], scale_ref[...], shift_ref[...]
    h_s[0:tm, :] = _norm_mod(x_ref[...], nw, scale, shift)
    h_s[tm:tm + halo, :] = _norm_mod(
        jnp.concatenate([xp_ref[...], xn_ref[...]], axis=0), nw, scale, shift)

    def proj(off, width, rows=tm):
        return jnp.dot(h_s[0:rows, :], w_ref[:, off:off + width], preferred_element_type=f32)

    cu = proj(_OFF_C, CONV_WIDTH, tm + halo) * proj(_OFF_U, CONV_WIDTH, tm + halo)
    t_in_seq = pl.program_id(0) % tiles_per_seq
    zero = jnp.zeros((SUBLANES, CONV_WIDTH), f32)
    cu_s[0:SUBLANES, :] = jnp.where(t_in_seq > 0, cu[tm:tm + SUBLANES], zero)
    cu_s[SUBLANES:SUBLANES + tm, :] = cu[0:tm]
    cu_s[SUBLANES + tm:, :] = jnp.where(t_in_seq < tiles_per_seq - 1, cu[tm + SUBLANES:], zero)
    cw = cw_ref[...]
    conv = (cu_s[SUBLANES - 1:SUBLANES - 1 + tm, :] * cw[0:1]
            + cu_s[SUBLANES:SUBLANES + tm, :] * cw[1:2]
            + cu_s[SUBLANES + 1:SUBLANES + 1 + tm, :] * cw[2:3])
    ya_ref[...] = (proj(_OFF_B, CONV_WIDTH) * conv * _silu(proj(_OFF_ZA, CONV_WIDTH))).astype(bf16)

    q = proj(_OFF_Q, ATTN_WIDTH)
    kv = proj(_OFF_K, 2 * KV_WIDTH)
    k = kv[:, :KV_WIDTH]
    if rope:
        cos, slo, shi = cos_ref[...], slo_ref[...], shi_ref[...]
        q = _rope(q, cos, slo, shi)
        k = _rope(k, cos, slo, shi)
    q_ref[...] = (q * (HEAD_DIM ** -0.5 * LOG2E)).astype(bf16)
    k_ref[...] = _with_swapped_halves(k).astype(bf16)
    v_ref[...] = _with_swapped_halves(kv[:, KV_WIDTH:]).astype(bf16)
    zb_ref[...] = _silu(proj(_OFF_ZB, ATTN_WIDTH)).astype(bf16)
    part = 2 * LANES
    for n in range(D_MODEL // part):
        ga_ref[:, n * part:(n + 1) * part] = _sigmoid(proj(_OFF_GA + n * part, part)).astype(bf16)
    for n in range(D_MODEL // part):
        gb_ref[:, n * part:(n + 1) * part] = _sigmoid(proj(_OFF_GB + n * part, part)).astype(bf16)


def _mod_spec(layer, row_of, which):
    return pl.BlockSpec((None, None, None, 1, D_MODEL),
                        lambda *g: (layer, row_of(*g), which, 0, 0))


def _layer_spec(layer, shape):
    return pl.BlockSpec((None,) + tuple(shape), lambda *g: (layer,) + (0,) * len(shape))


def _projection(x2, layer, norm_w, mod, w_in, conv_w, rope_tabs, *, tm, seq, mod_row0,
                mod_per_seq):
    n_rows = x2.shape[0]
    tiles_per_seq = seq // tm
    halo_per_tile = tm // SUBLANES
    n_halo = n_rows // SUBLANES
    if mod_per_seq:
        mod_row = lambda t: mod_row0 + t // tiles_per_seq
    else:
        mod_row = lambda t: mod_row0
    row = lambda w: pl.BlockSpec((tm, w), lambda t: (t, 0))
    in_specs = [
        row(D_MODEL),
        pl.BlockSpec((SUBLANES, D_MODEL),
                     lambda t: (jnp.maximum(t * halo_per_tile - 1, 0), 0)),
        pl.BlockSpec((SUBLANES, D_MODEL),
                     lambda t: (jnp.minimum((t + 1) * halo_per_tile, n_halo - 1), 0)),
        _layer_spec(layer, (1, D_MODEL)),
        _mod_spec(layer, mod_row, 0),
        _mod_spec(layer, mod_row, 1),
        _layer_spec(layer, (D_MODEL, IN_COLS)),
        _layer_spec(layer, (CONV_K, CONV_WIDTH)),
    ]
    args = [x2, x2, x2, norm_w.reshape(-1, 1, D_MODEL), mod, mod, w_in, conv_w]
    if rope_tabs is not None:
        in_specs += [pl.BlockSpec((tm, LANES), lambda t: (t % tiles_per_seq, 0))] * 3
        args += list(rope_tabs)
    widths = (CONV_WIDTH, ATTN_WIDTH, 2 * KV_WIDTH, 2 * KV_WIDTH, ATTN_WIDTH, D_MODEL, D_MODEL)
    return pl.pallas_call(
        functools.partial(_proj_kernel, rope=rope_tabs is not None,
                          tiles_per_seq=tiles_per_seq),
        grid=(n_rows // tm,),
        in_specs=in_specs,
        out_specs=[row(w) for w in widths],
        out_shape=[jax.ShapeDtypeStruct((n_rows, w), bf16) for w in widths],
        scratch_shapes=[pltpu.VMEM((tm + 2 * SUBLANES, D_MODEL), bf16),
                        pltpu.VMEM((tm + 2 * SUBLANES, CONV_WIDTH), f32)],
        compiler_params=pltpu.CompilerParams(
            dimension_semantics=("arbitrary",), vmem_limit_bytes=VMEM_LIMIT),
        name="projection_rope" if rope_tabs is not None else "projection",
    )(*args)


def _kv_variants(a2):
    a, sw = a2[:, :LANES], a2[:, LANES:]
    low = lax.broadcasted_iota(jnp.int32, a.shape, 1) < HEAD_DIM
    z = jnp.zeros_like(a)
    return [jnp.where(low, a, z), jnp.where(low, z, sw),
            jnp.where(low, sw, z), jnp.where(low, z, a)]


def _mix_kernel(*refs, layer, tq, has_local, final):
    it = iter(refs)
    sink_ref = next(it)
    x_ref, gate_ref, ya_ref, q_ref, zb_ref, ga_ref, gb_ref = (next(it) for _ in range(7))
    if has_local:
        k_ref, kp_ref, kn_ref, v_ref, vp_ref, vn_ref = (next(it) for _ in range(6))
    kc_ref, vc_ref = next(it), next(it)
    wa_ref, wb_ref, wo_ref = (next(it) for _ in range(3))
    if final:
        fw_ref = next(it)
    o_ref = next(it)
    y_s, yb_s, k_s, vm_s, s_s, p_s, z_s = (next(it) for _ in range(7))
    if has_local:
        bias_s = next(it)

    i = pl.program_id(1)
    n_i = pl.num_programs(1)
    nblk = tq // BLOCK
    ctx_len = kc_ref.shape[0]
    n_loc = 3 * BLOCK if has_local else 0
    nk = ctx_len + n_loc
    n_slot = vm_s.shape[0]
    rows_a = SLABS_PER_KV * BLOCK

    @pl.when(i == 0)
    def _():
        kvar = _kv_variants(kc_ref[...])
        vvar = _kv_variants(vc_ref[...])
        low = lax.broadcasted_iota(jnp.int32, (nk, LANES), 1) < HEAD_DIM
        ones_on = [jnp.where(low, 1.0, 0.0).astype(bf16), jnp.where(low, 0.0, 1.0).astype(bf16)]
        for n in range(4):
            k_s[n, 0:ctx_len, :] = kvar[n]
        for slot in range(n_slot):
            for kh in range(N_KV_HEADS):
                for e in range(HEADS_PER_SLAB):
                    vm_s[slot, kh, e * nk:e * nk + ctx_len, 0:LANES] = vvar[2 * kh + e]
                    vm_s[slot, kh, e * nk:(e + 1) * nk, LANES:] = ones_on[e]

    def local_block(ref_prev, ref_tile, ref_next, t):
        if t == 0:
            return ref_prev[...]
        if t == nblk + 1:
            return ref_next[...]
        return ref_tile[(t - 1) * BLOCK:t * BLOCK, :]

    def fill_values(j, b):
        vvar = _kv_variants(local_block(vp_ref, v_ref, vn_ref, j + b))
        for kh in range(N_KV_HEADS):
            for e in range(HEADS_PER_SLAB):
                r = e * nk + ctx_len + b * BLOCK
                vm_s[j, kh, r:r + BLOCK, 0:LANES] = vvar[2 * kh + e]

    def fill_keys(t):
        for n, a in enumerate(_kv_variants(local_block(kp_ref, k_ref, kn_ref, t))):
            k_s[n, ctx_len + t * BLOCK:ctx_len + (t + 1) * BLOCK, :] = a

    def fill_bias():
        qi = lax.broadcasted_iota(jnp.int32, (rows_a, BLOCK), 0) % BLOCK
        ci = lax.broadcasted_iota(jnp.int32, (rows_a, BLOCK), 1)
        tri_prev = jnp.where(ci >= qi, 0.0, NEG)
        tri_next = jnp.where(ci <= qi, 0.0, NEG)
        bias_s[0] = tri_prev + jnp.where(i == 0, NEG, 0.0)
        bias_s[1] = tri_prev
        bias_s[2] = tri_next
        bias_s[3] = tri_next + jnp.where(i == n_i - 1, NEG, 0.0)

    if has_local:
        for t in range(nblk + 2):
            fill_keys(t)
        fill_bias()

    nt = (((1,), (1,)), ((), ()))

    def stage_a(j, kh, e):
        u = (j * N_KV_HEADS + kh) * HEADS_PER_SLAB + e
        rows = slice(j * BLOCK, (j + 1) * BLOCK)
        q2 = jnp.concatenate(
            [q_ref[rows, (kh * SLABS_PER_KV + s) * LANES:(kh * SLABS_PER_KV + s + 1) * LANES]
             for s in range(SLABS_PER_KV)], axis=0)
        var = 2 * kh + e
        s_s[u, :, 0:ctx_len] = lax.dot_general(
            q2, k_s[var, 0:ctx_len, :], nt, preferred_element_type=f32)
        if has_local:
            r0 = ctx_len + j * BLOCK
            s_loc = lax.dot_general(q2, k_s[var, r0:r0 + n_loc, :], nt,
                                    preferred_element_type=f32)
            c0 = ctx_len
            s_s[u, :, c0:c0 + BLOCK] = s_loc[:, :BLOCK] + bias_s[0 if j == 0 else 1]
            s_s[u, :, c0 + BLOCK:c0 + 2 * BLOCK] = s_loc[:, BLOCK:2 * BLOCK]
            s_s[u, :, c0 + 2 * BLOCK:] = s_loc[:, 2 * BLOCK:] + bias_s[3 if j == nblk - 1 else 2]

    low_half = lax.broadcasted_iota(jnp.int32, (BLOCK, LANES), 1) < HEAD_DIM

    def stage_b(j, kh):
        jk = j * N_KV_HEADS + kh
        for s in range(SLABS_PER_KV):
            rs = slice(s * BLOCK, (s + 1) * BLOCK)
            z = []
            for e in range(HEADS_PER_SLAB):
                u = jk * HEADS_PER_SLAB + e
                sk = sink_ref[layer, (kh * SLABS_PER_KV + s) * HEADS_PER_SLAB + e] * LOG2E
                mx = s_s[u, rs, 0:LANES]
                for n in range(1, nk // LANES):
                    mx = jnp.maximum(mx, s_s[u, rs, n * LANES:(n + 1) * LANES])
                m = jnp.maximum(jnp.max(mx, axis=1, keepdims=True), sk)
                for n in range(nk // LANES):
                    col = e * nk + n * LANES
                    p_s[jk, rs, col:col + LANES] = jnp.exp2(
                        s_s[u, rs, n * LANES:(n + 1) * LANES] - m).astype(bf16)
                z.append(jnp.exp2(sk - m))
            z_s[jk, rs, :] = jnp.where(low_half, z[0], z[1])

    def stage_c(j, kh):
        jk = j * N_KV_HEADS + kh
        res = jnp.dot(p_s[jk], vm_s[j if has_local else 0, kh], preferred_element_type=f32)
        attn = res[:, :LANES] / (res[:, LANES:] + z_s[jk])
        rows = slice(j * BLOCK, (j + 1) * BLOCK)
        for s in range(SLABS_PER_KV):
            c = kh * SLABS_PER_KV + s
            lanes = slice(c * LANES, (c + 1) * LANES)
            yb_s[rows, lanes] = (attn[s * BLOCK:(s + 1) * BLOCK]
                                 * zb_ref[rows, lanes].astype(f32)).astype(bf16)

    for t in range(nblk + 2):
        for kh in range(N_KV_HEADS):
            if has_local and 0 <= t - 1 < nblk:
                for b in range(3)[kh::N_KV_HEADS]:
                    fill_values(t - 1, b)
            if t < nblk:
                for e in range(HEADS_PER_SLAB):
                    stage_a(t, kh, e)
            if 0 <= t - 1 < nblk:
                stage_b(t - 1, kh)
            if 0 <= t - 2 < nblk:
                stage_c(t - 2, kh)

    n_part = 4
    width = D_MODEL // n_part
    for n in range(n_part):
        cols = slice(n * width, (n + 1) * width)
        y_s[:, cols] = (
            jnp.dot(ya_ref[...], wa_ref[:, cols], preferred_element_type=f32)
            * ga_ref[:, cols].astype(f32)
            + jnp.dot(yb_s[...], wb_ref[:, cols], preferred_element_type=f32)
            * gb_ref[:, cols].astype(f32)).astype(bf16)
    y16 = y_s[...]
    if final:
        xn = x_ref[...] + gate_ref[...] * jnp.dot(y16, wo_ref[...], preferred_element_type=f32)
        ms = jnp.mean(xn * xn, axis=-1, keepdims=True)
        o_ref[...] = xn * lax.rsqrt(ms + EPS) * fw_ref[...]
    else:
        for n in range(n_part):
            cols = slice(n * width, (n + 1) * width)
            o_ref[:, cols] = x_ref[:, cols] + gate_ref[:, cols] * jnp.dot(
                y16, wo_ref[:, cols], preferred_element_type=f32)


def _mixer(x2, layer, mod, sink, proj_out, kc, vc, wa, wb, wo, final_w, *,
           seq, tq, has_local, gate_row0, gate_per_batch):
    ya, q, k, v, zb, ga, gb = proj_out
    n_rows = x2.shape[0]
    batch = n_rows // seq
    ctx_len = kc.shape[0] // batch
    n_i = seq // tq
    nblk = tq // BLOCK
    final = final_w is not None

    def row(w):
        return pl.BlockSpec((tq, w), lambda b, i: (b * n_i + i, 0))

    def halo(rows_blk, w, side):
        per_tile = tq // rows_blk
        per_seq = seq // rows_blk
        if side < 0:
            return pl.BlockSpec((rows_blk, w), lambda b, i: (
                b * per_seq + jnp.maximum(i * per_tile - 1, 0), 0))
        return pl.BlockSpec((rows_blk, w), lambda b, i: (
            b * per_seq + jnp.minimum((i + 1) * per_tile, per_seq - 1), 0))

    def whole(shape):
        return pl.BlockSpec(shape, lambda b, i: (0,) * len(shape))

    if gate_per_batch:
        gate_spec = _mod_spec(layer, lambda b, i: gate_row0 + b, 2)
    else:
        gate_spec = _mod_spec(layer, lambda b, i: gate_row0, 2)

    kvw = 2 * KV_WIDTH
    in_specs = [pl.BlockSpec(memory_space=pltpu.SMEM),
                row(D_MODEL), gate_spec, row(CONV_WIDTH),
                row(ATTN_WIDTH), row(ATTN_WIDTH), row(D_MODEL), row(D_MODEL)]
    args = [sink, x2, mod, ya, q, zb, ga, gb]
    if has_local:
        in_specs += [row(kvw), halo(BLOCK, kvw, -1), halo(BLOCK, kvw, +1)] * 2
        args += [k, k, k, v, v, v]
    ctx_spec = pl.BlockSpec((ctx_len, kvw), lambda b, i: (b, 0))
    in_specs += [ctx_spec, ctx_spec, _layer_spec(layer, (CONV_WIDTH, D_MODEL)),
                 _layer_spec(layer, (ATTN_WIDTH, D_MODEL)), _layer_spec(layer, (D_MODEL, D_MODEL))]
    args += [kc, vc, wa, wb, wo]
    if final:
        in_specs.append(whole((1, D_MODEL)))
        args.append(final_w.reshape(1, D_MODEL))

    nk = ctx_len + (3 * BLOCK if has_local else 0)
    k_rows = ctx_len + tq + 2 * BLOCK if has_local else ctx_len
    n_slot = nblk if has_local else 1
    rows_a = SLABS_PER_KV * BLOCK
    scratch = [pltpu.VMEM((tq, D_MODEL), bf16),
               pltpu.VMEM((tq, ATTN_WIDTH), bf16),
               pltpu.VMEM((2 * N_KV_HEADS, k_rows, LANES), bf16),
               pltpu.VMEM((n_slot, N_KV_HEADS, HEADS_PER_SLAB * nk, 2 * LANES), bf16),
               pltpu.VMEM((nblk * N_KV_HEADS * HEADS_PER_SLAB, rows_a, nk), f32),
               pltpu.VMEM((nblk * N_KV_HEADS, rows_a, HEADS_PER_SLAB * nk), bf16),
               pltpu.VMEM((nblk * N_KV_HEADS, rows_a, LANES), f32)]
    if has_local:
        scratch += [pltpu.VMEM((4, rows_a, BLOCK), f32)]

    return pl.pallas_call(
        functools.partial(_mix_kernel, layer=layer, tq=tq, has_local=has_local, final=final),
        grid=(batch, n_i),
        in_specs=in_specs,
        out_specs=row(D_MODEL),
        out_shape=jax.ShapeDtypeStruct((n_rows, D_MODEL), f32),
        scratch_shapes=scratch,
        compiler_params=pltpu.CompilerParams(
            dimension_semantics=("arbitrary", "arbitrary"), vmem_limit_bytes=VMEM_LIMIT),
        name="mixer_latent" if has_local else "mixer_context",
    )(*args)


def _rope_tables(seq):
    pos = np.arange(seq)
    inv_freq = jnp.asarray(ROPE_THETA, f32) ** (-jnp.arange(ROPE_FREQS, dtype=f32) / ROPE_FREQS)
    lane = np.arange(LANES)
    use_col = (lane % HEAD_DIM) >= HEAD_DIM // 2
    upper = (lane % (2 * ROPE_FREQS)) >= ROPE_FREQS
    coord = jnp.where(use_col[None, :], (pos % GRID_W)[:, None], (pos // GRID_W)[:, None])
    ang = coord.astype(f32) * inv_freq[lane % ROPE_FREQS][None, :]
    cos, sin = jnp.cos(ang), jnp.sin(ang)
    zero = jnp.zeros_like(sin)
    sin_lo = jnp.where(upper[None, :], zero, -sin)
    sin_hi = jnp.where(upper[None, :], sin, zero)
    return cos, sin_lo, sin_hi


def kernel(x, c, ctx, c_ctx, norm_w, w_mod, b_mod, w_in, conv_w, w_a_out, w_b_out,
           attn_sink, w_o, final_norm_w):
    batch, seq, _ = x.shape
    ctx_len = ctx.shape[1]
    depth = w_in.shape[0]
    assert batch + 1 <= MOD_ROWS and seq % 512 == 0 and ctx_len % BLOCK == 0

    c_all = jnp.zeros((MOD_ROWS, D_MODEL), f32).at[:batch].set(c).at[batch].set(c_ctx)
    mod = _modulation(c_all, w_mod, b_mod).reshape(depth, MOD_ROWS, 3, 1, D_MODEL)
    rope_tabs = _rope_tables(seq)

    w_in_b = w_in.astype(bf16)
    wa_b, wb_b, wo_b = w_a_out.astype(bf16), w_b_out.astype(bf16), w_o.astype(bf16)

    x2 = x.reshape(batch * seq, D_MODEL)
    c2 = ctx.reshape(batch * ctx_len, D_MODEL)
    for l in range(depth):
        last = l == depth - 1
        pc = _projection(c2, l, norm_w, mod, w_in_b, conv_w, None,
                         tm=ctx_len, seq=ctx_len, mod_row0=batch, mod_per_seq=False)
        px = _projection(x2, l, norm_w, mod, w_in_b, conv_w, rope_tabs,
                         tm=512, seq=seq, mod_row0=0, mod_per_seq=True)
        kc, vc = pc[2], pc[3]
        x2 = _mixer(x2, l, mod, attn_sink, px, kc, vc, wa_b, wb_b, wo_b,
                    final_norm_w if last else None,
                    seq=seq, tq=512, has_local=True, gate_row0=0, gate_per_batch=True)
        if not last:
            c2 = _mixer(c2, l, mod, attn_sink, pc, kc, vc, wa_b, wb_b, wo_b, None,
                        seq=ctx_len, tq=ctx_len, has_local=False, gate_row0=batch,
                        gate_per_batch=False)
    return x2.reshape(batch, seq, D_MODEL)
```

```python
import functools
import math

import numpy as np
import jax
import jax.numpy as jnp
from jax import lax
from jax.experimental import pallas as pl
from jax.experimental.pallas import tpu as pltpu

D_MODEL = 1024
GRID_W = 64
CONV_WIDTH = 512
CONV_K = 3
N_HEADS = 8
N_KV_HEADS = 2
HEAD_DIM = 64
ATTN_WIDTH = N_HEADS * HEAD_DIM
KV_WIDTH = N_KV_HEADS * HEAD_DIM
BLOCK = 128
ROPE_THETA = 10000.0
ROPE_FREQS = HEAD_DIM // 4
EPS = 1e-6
NEG = -1e30
LOG2E = math.log2(math.e)

LANES = 128
SUBLANES = 8
MOD_ROWS = 24
VMEM_LIMIT = 56 * 1024 * 1024
N_SLAB = ATTN_WIDTH // LANES
HEADS_PER_SLAB = LANES // HEAD_DIM
SLABS_PER_KV = N_SLAB // N_KV_HEADS

_OFF_B = 0
_OFF_C = _OFF_B + CONV_WIDTH
_OFF_U = _OFF_C + CONV_WIDTH
_OFF_ZA = _OFF_U + CONV_WIDTH
_OFF_Q = _OFF_ZA + CONV_WIDTH
_OFF_K = _OFF_Q + ATTN_WIDTH
_OFF_V = _OFF_K + KV_WIDTH
_OFF_ZB = _OFF_V + KV_WIDTH
_OFF_GA = _OFF_ZB + ATTN_WIDTH
_OFF_GB = _OFF_GA + D_MODEL
IN_COLS = _OFF_GB + D_MODEL

f32 = jnp.float32
bf16 = jnp.bfloat16


def _sigmoid(x):
    return 0.5 * jnp.tanh(0.5 * x) + 0.5


def _silu(x):
    return x * _sigmoid(x)


def _mod_kernel(c_ref, w_ref, b_ref, o_ref):
    s = _silu(c_ref[...])
    o_ref[0] = jnp.dot(s, w_ref[0], preferred_element_type=f32,
                       precision=lax.Precision.HIGHEST) + b_ref[0]


def _modulation(c_all, w_mod, b_mod):
    depth = w_mod.shape[0]
    n_col = 3 * D_MODEL // D_MODEL
    return pl.pallas_call(
        _mod_kernel,
        grid=(depth, n_col),
        in_specs=[
            pl.BlockSpec((MOD_ROWS, D_MODEL), lambda l, j: (0, 0)),
            pl.BlockSpec((1, D_MODEL, D_MODEL), lambda l, j: (l, 0, j)),
            pl.BlockSpec((1, 1, D_MODEL), lambda l, j: (l, 0, j)),
        ],
        out_specs=pl.BlockSpec((1, MOD_ROWS, D_MODEL), lambda l, j: (l, 0, j)),
        out_shape=jax.ShapeDtypeStruct((depth, MOD_ROWS, 3 * D_MODEL), f32),
        name="modulation",
    )(c_all, w_mod, b_mod.reshape(depth, 1, 3 * D_MODEL))


def _rope(p, cos, sin_lo, sin_hi):
    outs = []
    for j in range(p.shape[1] // LANES):
        x = p[:, j * LANES:(j + 1) * LANES]
        outs.append(x * cos
                    + pltpu.roll(x, LANES - ROPE_FREQS, 1) * sin_lo
                    + pltpu.roll(x, ROPE_FREQS, 1) * sin_hi)
    return outs[0] if len(outs) == 1 else jnp.concatenate(outs, axis=1)


def _with_swapped_halves(a):
    return jnp.concatenate([a, pltpu.roll(a, HEAD_DIM, 1)], axis=1)


def _norm_mod(x, nw, scale, shift):
    ms = jnp.mean(x * x, axis=-1, keepdims=True)
    y = x * lax.rsqrt(ms + EPS) * nw
    return (y * (1.0 + scale) + shift).astype(bf16)


def _proj_kernel(*refs, rope, tiles_per_seq, sub):
    it = iter(refs)
    x_ref, xp_ref, xn_ref, nw_ref, shift_ref, scale_ref, w_ref, cw_ref = (
        next(it) for _ in range(8))
    if rope:
        cos_ref, slo_ref, shi_ref = (next(it) for _ in range(3))
    ya_ref, q_ref, k_ref, v_ref, zb_ref, ga_ref, gb_ref = (next(it) for _ in range(7))
    h_s, cu_s = next(it), next(it)
    tm = x_ref.shape[0]
    n_sub = tm // sub
    halo = 2 * SUBLANES
    nw, scale, shift = nw_ref[...], scale_ref[...], shift_ref[...]
    t_in_seq = pl.program_id(0) % tiles_per_seq
    cw = cw_ref[...]
    zero = jnp.zeros((SUBLANES, CONV_WIDTH), f32)

    for n in range(n_sub):
        r0 = n * sub
        rows = slice(r0, r0 + sub)
        h, cu_b = h_s.at[n], cu_s.at[n]
        before = xp_ref[...] if n == 0 else x_ref[r0 - SUBLANES:r0, :]
        after = xn_ref[...] if n == n_sub - 1 else x_ref[r0 + sub:r0 + sub + SUBLANES, :]
        h[0:sub, :] = _norm_mod(x_ref[rows, :], nw, scale, shift)
        h[sub:sub + halo, :] = _norm_mod(jnp.concatenate([before, after], axis=0),
                                         nw, scale, shift)

        def proj(off, width, n_rows=sub):
            return jnp.dot(h[0:n_rows, :], w_ref[:, off:off + width],
                           preferred_element_type=f32)

        cu = proj(_OFF_C, CONV_WIDTH, sub + halo) * proj(_OFF_U, CONV_WIDTH, sub + halo)
        cu_before, cu_after = cu[sub:sub + SUBLANES], cu[sub + SUBLANES:]
        if n == 0:
            cu_before = jnp.where(t_in_seq > 0, cu_before, zero)
        if n == n_sub - 1:
            cu_after = jnp.where(t_in_seq < tiles_per_seq - 1, cu_after, zero)
        cu_b[0:SUBLANES, :] = cu_before
        cu_b[SUBLANES:SUBLANES + sub, :] = cu[0:sub]
        cu_b[SUBLANES + sub:, :] = cu_after
        conv = (cu_b[SUBLANES - 1:SUBLANES - 1 + sub, :] * cw[0:1]
                + cu_b[SUBLANES:SUBLANES + sub, :] * cw[1:2]
                + cu_b[SUBLANES + 1:SUBLANES + 1 + sub, :] * cw[2:3])
        ya_ref[rows, :] = (proj(_OFF_B, CONV_WIDTH) * conv
                           * _silu(proj(_OFF_ZA, CONV_WIDTH))).astype(bf16)

        q = proj(_OFF_Q, ATTN_WIDTH)
        kv = proj(_OFF_K, 2 * KV_WIDTH)
        k = kv[:, :KV_WIDTH]
        if rope:
            cos, slo, shi = cos_ref[rows, :], slo_ref[rows, :], shi_ref[rows, :]
            q = _rope(q, cos, slo, shi)
            k = _rope(k, cos, slo, shi)
        q_ref[rows, :] = (q * (HEAD_DIM ** -0.5 * LOG2E)).astype(bf16)
        k_ref[rows, :] = _with_swapped_halves(k).astype(bf16)
        v_ref[rows, :] = _with_swapped_halves(kv[:, KV_WIDTH:]).astype(bf16)
        zb_ref[rows, :] = _silu(proj(_OFF_ZB, ATTN_WIDTH)).astype(bf16)
        part = 2 * LANES
        for c in range(D_MODEL // part):
            ga_ref[rows, c * part:(c + 1) * part] = _sigmoid(
                proj(_OFF_GA + c * part, part)).astype(bf16)
        for c in range(D_MODEL // part):
            gb_ref[rows, c * part:(c + 1) * part] = _sigmoid(
                proj(_OFF_GB + c * part, part)).astype(bf16)


def _mod_spec(layer, row_of, which):
    return pl.BlockSpec((None, None, None, 1, D_MODEL),
                        lambda *g: (layer, row_of(*g), which, 0, 0))


def _layer_spec(layer, shape):
    return pl.BlockSpec((None,) + tuple(shape), lambda *g: (layer,) + (0,) * len(shape))


def _projection(x2, layer, norm_w, mod, w_in, conv_w, rope_tabs, *, tm, sub, seq, mod_row0,
                mod_per_seq):
    n_rows = x2.shape[0]
    tiles_per_seq = seq // tm
    halo_per_tile = tm // SUBLANES
    n_halo = n_rows // SUBLANES
    if mod_per_seq:
        mod_row = lambda t: mod_row0 + t // tiles_per_seq
    else:
        mod_row = lambda t: mod_row0
    row = lambda w: pl.BlockSpec((tm, w), lambda t: (t, 0))
    in_specs = [
        row(D_MODEL),
        pl.BlockSpec((SUBLANES, D_MODEL),
                     lambda t: (jnp.maximum(t * halo_per_tile - 1, 0), 0)),
        pl.BlockSpec((SUBLANES, D_MODEL),
                     lambda t: (jnp.minimum((t + 1) * halo_per_tile, n_halo - 1), 0)),
        _layer_spec(layer, (1, D_MODEL)),
        _mod_spec(layer, mod_row, 0),
        _mod_spec(layer, mod_row, 1),
        _layer_spec(layer, (D_MODEL, IN_COLS)),
        _layer_spec(layer, (CONV_K, CONV_WIDTH)),
    ]
    args = [x2, x2, x2, norm_w.reshape(-1, 1, D_MODEL), mod, mod, w_in, conv_w]
    if rope_tabs is not None:
        in_specs += [pl.BlockSpec((tm, LANES), lambda t: (t % tiles_per_seq, 0))] * 3
        args += list(rope_tabs)
    widths = (CONV_WIDTH, ATTN_WIDTH, 2 * KV_WIDTH, 2 * KV_WIDTH, ATTN_WIDTH, D_MODEL, D_MODEL)
    return pl.pallas_call(
        functools.partial(_proj_kernel, rope=rope_tabs is not None,
                          tiles_per_seq=tiles_per_seq, sub=sub),
        grid=(n_rows // tm,),
        in_specs=in_specs,
        out_specs=[row(w) for w in widths],
        out_shape=[jax.ShapeDtypeStruct((n_rows, w), bf16) for w in widths],
        scratch_shapes=[pltpu.VMEM((tm // sub, sub + 2 * SUBLANES, D_MODEL), bf16),
                        pltpu.VMEM((tm // sub, sub + 2 * SUBLANES, CONV_WIDTH), f32)],
        compiler_params=pltpu.CompilerParams(
            dimension_semantics=("arbitrary",), vmem_limit_bytes=VMEM_LIMIT),
        name="projection_rope" if rope_tabs is not None else "projection",
    )(*args)


def _kv_variants(a2):
    a, sw = a2[:, :LANES], a2[:, LANES:]
    low = lax.broadcasted_iota(jnp.int32, a.shape, 1) < HEAD_DIM
    z = jnp.zeros_like(a)
    return [jnp.where(low, a, z), jnp.where(low, z, sw),
            jnp.where(low, sw, z), jnp.where(low, z, a)]


def _mix_kernel(*refs, layer, tq, has_local, final):
    it = iter(refs)
    sink_ref = next(it)
    x_ref, gate_ref, ya_ref, q_ref, zb_ref, ga_ref, gb_ref = (next(it) for _ in range(7))
    if has_local:
        k_ref, kp_ref, kn_ref, v_ref, vp_ref, vn_ref = (next(it) for _ in range(6))
    kc_ref, vc_ref = next(it), next(it)
    wa_ref, wb_ref, wo_ref = (next(it) for _ in range(3))
    if final:
        fw_ref = next(it)
    o_ref = next(it)
    y_s, yb_s, k_s, vm_s, s_s, p_s, z_s = (next(it) for _ in range(7))
    if has_local:
        bias_s = next(it)

    i = pl.program_id(1)
    n_i = pl.num_programs(1)
    nblk = tq // BLOCK
    ctx_len = kc_ref.shape[0]
    n_loc = 3 * BLOCK if has_local else 0
    nk = ctx_len + n_loc
    n_slot = vm_s.shape[0]
    rows_a = SLABS_PER_KV * BLOCK

    @pl.when(i == 0)
    def _():
        kvar = _kv_variants(kc_ref[...])
        vvar = _kv_variants(vc_ref[...])
        low = lax.broadcasted_iota(jnp.int32, (nk, LANES), 1) < HEAD_DIM
        ones_on = [jnp.where(low, 1.0, 0.0).astype(bf16), jnp.where(low, 0.0, 1.0).astype(bf16)]
        for n in range(4):
            k_s[n, 0:ctx_len, :] = kvar[n]
        for slot in range(n_slot):
            for kh in range(N_KV_HEADS):
                for e in range(HEADS_PER_SLAB):
                    vm_s[slot, kh, e * nk:e * nk + ctx_len, 0:LANES] = vvar[2 * kh + e]
                    vm_s[slot, kh, e * nk:(e + 1) * nk, LANES:] = ones_on[e]

    def local_block(ref_prev, ref_tile, ref_next, t):
        if t == 0:
            return ref_prev[...]
        if t == nblk + 1:
            return ref_next[...]
        return ref_tile[(t - 1) * BLOCK:t * BLOCK, :]

    def fill_values(j, b):
        vvar = _kv_variants(local_block(vp_ref, v_ref, vn_ref, j + b))
        for kh in range(N_KV_HEADS):
            for e in range(HEADS_PER_SLAB):
                r = e * nk + ctx_len + b * BLOCK
                vm_s[j, kh, r:r + BLOCK, 0:LANES] = vvar[2 * kh + e]

    def fill_keys(t):
        for n, a in enumerate(_kv_variants(local_block(kp_ref, k_ref, kn_ref, t))):
            k_s[n, ctx_len + t * BLOCK:ctx_len + (t + 1) * BLOCK, :] = a

    def fill_bias():
        qi = lax.broadcasted_iota(jnp.int32, (rows_a, BLOCK), 0) % BLOCK
        ci = lax.broadcasted_iota(jnp.int32, (rows_a, BLOCK), 1)
        tri_prev = jnp.where(ci >= qi, 0.0, NEG)
        tri_next = jnp.where(ci <= qi, 0.0, NEG)
        bias_s[0] = tri_prev + jnp.where(i == 0, NEG, 0.0)
        bias_s[1] = tri_prev
        bias_s[2] = tri_next
        bias_s[3] = tri_next + jnp.where(i == n_i - 1, NEG, 0.0)

    if has_local:
        for t in range(nblk + 2):
            fill_keys(t)
        fill_bias()

    nt = (((1,), (1,)), ((), ()))

    def stage_a(j, kh, e):
        u = (j * N_KV_HEADS + kh) * HEADS_PER_SLAB + e
        rows = slice(j * BLOCK, (j + 1) * BLOCK)
        q2 = jnp.concatenate(
            [q_ref[rows, (kh * SLABS_PER_KV + s) * LANES:(kh * SLABS_PER_KV + s + 1) * LANES]
             for s in range(SLABS_PER_KV)], axis=0)
        var = 2 * kh + e
        s_s[u, :, 0:ctx_len] = lax.dot_general(
            q2, k_s[var, 0:ctx_len, :], nt, preferred_element_type=f32)
        if has_local:
            r0 = ctx_len + j * BLOCK
            s_loc = lax.dot_general(q2, k_s[var, r0:r0 + n_loc, :], nt,
                                    preferred_element_type=f32)
            c0 = ctx_len
            s_s[u, :, c0:c0 + BLOCK] = s_loc[:, :BLOCK] + bias_s[0 if j == 0 else 1]
            s_s[u, :, c0 + BLOCK:c0 + 2 * BLOCK] = s_loc[:, BLOCK:2 * BLOCK]
            s_s[u, :, c0 + 2 * BLOCK:] = s_loc[:, 2 * BLOCK:] + bias_s[3 if j == nblk - 1 else 2]

    low_half = lax.broadcasted_iota(jnp.int32, (BLOCK, LANES), 1) < HEAD_DIM

    def stage_b(j, kh):
        jk = j * N_KV_HEADS + kh
        for s in range(SLABS_PER_KV):
            rs = slice(s * BLOCK, (s + 1) * BLOCK)
            z = []
            for e in range(HEADS_PER_SLAB):
                u = jk * HEADS_PER_SLAB + e
                sk = sink_ref[layer, (kh * SLABS_PER_KV + s) * HEADS_PER_SLAB + e] * LOG2E
                mx = s_s[u, rs, 0:LANES]
                for n in range(1, nk // LANES):
                    mx = jnp.maximum(mx, s_s[u, rs, n * LANES:(n + 1) * LANES])
                m = jnp.maximum(jnp.max(mx, axis=1, keepdims=True), sk)
                for n in range(nk // LANES):
                    col = e * nk + n * LANES
                    p_s[jk, rs, col:col + LANES] = jnp.exp2(
                        s_s[u, rs, n * LANES:(n + 1) * LANES] - m).astype(bf16)
                z.append(jnp.exp2(sk - m))
            z_s[jk, rs, :] = jnp.where(low_half, z[0], z[1])

    def stage_c(j, kh):
        jk = j * N_KV_HEADS + kh
        res = jnp.dot(p_s[jk], vm_s[j if has_local else 0, kh], preferred_element_type=f32)
        attn = res[:, :LANES] / (res[:, LANES:] + z_s[jk])
        rows = slice(j * BLOCK, (j + 1) * BLOCK)
        for s in range(SLABS_PER_KV):
            c = kh * SLABS_PER_KV + s
            lanes = slice(c * LANES, (c + 1) * LANES)
            yb_s[rows, lanes] = (attn[s * BLOCK:(s + 1) * BLOCK]
                                 * zb_ref[rows, lanes].astype(f32)).astype(bf16)

    for t in range(nblk + 2):
        for kh in range(N_KV_HEADS):
            if has_local and 0 <= t - 1 < nblk:
                for b in range(3)[kh::N_KV_HEADS]:
                    fill_values(t - 1, b)
            if t < nblk:
                for e in range(HEADS_PER_SLAB):
                    stage_a(t, kh, e)
            if 0 <= t - 1 < nblk:
                stage_b(t - 1, kh)
            if 0 <= t - 2 < nblk:
                stage_c(t - 2, kh)

    n_part = 4
    width = D_MODEL // n_part
    for n in range(n_part):
        cols = slice(n * width, (n + 1) * width)
        y_s[:, cols] = (
            jnp.dot(ya_ref[...], wa_ref[:, cols], preferred_element_type=f32)
            * ga_ref[:, cols].astype(f32)
            + jnp.dot(yb_s[...], wb_ref[:, cols], preferred_element_type=f32)
            * gb_ref[:, cols].astype(f32)).astype(bf16)
    y16 = y_s[...]
    if final:
        xn = x_ref[...] + gate_ref[...] * jnp.dot(y16, wo_ref[...], preferred_element_type=f32)
        ms = jnp.mean(xn * xn, axis=-1, keepdims=True)
        o_ref[...] = xn * lax.rsqrt(ms + EPS) * fw_ref[...]
    else:
        for n in range(n_part):
            cols = slice(n * width, (n + 1) * width)
            o_ref[:, cols] = x_ref[:, cols] + gate_ref[:, cols] * jnp.dot(
                y16, wo_ref[:, cols], preferred_element_type=f32)


def _mixer(x2, layer, mod, sink, proj_out, kc, vc, wa, wb, wo, final_w, *,
           seq, tq, has_local, gate_row0, gate_per_batch):
    ya, q, k, v, zb, ga, gb = proj_out
    n_rows = x2.shape[0]
    batch = n_rows // seq
    ctx_len = kc.shape[0] // batch
    n_i = seq // tq
    nblk = tq // BLOCK
    final = final_w is not None

    def row(w):
        return pl.BlockSpec((tq, w), lambda b, i: (b * n_i + i, 0))

    def halo(rows_blk, w, side):
        per_tile = tq // rows_blk
        per_seq = seq // rows_blk
        if side < 0:
            return pl.BlockSpec((rows_blk, w), lambda b, i: (
                b * per_seq + jnp.maximum(i * per_tile - 1, 0), 0))
        return pl.BlockSpec((rows_blk, w), lambda b, i: (
            b * per_seq + jnp.minimum((i + 1) * per_tile, per_seq - 1), 0))

    def whole(shape):
        return pl.BlockSpec(shape, lambda b, i: (0,) * len(shape))

    if gate_per_batch:
        gate_spec = _mod_spec(layer, lambda b, i: gate_row0 + b, 2)
    else:
        gate_spec = _mod_spec(layer, lambda b, i: gate_row0, 2)

    kvw = 2 * KV_WIDTH
    in_specs = [pl.BlockSpec(memory_space=pltpu.SMEM),
                row(D_MODEL), gate_spec, row(CONV_WIDTH),
                row(ATTN_WIDTH), row(ATTN_WIDTH), row(D_MODEL), row(D_MODEL)]
    args = [sink, x2, mod, ya, q, zb, ga, gb]
    if has_local:
        in_specs += [row(kvw), halo(BLOCK, kvw, -1), halo(BLOCK, kvw, +1)] * 2
        args += [k, k, k, v, v, v]
    ctx_spec = pl.BlockSpec((ctx_len, kvw), lambda b, i: (b, 0))
    in_specs += [ctx_spec, ctx_spec, _layer_spec(layer, (CONV_WIDTH, D_MODEL)),
                 _layer_spec(layer, (ATTN_WIDTH, D_MODEL)), _layer_spec(layer, (D_MODEL, D_MODEL))]
    args += [kc, vc, wa, wb, wo]
    if final:
        in_specs.append(whole((1, D_MODEL)))
        args.append(final_w.reshape(1, D_MODEL))

    nk = ctx_len + (3 * BLOCK if has_local else 0)
    k_rows = ctx_len + tq + 2 * BLOCK if has_local else ctx_len
    n_slot = nblk if has_local else 1
    rows_a = SLABS_PER_KV * BLOCK
    scratch = [pltpu.VMEM((tq, D_MODEL), bf16),
               pltpu.VMEM((tq, ATTN_WIDTH), bf16),
               pltpu.VMEM((2 * N_KV_HEADS, k_rows, LANES), bf16),
               pltpu.VMEM((n_slot, N_KV_HEADS, HEADS_PER_SLAB * nk, 2 * LANES), bf16),
               pltpu.VMEM((nblk * N_KV_HEADS * HEADS_PER_SLAB, rows_a, nk), f32),
               pltpu.VMEM((nblk * N_KV_HEADS, rows_a, HEADS_PER_SLAB * nk), bf16),
               pltpu.VMEM((nblk * N_KV_HEADS, rows_a, LANES), f32)]
    if has_local:
        scratch += [pltpu.VMEM((4, rows_a, BLOCK), f32)]

    return pl.pallas_call(
        functools.partial(_mix_kernel, layer=layer, tq=tq, has_local=has_local, final=final),
        grid=(batch, n_i),
        in_specs=in_specs,
        out_specs=row(D_MODEL),
        out_shape=jax.ShapeDtypeStruct((n_rows, D_MODEL), f32),
        scratch_shapes=scratch,
        compiler_params=pltpu.CompilerParams(
            dimension_semantics=("arbitrary", "arbitrary"), vmem_limit_bytes=VMEM_LIMIT),
        name="mixer_latent" if has_local else "mixer_context",
    )(*args)


def _rope_tables(seq):
    pos = np.arange(seq)
    inv_freq = jnp.asarray(ROPE_THETA, f32) ** (-jnp.arange(ROPE_FREQS, dtype=f32) / ROPE_FREQS)
    lane = np.arange(LANES)
    use_col = (lane % HEAD_DIM) >= HEAD_DIM // 2
    upper = (lane % (2 * ROPE_FREQS)) >= ROPE_FREQS
    coord = jnp.where(use_col[None, :], (pos % GRID_W)[:, None], (pos // GRID_W)[:, None])
    ang = coord.astype(f32) * inv_freq[lane % ROPE_FREQS][None, :]
    cos, sin = jnp.cos(ang), jnp.sin(ang)
    zero = jnp.zeros_like(sin)
    sin_lo = jnp.where(upper[None, :], zero, -sin)
    sin_hi = jnp.where(upper[None, :], sin, zero)
    return cos, sin_lo, sin_hi


def kernel(x, c, ctx, c_ctx, norm_w, w_mod, b_mod, w_in, conv_w, w_a_out, w_b_out,
           attn_sink, w_o, final_norm_w):
    batch, seq, _ = x.shape
    ctx_len = ctx.shape[1]
    depth = w_in.shape[0]
    assert batch + 1 <= MOD_ROWS and seq % 512 == 0 and ctx_len % BLOCK == 0

    c_all = jnp.zeros((MOD_ROWS, D_MODEL), f32).at[:batch].set(c).at[batch].set(c_ctx)
    mod = _modulation(c_all, w_mod, b_mod).reshape(depth, MOD_ROWS, 3, 1, D_MODEL)
    rope_tabs = _rope_tables(seq)

    w_in_b = w_in.astype(bf16)
    wa_b, wb_b, wo_b = w_a_out.astype(bf16), w_b_out.astype(bf16), w_o.astype(bf16)

    x2 = x.reshape(batch * seq, D_MODEL)
    c2 = ctx.reshape(batch * ctx_len, D_MODEL)
    for l in range(depth):
        last = l == depth - 1
        pc = _projection(c2, l, norm_w, mod, w_in_b, conv_w, None,
                         tm=ctx_len, sub=ctx_len, seq=ctx_len, mod_row0=batch, mod_per_seq=False)
        px = _projection(x2, l, norm_w, mod, w_in_b, conv_w, rope_tabs,
                         tm=1024, sub=512, seq=seq, mod_row0=0, mod_per_seq=True)
        kc, vc = pc[2], pc[3]
        x2 = _mixer(x2, l, mod, attn_sink, px, kc, vc, wa_b, wb_b, wo_b,
                    final_norm_w if last else None,
                    seq=seq, tq=512, has_local=True, gate_row0=0, gate_per_batch=True)
        if not last:
            c2 = _mixer(c2, l, mod, attn_sink, pc, kc, vc, wa_b, wb_b, wo_b, None,
                        seq=ctx_len, tq=ctx_len, has_local=False, gate_row0=batch,
                        gate_per_batch=False)
    return x2.reshape(batch, seq, D_MODEL)
```

```python
import functools
import math

import numpy as np
import jax
import jax.numpy as jnp
from jax import lax
from jax.experimental import pallas as pl
from jax.experimental.pallas import tpu as pltpu

D_MODEL = 1024
GRID_W = 64
CONV_WIDTH = 512
CONV_K = 3
N_HEADS = 8
N_KV_HEADS = 2
HEAD_DIM = 64
ATTN_WIDTH = N_HEADS * HEAD_DIM
KV_WIDTH = N_KV_HEADS * HEAD_DIM
BLOCK = 128
ROPE_THETA = 10000.0
ROPE_FREQS = HEAD_DIM // 4
EPS = 1e-6
NEG = -1e30
LOG2E = math.log2(math.e)

LANES = 128
SUBLANES = 8
MOD_ROWS = 24
VMEM_LIMIT = 56 * 1024 * 1024
N_SLAB = ATTN_WIDTH // LANES
HEADS_PER_SLAB = LANES // HEAD_DIM
SLABS_PER_KV = N_SLAB // N_KV_HEADS

_OFF_B = 0
_OFF_C = _OFF_B + CONV_WIDTH
_OFF_U = _OFF_C + CONV_WIDTH
_OFF_ZA = _OFF_U + CONV_WIDTH
_OFF_Q = _OFF_ZA + CONV_WIDTH
_OFF_K = _OFF_Q + ATTN_WIDTH
_OFF_V = _OFF_K + KV_WIDTH
_OFF_ZB = _OFF_V + KV_WIDTH
_OFF_GA = _OFF_ZB + ATTN_WIDTH
_OFF_GB = _OFF_GA + D_MODEL
IN_COLS = _OFF_GB + D_MODEL

f32 = jnp.float32
bf16 = jnp.bfloat16


def _sigmoid(x):
    return 0.5 * jnp.tanh(0.5 * x) + 0.5


def _silu(x):
    return x * _sigmoid(x)


def _mod_kernel(c_ref, w_ref, b_ref, o_ref):
    s = _silu(c_ref[...])
    o_ref[0] = jnp.dot(s, w_ref[0], preferred_element_type=f32,
                       precision=lax.Precision.HIGHEST) + b_ref[0]


def _modulation(c_all, w_mod, b_mod):
    depth = w_mod.shape[0]
    n_col = 3 * D_MODEL // D_MODEL
    return pl.pallas_call(
        _mod_kernel,
        grid=(depth, n_col),
        in_specs=[
            pl.BlockSpec((MOD_ROWS, D_MODEL), lambda l, j: (0, 0)),
            pl.BlockSpec((1, D_MODEL, D_MODEL), lambda l, j: (l, 0, j)),
            pl.BlockSpec((1, 1, D_MODEL), lambda l, j: (l, 0, j)),
        ],
        out_specs=pl.BlockSpec((1, MOD_ROWS, D_MODEL), lambda l, j: (l, 0, j)),
        out_shape=jax.ShapeDtypeStruct((depth, MOD_ROWS, 3 * D_MODEL), f32),
        name="modulation",
    )(c_all, w_mod, b_mod.reshape(depth, 1, 3 * D_MODEL))


def _rope(p, cos, sin_lo, sin_hi):
    outs = []
    for j in range(p.shape[1] // LANES):
        x = p[:, j * LANES:(j + 1) * LANES]
        outs.append(x * cos
                    + pltpu.roll(x, LANES - ROPE_FREQS, 1) * sin_lo
                    + pltpu.roll(x, ROPE_FREQS, 1) * sin_hi)
    return outs[0] if len(outs) == 1 else jnp.concatenate(outs, axis=1)


def _with_swapped_halves(a):
    return jnp.concatenate([a, pltpu.roll(a, HEAD_DIM, 1)], axis=1)


def _norm_mod(x, nw, scale, shift):
    ms = jnp.mean(x * x, axis=-1, keepdims=True)
    y = x * lax.rsqrt(ms + EPS) * nw
    return (y * (1.0 + scale) + shift).astype(bf16)


def _proj_kernel(*refs, rope, seq, sub):
    it = iter(refs)
    x_ref, xp_ref, xn_ref, nw_ref, shift_ref, scale_ref, w_ref, cw_ref = (
        next(it) for _ in range(8))
    if rope:
        cos_ref, slo_ref, shi_ref = (next(it) for _ in range(3))
    ya_ref, q_ref, k_ref, v_ref, zb_ref, ga_ref, gb_ref = (next(it) for _ in range(7))
    h_s, cu_s = next(it), next(it)
    tm = x_ref.shape[0]
    n_sub = tm // sub
    halo = 2 * SUBLANES
    nw, scale, shift = nw_ref[...], scale_ref[...], shift_ref[...]
    assert seq % tm == 0 or tm % seq == 0
    tiles_per_seq = max(seq // tm, 1)
    t_in_seq = pl.program_id(0) % tiles_per_seq
    cw = cw_ref[...]
    zero = jnp.zeros((SUBLANES, CONV_WIDTH), f32)

    for n in range(n_sub):
        r0 = n * sub
        rows = slice(r0, r0 + sub)
        h, cu_b = h_s.at[n], cu_s.at[n]
        before = xp_ref[...] if n == 0 else x_ref[r0 - SUBLANES:r0, :]
        after = xn_ref[...] if n == n_sub - 1 else x_ref[r0 + sub:r0 + sub + SUBLANES, :]
        h[0:sub, :] = _norm_mod(x_ref[rows, :], nw, scale, shift)
        h[sub:sub + halo, :] = _norm_mod(jnp.concatenate([before, after], axis=0),
                                         nw, scale, shift)

        def proj(off, width, n_rows=sub):
            return jnp.dot(h[0:n_rows, :], w_ref[:, off:off + width],
                           preferred_element_type=f32)

        cu = proj(_OFF_C, CONV_WIDTH, sub + halo) * proj(_OFF_U, CONV_WIDTH, sub + halo)
        cu_before, cu_after = cu[sub:sub + SUBLANES], cu[sub + SUBLANES:]
        if r0 % seq == 0:
            cu_before = jnp.where(t_in_seq > 0, cu_before, zero) if r0 == 0 else zero
        if (r0 + sub) % seq == 0 or r0 + sub == tm:
            cu_after = (jnp.where(t_in_seq < tiles_per_seq - 1, cu_after, zero)
                        if r0 + sub == tm else zero)
        cu_b[0:SUBLANES, :] = cu_before
        cu_b[SUBLANES:SUBLANES + sub, :] = cu[0:sub]
        cu_b[SUBLANES + sub:, :] = cu_after
        conv = (cu_b[SUBLANES - 1:SUBLANES - 1 + sub, :] * cw[0:1]
                + cu_b[SUBLANES:SUBLANES + sub, :] * cw[1:2]
                + cu_b[SUBLANES + 1:SUBLANES + 1 + sub, :] * cw[2:3])
        ya_ref[rows, :] = (proj(_OFF_B, CONV_WIDTH) * conv
                           * _silu(proj(_OFF_ZA, CONV_WIDTH))).astype(bf16)

        q = proj(_OFF_Q, ATTN_WIDTH)
        kv = proj(_OFF_K, 2 * KV_WIDTH)
        k = kv[:, :KV_WIDTH]
        if rope:
            cos, slo, shi = cos_ref[rows, :], slo_ref[rows, :], shi_ref[rows, :]
            q = _rope(q, cos, slo, shi)
            k = _rope(k, cos, slo, shi)
        q_ref[rows, :] = (q * (HEAD_DIM ** -0.5 * LOG2E)).astype(bf16)
        k_ref[rows, :] = _with_swapped_halves(k).astype(bf16)
        v_ref[rows, :] = _with_swapped_halves(kv[:, KV_WIDTH:]).astype(bf16)
        zb_ref[rows, :] = _silu(proj(_OFF_ZB, ATTN_WIDTH)).astype(bf16)
        part = 2 * LANES
        for c in range(D_MODEL // part):
            ga_ref[rows, c * part:(c + 1) * part] = _sigmoid(
                proj(_OFF_GA + c * part, part)).astype(bf16)
        for c in range(D_MODEL // part):
            gb_ref[rows, c * part:(c + 1) * part] = _sigmoid(
                proj(_OFF_GB + c * part, part)).astype(bf16)


def _mod_spec(layer, row_of, which):
    return pl.BlockSpec((None, None, None, 1, D_MODEL),
                        lambda *g: (layer, row_of(*g), which, 0, 0))


def _layer_spec(layer, shape):
    return pl.BlockSpec((None,) + tuple(shape), lambda *g: (layer,) + (0,) * len(shape))


def _projection(x2, layer, norm_w, mod, w_in, conv_w, rope_tabs, *, tm, sub, seq, mod_row0,
                mod_per_seq):
    n_rows = x2.shape[0]
    assert seq % tm == 0 or (tm % seq == 0 and not mod_per_seq and rope_tabs is None)
    tiles_per_seq = max(seq // tm, 1)
    halo_per_tile = tm // SUBLANES
    n_halo = n_rows // SUBLANES
    if mod_per_seq:
        mod_row = lambda t: mod_row0 + t // tiles_per_seq
    else:
        mod_row = lambda t: mod_row0
    row = lambda w: pl.BlockSpec((tm, w), lambda t: (t, 0))
    in_specs = [
        row(D_MODEL),
        pl.BlockSpec((SUBLANES, D_MODEL),
                     lambda t: (jnp.maximum(t * halo_per_tile - 1, 0), 0)),
        pl.BlockSpec((SUBLANES, D_MODEL),
                     lambda t: (jnp.minimum((t + 1) * halo_per_tile, n_halo - 1), 0)),
        _layer_spec(layer, (1, D_MODEL)),
        _mod_spec(layer, mod_row, 0),
        _mod_spec(layer, mod_row, 1),
        _layer_spec(layer, (D_MODEL, IN_COLS)),
        _layer_spec(layer, (CONV_K, CONV_WIDTH)),
    ]
    args = [x2, x2, x2, norm_w.reshape(-1, 1, D_MODEL), mod, mod, w_in, conv_w]
    if rope_tabs is not None:
        in_specs += [pl.BlockSpec((tm, LANES), lambda t: (t % tiles_per_seq, 0))] * 3
        args += list(rope_tabs)
    widths = (CONV_WIDTH, ATTN_WIDTH, 2 * KV_WIDTH, 2 * KV_WIDTH, ATTN_WIDTH, D_MODEL, D_MODEL)
    return pl.pallas_call(
        functools.partial(_proj_kernel, rope=rope_tabs is not None,
                          seq=seq, sub=sub),
        grid=(n_rows // tm,),
        in_specs=in_specs,
        out_specs=[row(w) for w in widths],
        out_shape=[jax.ShapeDtypeStruct((n_rows, w), bf16) for w in widths],
        scratch_shapes=[pltpu.VMEM((tm // sub, sub + 2 * SUBLANES, D_MODEL), bf16),
                        pltpu.VMEM((tm // sub, sub + 2 * SUBLANES, CONV_WIDTH), f32)],
        compiler_params=pltpu.CompilerParams(
            dimension_semantics=("arbitrary",), vmem_limit_bytes=VMEM_LIMIT),
        name="projection_rope" if rope_tabs is not None else "projection",
    )(*args)


def _kv_variants(a2):
    a, sw = a2[:, :LANES], a2[:, LANES:]
    low = lax.broadcasted_iota(jnp.int32, a.shape, 1) < HEAD_DIM
    z = jnp.zeros_like(a)
    return [jnp.where(low, a, z), jnp.where(low, z, sw),
            jnp.where(low, sw, z), jnp.where(low, z, a)]


def _mix_kernel(*refs, layer, tq, has_local, final):
    it = iter(refs)
    sink_ref = next(it)
    x_ref, gate_ref, ya_ref, q_ref, zb_ref, ga_ref, gb_ref = (next(it) for _ in range(7))
    if has_local:
        k_ref, kp_ref, kn_ref, v_ref, vp_ref, vn_ref = (next(it) for _ in range(6))
    kc_ref, vc_ref = next(it), next(it)
    wa_ref, wb_ref, wo_ref = (next(it) for _ in range(3))
    if final:
        fw_ref = next(it)
    o_ref = next(it)
    y_s, yb_s, k_s, vm_s, s_s, p_s, z_s = (next(it) for _ in range(7))
    if has_local:
        bias_s = next(it)

    i = pl.program_id(1)
    n_i = pl.num_programs(1)
    nblk = tq // BLOCK
    ctx_len = kc_ref.shape[0]
    n_loc = 3 * BLOCK if has_local else 0
    nk = ctx_len + n_loc
    n_slot = vm_s.shape[0]
    rows_a = SLABS_PER_KV * BLOCK

    @pl.when(i == 0)
    def _():
        kvar = _kv_variants(kc_ref[...])
        vvar = _kv_variants(vc_ref[...])
        low = lax.broadcasted_iota(jnp.int32, (nk, LANES), 1) < HEAD_DIM
        ones_on = [jnp.where(low, 1.0, 0.0).astype(bf16), jnp.where(low, 0.0, 1.0).astype(bf16)]
        for n in range(4):
            k_s[n, 0:ctx_len, :] = kvar[n]
        for slot in range(n_slot):
            for kh in range(N_KV_HEADS):
                for e in range(HEADS_PER_SLAB):
                    vm_s[slot, kh, e * nk:e * nk + ctx_len, 0:LANES] = vvar[2 * kh + e]
                    vm_s[slot, kh, e * nk:(e + 1) * nk, LANES:] = ones_on[e]

    def local_block(ref_prev, ref_tile, ref_next, t):
        if t == 0:
            return ref_prev[...]
        if t == nblk + 1:
            return ref_next[...]
        return ref_tile[(t - 1) * BLOCK:t * BLOCK, :]

    def fill_values(j, b):
        vvar = _kv_variants(local_block(vp_ref, v_ref, vn_ref, j + b))
        for kh in range(N_KV_HEADS):
            for e in range(HEADS_PER_SLAB):
                r = e * nk + ctx_len + b * BLOCK
                vm_s[j, kh, r:r + BLOCK, 0:LANES] = vvar[2 * kh + e]

    def fill_keys(t):
        for n, a in enumerate(_kv_variants(local_block(kp_ref, k_ref, kn_ref, t))):
            k_s[n, ctx_len + t * BLOCK:ctx_len + (t + 1) * BLOCK, :] = a

    def fill_bias():
        qi = lax.broadcasted_iota(jnp.int32, (rows_a, BLOCK), 0) % BLOCK
        ci = lax.broadcasted_iota(jnp.int32, (rows_a, BLOCK), 1)
        tri_prev = jnp.where(ci >= qi, 0.0, NEG)
        tri_next = jnp.where(ci <= qi, 0.0, NEG)
        bias_s[0] = tri_prev + jnp.where(i == 0, NEG, 0.0)
        bias_s[1] = tri_prev
        bias_s[2] = tri_next
        bias_s[3] = tri_next + jnp.where(i == n_i - 1, NEG, 0.0)

    if has_local:
        for t in range(nblk + 2):
            fill_keys(t)
        fill_bias()

    nt = (((1,), (1,)), ((), ()))

    def stage_a(j, kh, e):
        u = (j * N_KV_HEADS + kh) * HEADS_PER_SLAB + e
        rows = slice(j * BLOCK, (j + 1) * BLOCK)
        q2 = jnp.concatenate(
            [q_ref[rows, (kh * SLABS_PER_KV + s) * LANES:(kh * SLABS_PER_KV + s + 1) * LANES]
             for s in range(SLABS_PER_KV)], axis=0)
        var = 2 * kh + e
        s_s[u, :, 0:ctx_len] = lax.dot_general(
            q2, k_s[var, 0:ctx_len, :], nt, preferred_element_type=f32)
        if has_local:
            r0 = ctx_len + j * BLOCK
            s_loc = lax.dot_general(q2, k_s[var, r0:r0 + n_loc, :], nt,
                                    preferred_element_type=f32)
            c0 = ctx_len
            s_s[u, :, c0:c0 + BLOCK] = s_loc[:, :BLOCK] + bias_s[0 if j == 0 else 1]
            s_s[u, :, c0 + BLOCK:c0 + 2 * BLOCK] = s_loc[:, BLOCK:2 * BLOCK]
            s_s[u, :, c0 + 2 * BLOCK:] = s_loc[:, 2 * BLOCK:] + bias_s[3 if j == nblk - 1 else 2]

    low_half = lax.broadcasted_iota(jnp.int32, (BLOCK, LANES), 1) < HEAD_DIM

    def stage_b(j, kh):
        jk = j * N_KV_HEADS + kh
        for s in range(SLABS_PER_KV):
            rs = slice(s * BLOCK, (s + 1) * BLOCK)
            z = []
            for e in range(HEADS_PER_SLAB):
                u = jk * HEADS_PER_SLAB + e
                sk = sink_ref[layer, (kh * SLABS_PER_KV + s) * HEADS_PER_SLAB + e] * LOG2E
                mx = s_s[u, rs, 0:LANES]
                for n in range(1, nk // LANES):
                    mx = jnp.maximum(mx, s_s[u, rs, n * LANES:(n + 1) * LANES])
                m = jnp.maximum(jnp.max(mx, axis=1, keepdims=True), sk)
                for n in range(nk // LANES):
                    col = e * nk + n * LANES
                    p_s[jk, rs, col:col + LANES] = jnp.exp2(
                        s_s[u, rs, n * LANES:(n + 1) * LANES] - m).astype(bf16)
                z.append(jnp.exp2(sk - m))
            z_s[jk, rs, :] = jnp.where(low_half, z[0], z[1])

    def stage_c(j, kh):
        jk = j * N_KV_HEADS + kh
        res = jnp.dot(p_s[jk], vm_s[j if has_local else 0, kh], preferred_element_type=f32)
        attn = res[:, :LANES] / (res[:, LANES:] + z_s[jk])
        rows = slice(j * BLOCK, (j + 1) * BLOCK)
        for s in range(SLABS_PER_KV):
            c = kh * SLABS_PER_KV + s
            lanes = slice(c * LANES, (c + 1) * LANES)
            yb_s[rows, lanes] = (attn[s * BLOCK:(s + 1) * BLOCK]
                                 * zb_ref[rows, lanes].astype(f32)).astype(bf16)

    for t in range(nblk + 2):
        for kh in range(N_KV_HEADS):
            if has_local and 0 <= t - 1 < nblk:
                for b in range(3)[kh::N_KV_HEADS]:
                    fill_values(t - 1, b)
            if t < nblk:
                for e in range(HEADS_PER_SLAB):
                    stage_a(t, kh, e)
            if 0 <= t - 1 < nblk:
                stage_b(t - 1, kh)
            if 0 <= t - 2 < nblk:
                stage_c(t - 2, kh)

    n_part = 4
    width = D_MODEL // n_part
    for n in range(n_part):
        cols = slice(n * width, (n + 1) * width)
        y_s[:, cols] = (
            jnp.dot(ya_ref[...], wa_ref[:, cols], preferred_element_type=f32)
            * ga_ref[:, cols].astype(f32)
            + jnp.dot(yb_s[...], wb_ref[:, cols], preferred_element_type=f32)
            * gb_ref[:, cols].astype(f32)).astype(bf16)
    y16 = y_s[...]
    if final:
        xn = x_ref[...] + gate_ref[...] * jnp.dot(y16, wo_ref[...], preferred_element_type=f32)
        ms = jnp.mean(xn * xn, axis=-1, keepdims=True)
        o_ref[...] = xn * lax.rsqrt(ms + EPS) * fw_ref[...]
    else:
        for n in range(n_part):
            cols = slice(n * width, (n + 1) * width)
            o_ref[:, cols] = x_ref[:, cols] + gate_ref[:, cols] * jnp.dot(
                y16, wo_ref[:, cols], preferred_element_type=f32)


def _mixer(x2, layer, mod, sink, proj_out, kc, vc, wa, wb, wo, final_w, *,
           seq, tq, has_local, gate_row0, gate_per_batch):
    ya, q, k, v, zb, ga, gb = proj_out
    n_rows = x2.shape[0]
    batch = n_rows // seq
    ctx_len = kc.shape[0] // batch
    n_i = seq // tq
    nblk = tq // BLOCK
    final = final_w is not None

    def row(w):
        return pl.BlockSpec((tq, w), lambda b, i: (b * n_i + i, 0))

    def halo(rows_blk, w, side):
        per_tile = tq // rows_blk
        per_seq = seq // rows_blk
        if side < 0:
            return pl.BlockSpec((rows_blk, w), lambda b, i: (
                b * per_seq + jnp.maximum(i * per_tile - 1, 0), 0))
        return pl.BlockSpec((rows_blk, w), lambda b, i: (
            b * per_seq + jnp.minimum((i + 1) * per_tile, per_seq - 1), 0))

    def whole(shape):
        return pl.BlockSpec(shape, lambda b, i: (0,) * len(shape))

    if gate_per_batch:
        gate_spec = _mod_spec(layer, lambda b, i: gate_row0 + b, 2)
    else:
        gate_spec = _mod_spec(layer, lambda b, i: gate_row0, 2)

    kvw = 2 * KV_WIDTH
    in_specs = [pl.BlockSpec(memory_space=pltpu.SMEM),
                row(D_MODEL), gate_spec, row(CONV_WIDTH),
                row(ATTN_WIDTH), row(ATTN_WIDTH), row(D_MODEL), row(D_MODEL)]
    args = [sink, x2, mod, ya, q, zb, ga, gb]
    if has_local:
        in_specs += [row(kvw), halo(BLOCK, kvw, -1), halo(BLOCK, kvw, +1)] * 2
        args += [k, k, k, v, v, v]
    ctx_spec = pl.BlockSpec((ctx_len, kvw), lambda b, i: (b, 0))
    in_specs += [ctx_spec, ctx_spec, _layer_spec(layer, (CONV_WIDTH, D_MODEL)),
                 _layer_spec(layer, (ATTN_WIDTH, D_MODEL)), _layer_spec(layer, (D_MODEL, D_MODEL))]
    args += [kc, vc, wa, wb, wo]
    if final:
        in_specs.append(whole((1, D_MODEL)))
        args.append(final_w.reshape(1, D_MODEL))

    nk = ctx_len + (3 * BLOCK if has_local else 0)
    k_rows = ctx_len + tq + 2 * BLOCK if has_local else ctx_len
    n_slot = nblk if has_local else 1
    rows_a = SLABS_PER_KV * BLOCK
    scratch = [pltpu.VMEM((tq, D_MODEL), bf16),
               pltpu.VMEM((tq, ATTN_WIDTH), bf16),
               pltpu.VMEM((2 * N_KV_HEADS, k_rows, LANES), bf16),
               pltpu.VMEM((n_slot, N_KV_HEADS, HEADS_PER_SLAB * nk, 2 * LANES), bf16),
               pltpu.VMEM((nblk * N_KV_HEADS * HEADS_PER_SLAB, rows_a, nk), f32),
               pltpu.VMEM((nblk * N_KV_HEADS, rows_a, HEADS_PER_SLAB * nk), bf16),
               pltpu.VMEM((nblk * N_KV_HEADS, rows_a, LANES), f32)]
    if has_local:
        scratch += [pltpu.VMEM((4, rows_a, BLOCK), f32)]

    return pl.pallas_call(
        functools.partial(_mix_kernel, layer=layer, tq=tq, has_local=has_local, final=final),
        grid=(batch, n_i),
        in_specs=in_specs,
        out_specs=row(D_MODEL),
        out_shape=jax.ShapeDtypeStruct((n_rows, D_MODEL), f32),
        scratch_shapes=scratch,
        compiler_params=pltpu.CompilerParams(
            dimension_semantics=("arbitrary", "arbitrary"), vmem_limit_bytes=VMEM_LIMIT),
        name="mixer_latent" if has_local else "mixer_context",
    )(*args)


def _rope_tables(seq):
    pos = np.arange(seq)
    inv_freq = jnp.asarray(ROPE_THETA, f32) ** (-jnp.arange(ROPE_FREQS, dtype=f32) / ROPE_FREQS)
    lane = np.arange(LANES)
    use_col = (lane % HEAD_DIM) >= HEAD_DIM // 2
    upper = (lane % (2 * ROPE_FREQS)) >= ROPE_FREQS
    coord = jnp.where(use_col[None, :], (pos % GRID_W)[:, None], (pos // GRID_W)[:, None])
    ang = coord.astype(f32) * inv_freq[lane % ROPE_FREQS][None, :]
    cos, sin = jnp.cos(ang), jnp.sin(ang)
    zero = jnp.zeros_like(sin)
    sin_lo = jnp.where(upper[None, :], zero, -sin)
    sin_hi = jnp.where(upper[None, :], sin, zero)
    return cos, sin_lo, sin_hi


def kernel(x, c, ctx, c_ctx, norm_w, w_mod, b_mod, w_in, conv_w, w_a_out, w_b_out,
           attn_sink, w_o, final_norm_w):
    batch, seq, _ = x.shape
    ctx_len = ctx.shape[1]
    depth = w_in.shape[0]
    assert batch + 1 <= MOD_ROWS and seq % 512 == 0 and ctx_len % BLOCK == 0

    c_all = jnp.zeros((MOD_ROWS, D_MODEL), f32).at[:batch].set(c).at[batch].set(c_ctx)
    mod = _modulation(c_all, w_mod, b_mod).reshape(depth, MOD_ROWS, 3, 1, D_MODEL)
    rope_tabs = _rope_tables(seq)

    w_in_b = w_in.astype(bf16)
    wa_b, wb_b, wo_b = w_a_out.astype(bf16), w_b_out.astype(bf16), w_o.astype(bf16)

    x2 = x.reshape(batch * seq, D_MODEL)
    c2 = ctx.reshape(batch * ctx_len, D_MODEL)
    for l in range(depth):
        last = l == depth - 1
        pc = _projection(c2, l, norm_w, mod, w_in_b, conv_w, None,
                         tm=4 * ctx_len, sub=ctx_len, seq=ctx_len, mod_row0=batch,
                         mod_per_seq=False)
        px = _projection(x2, l, norm_w, mod, w_in_b, conv_w, rope_tabs,
                         tm=1024, sub=512, seq=seq, mod_row0=0, mod_per_seq=True)
        kc, vc = pc[2], pc[3]
        x2 = _mixer(x2, l, mod, attn_sink, px, kc, vc, wa_b, wb_b, wo_b,
                    final_norm_w if last else None,
                    seq=seq, tq=512, has_local=True, gate_row0=0, gate_per_batch=True)
        if not last:
            c2 = _mixer(c2, l, mod, attn_sink, pc, kc, vc, wa_b, wb_b, wo_b, None,
                        seq=ctx_len, tq=ctx_len, has_local=False, gate_row0=batch,
                        gate_per_batch=False)
    return x2.reshape(batch, seq, D_MODEL)
```

```python
import functools
import math

import numpy as np
import jax
import jax.numpy as jnp
from jax import lax
from jax.experimental import pallas as pl
from jax.experimental.pallas import tpu as pltpu

D_MODEL = 1024
GRID_W = 64
CONV_WIDTH = 512
CONV_K = 3
N_HEADS = 8
N_KV_HEADS = 2
HEAD_DIM = 64
ATTN_WIDTH = N_HEADS * HEAD_DIM
KV_WIDTH = N_KV_HEADS * HEAD_DIM
BLOCK = 128
ROPE_THETA = 10000.0
ROPE_FREQS = HEAD_DIM // 4
EPS = 1e-6
NEG = -1e30
LOG2E = math.log2(math.e)

LANES = 128
SUBLANES = 8
MOD_ROWS = 24
VMEM_LIMIT = 56 * 1024 * 1024
N_SLAB = ATTN_WIDTH // LANES
HEADS_PER_SLAB = LANES // HEAD_DIM
SLABS_PER_KV = N_SLAB // N_KV_HEADS

_OFF_B = 0
_OFF_C = _OFF_B + CONV_WIDTH
_OFF_U = _OFF_C + CONV_WIDTH
_OFF_ZA = _OFF_U + CONV_WIDTH
_OFF_Q = _OFF_ZA + CONV_WIDTH
_OFF_K = _OFF_Q + ATTN_WIDTH
_OFF_V = _OFF_K + KV_WIDTH
_OFF_ZB = _OFF_V + KV_WIDTH
_OFF_GA = _OFF_ZB + ATTN_WIDTH
_OFF_GB = _OFF_GA + D_MODEL
IN_COLS = _OFF_GB + D_MODEL

f32 = jnp.float32
bf16 = jnp.bfloat16


def _sigmoid(x):
    return 0.5 * jnp.tanh(0.5 * x) + 0.5


def _silu(x):
    return x * _sigmoid(x)


def _mod_kernel(c_ref, w_ref, b_ref, o_ref):
    s = _silu(c_ref[...])
    o_ref[0] = jnp.dot(s, w_ref[0], preferred_element_type=f32,
                       precision=lax.Precision.HIGHEST) + b_ref[0]


def _modulation(c_all, w_mod, b_mod):
    depth = w_mod.shape[0]
    n_col = 3 * D_MODEL // D_MODEL
    return pl.pallas_call(
        _mod_kernel,
        grid=(depth, n_col),
        in_specs=[
            pl.BlockSpec((MOD_ROWS, D_MODEL), lambda l, j: (0, 0)),
            pl.BlockSpec((1, D_MODEL, D_MODEL), lambda l, j: (l, 0, j)),
            pl.BlockSpec((1, 1, D_MODEL), lambda l, j: (l, 0, j)),
        ],
        out_specs=pl.BlockSpec((1, MOD_ROWS, D_MODEL), lambda l, j: (l, 0, j)),
        out_shape=jax.ShapeDtypeStruct((depth, MOD_ROWS, 3 * D_MODEL), f32),
        name="modulation",
    )(c_all, w_mod, b_mod.reshape(depth, 1, 3 * D_MODEL))


def _rope(p, cos, sin_lo, sin_hi):
    outs = []
    for j in range(p.shape[1] // LANES):
        x = p[:, j * LANES:(j + 1) * LANES]
        outs.append(x * cos
                    + pltpu.roll(x, LANES - ROPE_FREQS, 1) * sin_lo
                    + pltpu.roll(x, ROPE_FREQS, 1) * sin_hi)
    return outs[0] if len(outs) == 1 else jnp.concatenate(outs, axis=1)


def _with_swapped_halves(a):
    return jnp.concatenate([a, pltpu.roll(a, HEAD_DIM, 1)], axis=1)


def _norm_mod(x, gain, shift):
    ms = jnp.mean(x * x, axis=-1, keepdims=True)
    return (x * lax.rsqrt(ms + EPS) * gain + shift).astype(bf16)


def _proj_kernel(*refs, rope, seq, subs):
    it = iter(refs)
    x_ref, xp_ref, xn_ref, nw_ref, shift_ref, scale_ref, w_ref, cw_ref = (
        next(it) for _ in range(8))
    if rope:
        cos_ref, slo_ref, shi_ref = (next(it) for _ in range(3))
    ya_ref, q_ref, k_ref, v_ref, zb_ref, ga_ref, gb_ref = (next(it) for _ in range(7))
    h_s, cu_s = next(it), next(it)
    tm = x_ref.shape[0]
    assert sum(subs) == tm and (seq % tm == 0 or tm % seq == 0)
    halo = 2 * SUBLANES
    gain, shift = nw_ref[...] * (1.0 + scale_ref[...]), shift_ref[...]
    tiles_per_seq = max(seq // tm, 1)
    t_in_seq = pl.program_id(0) % tiles_per_seq
    cw = cw_ref[...]
    zero = jnp.zeros((SUBLANES, CONV_WIDTH), f32)

    def buffers(n):
        s0 = sum(subs[:n]) + n * halo
        return h_s.at[s0:s0 + subs[n] + halo], cu_s.at[s0:s0 + subs[n] + halo]

    for n, sub in enumerate(subs):
        r0 = sum(subs[:n])
        rows = slice(r0, r0 + sub)
        h, cu_b = buffers(n)
        before = xp_ref[...] if r0 == 0 else x_ref[r0 - SUBLANES:r0, :]
        after = xn_ref[...] if r0 + sub == tm else x_ref[r0 + sub:r0 + sub + SUBLANES, :]
        h[0:sub, :] = _norm_mod(x_ref[rows, :], gain, shift)
        h[sub:sub + halo, :] = _norm_mod(jnp.concatenate([before, after], axis=0), gain, shift)

        def proj(off, width, n_rows=sub):
            return jnp.dot(h[0:n_rows, :], w_ref[:, off:off + width],
                           preferred_element_type=f32)

        cu = proj(_OFF_C, CONV_WIDTH, sub + halo) * proj(_OFF_U, CONV_WIDTH, sub + halo)
        cu_before, cu_after = cu[sub:sub + SUBLANES], cu[sub + SUBLANES:]
        if r0 % seq == 0:
            cu_before = jnp.where(t_in_seq > 0, cu_before, zero) if r0 == 0 else zero
        if (r0 + sub) % seq == 0 or r0 + sub == tm:
            cu_after = (jnp.where(t_in_seq < tiles_per_seq - 1, cu_after, zero)
                        if r0 + sub == tm else zero)
        cu_b[0:SUBLANES, :] = cu_before
        cu_b[SUBLANES:SUBLANES + sub, :] = cu[0:sub]
        cu_b[SUBLANES + sub:, :] = cu_after
        conv = (cu_b[SUBLANES - 1:SUBLANES - 1 + sub, :] * cw[0:1]
                + cu_b[SUBLANES:SUBLANES + sub, :] * cw[1:2]
                + cu_b[SUBLANES + 1:SUBLANES + 1 + sub, :] * cw[2:3])
        ya_ref[rows, :] = (proj(_OFF_B, CONV_WIDTH) * conv
                           * _silu(proj(_OFF_ZA, CONV_WIDTH))).astype(bf16)

        q = proj(_OFF_Q, ATTN_WIDTH)
        kv = proj(_OFF_K, 2 * KV_WIDTH)
        k = kv[:, :KV_WIDTH]
        if rope:
            cos, slo, shi = cos_ref[rows, :], slo_ref[rows, :], shi_ref[rows, :]
            q = _rope(q, cos, slo, shi)
            k = _rope(k, cos, slo, shi)
        q_ref[rows, :] = (q * (HEAD_DIM ** -0.5 * LOG2E)).astype(bf16)
        k_ref[rows, :] = _with_swapped_halves(k).astype(bf16)
        v_ref[rows, :] = _with_swapped_halves(kv[:, KV_WIDTH:]).astype(bf16)
        zb_ref[rows, :] = _silu(proj(_OFF_ZB, ATTN_WIDTH)).astype(bf16)
        part = 2 * LANES
        for c in range(D_MODEL // part):
            ga_ref[rows, c * part:(c + 1) * part] = _sigmoid(
                proj(_OFF_GA + c * part, part)).astype(bf16)
        for c in range(D_MODEL // part):
            gb_ref[rows, c * part:(c + 1) * part] = _sigmoid(
                proj(_OFF_GB + c * part, part)).astype(bf16)


def _mod_spec(layer, row_of, which):
    return pl.BlockSpec((None, None, None, 1, D_MODEL),
                        lambda *g: (layer, row_of(*g), which, 0, 0))


def _layer_spec(layer, shape):
    return pl.BlockSpec((None,) + tuple(shape), lambda *g: (layer,) + (0,) * len(shape))


def _projection(x2, layer, norm_w, mod, w_in, conv_w, rope_tabs, *, subs, seq, mod_row0,
                mod_per_seq):
    tm = sum(subs)
    scratch_rows = tm + len(subs) * 2 * SUBLANES
    n_rows = x2.shape[0]
    assert seq % tm == 0 or (tm % seq == 0 and not mod_per_seq and rope_tabs is None)
    tiles_per_seq = max(seq // tm, 1)
    halo_per_tile = tm // SUBLANES
    n_halo = n_rows // SUBLANES
    if mod_per_seq:
        mod_row = lambda t: mod_row0 + t // tiles_per_seq
    else:
        mod_row = lambda t: mod_row0
    row = lambda w: pl.BlockSpec((tm, w), lambda t: (t, 0))
    in_specs = [
        row(D_MODEL),
        pl.BlockSpec((SUBLANES, D_MODEL),
                     lambda t: (jnp.maximum(t * halo_per_tile - 1, 0), 0)),
        pl.BlockSpec((SUBLANES, D_MODEL),
                     lambda t: (jnp.minimum((t + 1) * halo_per_tile, n_halo - 1), 0)),
        _layer_spec(layer, (1, D_MODEL)),
        _mod_spec(layer, mod_row, 0),
        _mod_spec(layer, mod_row, 1),
        _layer_spec(layer, (D_MODEL, IN_COLS)),
        _layer_spec(layer, (CONV_K, CONV_WIDTH)),
    ]
    args = [x2, x2, x2, norm_w.reshape(-1, 1, D_MODEL), mod, mod, w_in, conv_w]
    if rope_tabs is not None:
        in_specs += [pl.BlockSpec((tm, LANES), lambda t: (t % tiles_per_seq, 0))] * 3
        args += list(rope_tabs)
    widths = (CONV_WIDTH, ATTN_WIDTH, 2 * KV_WIDTH, 2 * KV_WIDTH, ATTN_WIDTH, D_MODEL, D_MODEL)
    return pl.pallas_call(
        functools.partial(_proj_kernel, rope=rope_tabs is not None,
                          seq=seq, subs=tuple(subs)),
        grid=(n_rows // tm,),
        in_specs=in_specs,
        out_specs=[row(w) for w in widths],
        out_shape=[jax.ShapeDtypeStruct((n_rows, w), bf16) for w in widths],
        scratch_shapes=[pltpu.VMEM((scratch_rows, D_MODEL), bf16),
                        pltpu.VMEM((scratch_rows, CONV_WIDTH), f32)],
        compiler_params=pltpu.CompilerParams(
            dimension_semantics=("arbitrary",), vmem_limit_bytes=VMEM_LIMIT),
        name="projection_rope" if rope_tabs is not None else "projection",
    )(*args)


def _kv_variants(a2):
    a, sw = a2[:, :LANES], a2[:, LANES:]
    low = lax.broadcasted_iota(jnp.int32, a.shape, 1) < HEAD_DIM
    z = jnp.zeros_like(a)
    return [jnp.where(low, a, z), jnp.where(low, z, sw),
            jnp.where(low, sw, z), jnp.where(low, z, a)]


def _mix_kernel(*refs, layer, tq, has_local, final):
    it = iter(refs)
    sink_ref = next(it)
    x_ref, gate_ref, ya_ref, q_ref, zb_ref, ga_ref, gb_ref = (next(it) for _ in range(7))
    if has_local:
        k_ref, kp_ref, kn_ref, v_ref, vp_ref, vn_ref = (next(it) for _ in range(6))
    kc_ref, vc_ref = next(it), next(it)
    wa_ref, wb_ref, wo_ref = (next(it) for _ in range(3))
    if final:
        fw_ref = next(it)
    o_ref = next(it)
    y_s, yb_s, k_s, vm_s, s_s, p_s, z_s = (next(it) for _ in range(7))
    if has_local:
        bias_s = next(it)

    i = pl.program_id(1)
    n_i = pl.num_programs(1)
    nblk = tq // BLOCK
    ctx_len = kc_ref.shape[0]
    n_loc = 3 * BLOCK if has_local else 0
    nk = ctx_len + n_loc
    n_slot = vm_s.shape[0]
    rows_a = SLABS_PER_KV * BLOCK

    @pl.when(i == 0)
    def _():
        kvar = _kv_variants(kc_ref[...])
        vvar = _kv_variants(vc_ref[...])
        low = lax.broadcasted_iota(jnp.int32, (nk, LANES), 1) < HEAD_DIM
        ones_on = [jnp.where(low, 1.0, 0.0).astype(bf16), jnp.where(low, 0.0, 1.0).astype(bf16)]
        for n in range(4):
            k_s[n, 0:ctx_len, :] = kvar[n]
        for slot in range(n_slot):
            for kh in range(N_KV_HEADS):
                for e in range(HEADS_PER_SLAB):
                    vm_s[slot, kh, e * nk:e * nk + ctx_len, 0:LANES] = vvar[2 * kh + e]
                    vm_s[slot, kh, e * nk:(e + 1) * nk, LANES:] = ones_on[e]

    def local_block(ref_prev, ref_tile, ref_next, t):
        if t == 0:
            return ref_prev[...]
        if t == nblk + 1:
            return ref_next[...]
        return ref_tile[(t - 1) * BLOCK:t * BLOCK, :]

    def fill_values(j, b):
        vvar = _kv_variants(local_block(vp_ref, v_ref, vn_ref, j + b))
        for kh in range(N_KV_HEADS):
            for e in range(HEADS_PER_SLAB):
                r = e * nk + ctx_len + b * BLOCK
                vm_s[j, kh, r:r + BLOCK, 0:LANES] = vvar[2 * kh + e]

    def fill_keys(t):
        for n, a in enumerate(_kv_variants(local_block(kp_ref, k_ref, kn_ref, t))):
            k_s[n, ctx_len + t * BLOCK:ctx_len + (t + 1) * BLOCK, :] = a

    def fill_bias():
        qi = lax.broadcasted_iota(jnp.int32, (rows_a, BLOCK), 0) % BLOCK
        ci = lax.broadcasted_iota(jnp.int32, (rows_a, BLOCK), 1)
        tri_prev = jnp.where(ci >= qi, 0.0, NEG)
        tri_next = jnp.where(ci <= qi, 0.0, NEG)
        bias_s[0] = tri_prev + jnp.where(i == 0, NEG, 0.0)
        bias_s[1] = tri_prev
        bias_s[2] = tri_next
        bias_s[3] = tri_next + jnp.where(i == n_i - 1, NEG, 0.0)

    if has_local:
        for t in range(nblk + 2):
            fill_keys(t)
        fill_bias()

    nt = (((1,), (1,)), ((), ()))

    def stage_a(j, kh, e):
        u = (j * N_KV_HEADS + kh) * HEADS_PER_SLAB + e
        rows = slice(j * BLOCK, (j + 1) * BLOCK)
        q2 = jnp.concatenate(
            [q_ref[rows, (kh * SLABS_PER_KV + s) * LANES:(kh * SLABS_PER_KV + s + 1) * LANES]
             for s in range(SLABS_PER_KV)], axis=0)
        var = 2 * kh + e
        s_s[u, :, 0:ctx_len] = lax.dot_general(
            q2, k_s[var, 0:ctx_len, :], nt, preferred_element_type=f32)
        if has_local:
            r0 = ctx_len + j * BLOCK
            s_loc = lax.dot_general(q2, k_s[var, r0:r0 + n_loc, :], nt,
                                    preferred_element_type=f32)
            c0 = ctx_len
            s_s[u, :, c0:c0 + BLOCK] = s_loc[:, :BLOCK] + bias_s[0 if j == 0 else 1]
            s_s[u, :, c0 + BLOCK:c0 + 2 * BLOCK] = s_loc[:, BLOCK:2 * BLOCK]
            s_s[u, :, c0 + 2 * BLOCK:] = s_loc[:, 2 * BLOCK:] + bias_s[3 if j == nblk - 1 else 2]

    low_half = lax.broadcasted_iota(jnp.int32, (BLOCK, LANES), 1) < HEAD_DIM

    def stage_b(j, kh):
        jk = j * N_KV_HEADS + kh
        for s in range(SLABS_PER_KV):
            rs = slice(s * BLOCK, (s + 1) * BLOCK)
            z = []
            for e in range(HEADS_PER_SLAB):
                u = jk * HEADS_PER_SLAB + e
                sk = sink_ref[layer, (kh * SLABS_PER_KV + s) * HEADS_PER_SLAB + e] * LOG2E
                mx = s_s[u, rs, 0:LANES]
                for n in range(1, nk // LANES):
                    mx = jnp.maximum(mx, s_s[u, rs, n * LANES:(n + 1) * LANES])
                m = jnp.maximum(jnp.max(mx, axis=1, keepdims=True), sk)
                for n in range(nk // LANES):
                    col = e * nk + n * LANES
                    p_s[jk, rs, col:col + LANES] = jnp.exp2(
                        s_s[u, rs, n * LANES:(n + 1) * LANES] - m).astype(bf16)
                z.append(jnp.exp2(sk - m))
            z_s[jk, rs, :] = jnp.where(low_half, z[0], z[1])

    def stage_c(j, kh):
        jk = j * N_KV_HEADS + kh
        res = jnp.dot(p_s[jk], vm_s[j if has_local else 0, kh], preferred_element_type=f32)
        attn = res[:, :LANES] / (res[:, LANES:] + z_s[jk])
        rows = slice(j * BLOCK, (j + 1) * BLOCK)
        for s in range(SLABS_PER_KV):
            c = kh * SLABS_PER_KV + s
            lanes = slice(c * LANES, (c + 1) * LANES)
            yb_s[rows, lanes] = (attn[s * BLOCK:(s + 1) * BLOCK]
                                 * zb_ref[rows, lanes].astype(f32)).astype(bf16)

    for t in range(nblk + 2):
        for kh in range(N_KV_HEADS):
            if has_local and 0 <= t - 1 < nblk:
                for b in range(3)[kh::N_KV_HEADS]:
                    fill_values(t - 1, b)
            if 0 <= t - 1 < nblk:
                stage_b(t - 1, kh)
            if 0 <= t - 2 < nblk:
                stage_c(t - 2, kh)
            if t < nblk:
                for e in range(HEADS_PER_SLAB):
                    stage_a(t, kh, e)

    n_part = 4
    width = D_MODEL // n_part
    for n in range(n_part):
        cols = slice(n * width, (n + 1) * width)
        y_s[:, cols] = (
            jnp.dot(ya_ref[...], wa_ref[:, cols], preferred_element_type=f32)
            * ga_ref[:, cols].astype(f32)
            + jnp.dot(yb_s[...], wb_ref[:, cols], preferred_element_type=f32)
            * gb_ref[:, cols].astype(f32)).astype(bf16)
    y16 = y_s[...]
    if final:
        xn = x_ref[...] + gate_ref[...] * jnp.dot(y16, wo_ref[...], preferred_element_type=f32)
        ms = jnp.mean(xn * xn, axis=-1, keepdims=True)
        o_ref[...] = xn * lax.rsqrt(ms + EPS) * fw_ref[...]
    else:
        for n in range(n_part):
            cols = slice(n * width, (n + 1) * width)
            o_ref[:, cols] = x_ref[:, cols] + gate_ref[:, cols] * jnp.dot(
                y16, wo_ref[:, cols], preferred_element_type=f32)


def _mixer(x2, layer, mod, sink, proj_out, kc, vc, wa, wb, wo, final_w, *,
           seq, tq, has_local, gate_row0, gate_per_batch):
    ya, q, k, v, zb, ga, gb = proj_out
    n_rows = x2.shape[0]
    batch = n_rows // seq
    ctx_len = kc.shape[0] // batch
    n_i = seq // tq
    nblk = tq // BLOCK
    final = final_w is not None

    def row(w):
        return pl.BlockSpec((tq, w), lambda b, i: (b * n_i + i, 0))

    def halo(rows_blk, w, side):
        per_tile = tq // rows_blk
        per_seq = seq // rows_blk
        if side < 0:
            return pl.BlockSpec((rows_blk, w), lambda b, i: (
                b * per_seq + jnp.maximum(i * per_tile - 1, 0), 0))
        return pl.BlockSpec((rows_blk, w), lambda b, i: (
            b * per_seq + jnp.minimum((i + 1) * per_tile, per_seq - 1), 0))

    def whole(shape):
        return pl.BlockSpec(shape, lambda b, i: (0,) * len(shape))

    if gate_per_batch:
        gate_spec = _mod_spec(layer, lambda b, i: gate_row0 + b, 2)
    else:
        gate_spec = _mod_spec(layer, lambda b, i: gate_row0, 2)

    kvw = 2 * KV_WIDTH
    in_specs = [pl.BlockSpec(memory_space=pltpu.SMEM),
                row(D_MODEL), gate_spec, row(CONV_WIDTH),
                row(ATTN_WIDTH), row(ATTN_WIDTH), row(D_MODEL), row(D_MODEL)]
    args = [sink, x2, mod, ya, q, zb, ga, gb]
    if has_local:
        in_specs += [row(kvw), halo(BLOCK, kvw, -1), halo(BLOCK, kvw, +1)] * 2
        args += [k, k, k, v, v, v]
    ctx_spec = pl.BlockSpec((ctx_len, kvw), lambda b, i: (b, 0))
    in_specs += [ctx_spec, ctx_spec, _layer_spec(layer, (CONV_WIDTH, D_MODEL)),
                 _layer_spec(layer, (ATTN_WIDTH, D_MODEL)), _layer_spec(layer, (D_MODEL, D_MODEL))]
    args += [kc, vc, wa, wb, wo]
    if final:
        in_specs.append(whole((1, D_MODEL)))
        args.append(final_w.reshape(1, D_MODEL))

    nk = ctx_len + (3 * BLOCK if has_local else 0)
    k_rows = ctx_len + tq + 2 * BLOCK if has_local else ctx_len
    n_slot = nblk if has_local else 1
    rows_a = SLABS_PER_KV * BLOCK
    scratch = [pltpu.VMEM((tq, D_MODEL), bf16),
               pltpu.VMEM((tq, ATTN_WIDTH), bf16),
               pltpu.VMEM((2 * N_KV_HEADS, k_rows, LANES), bf16),
               pltpu.VMEM((n_slot, N_KV_HEADS, HEADS_PER_SLAB * nk, 2 * LANES), bf16),
               pltpu.VMEM((nblk * N_KV_HEADS * HEADS_PER_SLAB, rows_a, nk), f32),
               pltpu.VMEM((nblk * N_KV_HEADS, rows_a, HEADS_PER_SLAB * nk), bf16),
               pltpu.VMEM((nblk * N_KV_HEADS, rows_a, LANES), f32)]
    if has_local:
        scratch += [pltpu.VMEM((4, rows_a, BLOCK), f32)]

    return pl.pallas_call(
        functools.partial(_mix_kernel, layer=layer, tq=tq, has_local=has_local, final=final),
        grid=(batch, n_i),
        in_specs=in_specs,
        out_specs=row(D_MODEL),
        out_shape=jax.ShapeDtypeStruct((n_rows, D_MODEL), f32),
        scratch_shapes=scratch,
        compiler_params=pltpu.CompilerParams(
            dimension_semantics=("arbitrary", "arbitrary"), vmem_limit_bytes=VMEM_LIMIT),
        name="mixer_latent" if has_local else "mixer_context",
    )(*args)


def _rope_tables(seq):
    n_rows = seq // GRID_W
    inv_freq = jnp.asarray(ROPE_THETA, f32) ** (-jnp.arange(ROPE_FREQS, dtype=f32) / ROPE_FREQS)
    lane = np.arange(LANES)
    lane_freq = inv_freq[lane % ROPE_FREQS][None, :]
    use_col = ((lane % HEAD_DIM) >= HEAD_DIM // 2)[None, None, :]
    upper = ((lane % (2 * ROPE_FREQS)) >= ROPE_FREQS)[None, None, :]
    ang_r = jnp.arange(n_rows, dtype=f32)[:, None] * lane_freq
    ang_c = jnp.arange(GRID_W, dtype=f32)[:, None] * lane_freq
    shape = (n_rows, GRID_W, LANES)
    cos = jnp.where(use_col, jnp.cos(ang_c)[None], jnp.cos(ang_r)[:, None]).reshape(seq, LANES)
    sin = jnp.broadcast_to(jnp.where(use_col, jnp.sin(ang_c)[None], jnp.sin(ang_r)[:, None]),
                           shape)
    zero = jnp.zeros(shape, f32)
    sin_lo = jnp.where(upper, zero, -sin).reshape(seq, LANES)
    sin_hi = jnp.where(upper, sin, zero).reshape(seq, LANES)
    return cos, sin_lo, sin_hi


def kernel(x, c, ctx, c_ctx, norm_w, w_mod, b_mod, w_in, conv_w, w_a_out, w_b_out,
           attn_sink, w_o, final_norm_w):
    batch, seq, _ = x.shape
    ctx_len = ctx.shape[1]
    depth = w_in.shape[0]
    assert batch + 1 <= MOD_ROWS and seq % 512 == 0 and ctx_len % BLOCK == 0

    c_all = jnp.zeros((MOD_ROWS, D_MODEL), f32).at[:batch].set(c).at[batch].set(c_ctx)
    mod = _modulation(c_all, w_mod, b_mod).reshape(depth, MOD_ROWS, 3, 1, D_MODEL)
    rope_tabs = _rope_tables(seq)

    w_in_b = w_in.astype(bf16)
    wa_b, wb_b, wo_b = w_a_out.astype(bf16), w_b_out.astype(bf16), w_o.astype(bf16)

    x2 = x.reshape(batch * seq, D_MODEL)
    c2 = ctx.reshape(batch * ctx_len, D_MODEL)
    for l in range(depth):
        last = l == depth - 1
        pc = _projection(c2, l, norm_w, mod, w_in_b, conv_w, None,
                         subs=(ctx_len,) * 4, seq=ctx_len, mod_row0=batch, mod_per_seq=False)
        px = _projection(x2, l, norm_w, mod, w_in_b, conv_w, rope_tabs,
                         subs=(512, 512), seq=seq, mod_row0=0, mod_per_seq=True)
        kc, vc = pc[2], pc[3]
        x2 = _mixer(x2, l, mod, attn_sink, px, kc, vc, wa_b, wb_b, wo_b,
                    final_norm_w if last else None,
                    seq=seq, tq=512, has_local=True, gate_row0=0, gate_per_batch=True)
        if not last:
            c2 = _mixer(c2, l, mod, attn_sink, pc, kc, vc, wa_b, wb_b, wo_b, None,
                        seq=ctx_len, tq=ctx_len, has_local=False, gate_row0=batch,
                        gate_per_batch=False)
    return x2.reshape(batch, seq, D_MODEL)
```

```python
import functools
import math

import numpy as np
import jax
import jax.numpy as jnp
from jax import lax
from jax.experimental import pallas as pl
from jax.experimental.pallas import tpu as pltpu

D_MODEL = 1024
GRID_W = 64
CONV_WIDTH = 512
CONV_K = 3
N_HEADS = 8
N_KV_HEADS = 2
HEAD_DIM = 64
ATTN_WIDTH = N_HEADS * HEAD_DIM
KV_WIDTH = N_KV_HEADS * HEAD_DIM
BLOCK = 128
ROPE_THETA = 10000.0
ROPE_FREQS = HEAD_DIM // 4
EPS = 1e-6
NEG = -1e30
LOG2E = math.log2(math.e)

LANES = 128
SUBLANES = 8
MOD_ROWS = 24
VMEM_LIMIT = 56 * 1024 * 1024
N_SLAB = ATTN_WIDTH // LANES
HEADS_PER_SLAB = LANES // HEAD_DIM
SLABS_PER_KV = N_SLAB // N_KV_HEADS

_OFF_B = 0
_OFF_C = _OFF_B + CONV_WIDTH
_OFF_U = _OFF_C + CONV_WIDTH
_OFF_ZA = _OFF_U + CONV_WIDTH
_OFF_Q = _OFF_ZA + CONV_WIDTH
_OFF_K = _OFF_Q + ATTN_WIDTH
_OFF_V = _OFF_K + KV_WIDTH
_OFF_ZB = _OFF_V + KV_WIDTH
_OFF_GA = _OFF_ZB + ATTN_WIDTH
_OFF_GB = _OFF_GA + D_MODEL
IN_COLS = _OFF_GB + D_MODEL

f32 = jnp.float32
bf16 = jnp.bfloat16


def _sigmoid(x):
    return 0.5 * jnp.tanh(0.5 * x) + 0.5


def _silu(x):
    return x * _sigmoid(x)


def _mod_kernel(c_ref, w_ref, b_ref, o_ref):
    s = _silu(c_ref[...])
    o_ref[0] = jnp.dot(s, w_ref[0], preferred_element_type=f32,
                       precision=lax.Precision.HIGHEST) + b_ref[0]


def _modulation(c_all, w_mod, b_mod):
    depth = w_mod.shape[0]
    n_col = 3 * D_MODEL // D_MODEL
    return pl.pallas_call(
        _mod_kernel,
        grid=(depth, n_col),
        in_specs=[
            pl.BlockSpec((MOD_ROWS, D_MODEL), lambda l, j: (0, 0)),
            pl.BlockSpec((1, D_MODEL, D_MODEL), lambda l, j: (l, 0, j)),
            pl.BlockSpec((1, 1, D_MODEL), lambda l, j: (l, 0, j)),
        ],
        out_specs=pl.BlockSpec((1, MOD_ROWS, D_MODEL), lambda l, j: (l, 0, j)),
        out_shape=jax.ShapeDtypeStruct((depth, MOD_ROWS, 3 * D_MODEL), f32),
        name="modulation",
    )(c_all, w_mod, b_mod.reshape(depth, 1, 3 * D_MODEL))


def _rope(p, cos, sin_lo, sin_hi):
    outs = []
    for j in range(p.shape[1] // LANES):
        x = p[:, j * LANES:(j + 1) * LANES]
        outs.append(x * cos
                    + pltpu.roll(x, LANES - ROPE_FREQS, 1) * sin_lo
                    + pltpu.roll(x, ROPE_FREQS, 1) * sin_hi)
    return outs[0] if len(outs) == 1 else jnp.concatenate(outs, axis=1)


def _with_swapped_halves(a):
    return jnp.concatenate([a, pltpu.roll(a, HEAD_DIM, 1)], axis=1)


def _norm_mod(x, gain, shift):
    ms = jnp.mean(x * x, axis=-1, keepdims=True)
    return (x * lax.rsqrt(ms + EPS) * gain + shift).astype(bf16)


def _proj_kernel(*refs, rope, seq, subs):
    it = iter(refs)
    x_ref, xp_ref, xn_ref, nw_ref, shift_ref, scale_ref, w_ref, cw_ref = (
        next(it) for _ in range(8))
    if rope:
        cos_ref, slo_ref, shi_ref = (next(it) for _ in range(3))
    ya_ref, q_ref, k_ref, v_ref, zb_ref, ga_ref, gb_ref = (next(it) for _ in range(7))
    h_s, cu_s = next(it), next(it)
    tm = x_ref.shape[0]
    assert sum(subs) == tm and (seq % tm == 0 or tm % seq == 0)
    halo = 2 * SUBLANES
    gain, shift = nw_ref[...] * (1.0 + scale_ref[...]), shift_ref[...]
    tiles_per_seq = max(seq // tm, 1)
    t_in_seq = pl.program_id(0) % tiles_per_seq
    cw = cw_ref[...]
    zero = jnp.zeros((SUBLANES, CONV_WIDTH), f32)

    def buffers(n):
        s0 = sum(subs[:n]) + n * halo
        return h_s.at[s0:s0 + subs[n] + halo], cu_s.at[s0:s0 + subs[n] + halo]

    for n, sub in enumerate(subs):
        r0 = sum(subs[:n])
        rows = slice(r0, r0 + sub)
        h, cu_b = buffers(n)
        before = xp_ref[...] if r0 == 0 else x_ref[r0 - SUBLANES:r0, :]
        after = xn_ref[...] if r0 + sub == tm else x_ref[r0 + sub:r0 + sub + SUBLANES, :]
        h[0:sub, :] = _norm_mod(x_ref[rows, :], gain, shift)
        h[sub:sub + halo, :] = _norm_mod(jnp.concatenate([before, after], axis=0), gain, shift)

        def proj(off, width, n_rows=sub):
            return jnp.dot(h[0:n_rows, :], w_ref[:, off:off + width],
                           preferred_element_type=f32)

        cu = proj(_OFF_C, CONV_WIDTH, sub + halo) * proj(_OFF_U, CONV_WIDTH, sub + halo)
        cu_before, cu_after = cu[sub:sub + SUBLANES], cu[sub + SUBLANES:]
        if r0 % seq == 0:
            cu_before = jnp.where(t_in_seq > 0, cu_before, zero) if r0 == 0 else zero
        if (r0 + sub) % seq == 0 or r0 + sub == tm:
            cu_after = (jnp.where(t_in_seq < tiles_per_seq - 1, cu_after, zero)
                        if r0 + sub == tm else zero)
        cu_b[0:SUBLANES, :] = cu_before
        cu_b[SUBLANES:SUBLANES + sub, :] = cu[0:sub]
        cu_b[SUBLANES + sub:, :] = cu_after
        conv = (cu_b[SUBLANES - 1:SUBLANES - 1 + sub, :] * cw[0:1]
                + cu_b[SUBLANES:SUBLANES + sub, :] * cw[1:2]
                + cu_b[SUBLANES + 1:SUBLANES + 1 + sub, :] * cw[2:3])
        ya_ref[rows, :] = (proj(_OFF_B, CONV_WIDTH) * conv
                           * _silu(proj(_OFF_ZA, CONV_WIDTH))).astype(bf16)

        q = proj(_OFF_Q, ATTN_WIDTH)
        kv = proj(_OFF_K, 2 * KV_WIDTH)
        k = kv[:, :KV_WIDTH]
        if rope:
            cos, slo, shi = cos_ref[rows, :], slo_ref[rows, :], shi_ref[rows, :]
            q = _rope(q, cos, slo, shi)
            k = _rope(k, cos, slo, shi)
        q_ref[rows, :] = (q * (HEAD_DIM ** -0.5 * LOG2E)).astype(bf16)
        k_ref[rows, :] = _with_swapped_halves(k).astype(bf16)
        v_ref[rows, :] = _with_swapped_halves(kv[:, KV_WIDTH:]).astype(bf16)
        zb_ref[rows, :] = _silu(proj(_OFF_ZB, ATTN_WIDTH)).astype(bf16)
        part = 2 * LANES
        for c in range(D_MODEL // part):
            ga_ref[rows, c * part:(c + 1) * part] = _sigmoid(
                proj(_OFF_GA + c * part, part)).astype(bf16)
        for c in range(D_MODEL // part):
            gb_ref[rows, c * part:(c + 1) * part] = _sigmoid(
                proj(_OFF_GB + c * part, part)).astype(bf16)


def _mod_spec(layer, row_of, which):
    return pl.BlockSpec((None, None, None, 1, D_MODEL),
                        lambda *g: (layer, row_of(*g), which, 0, 0))


def _layer_spec(layer, shape):
    return pl.BlockSpec((None,) + tuple(shape), lambda *g: (layer,) + (0,) * len(shape))


def _projection(x2, layer, norm_w, mod, w_in, conv_w, rope_tabs, *, subs, seq, mod_row0,
                mod_per_seq):
    tm = sum(subs)
    scratch_rows = tm + len(subs) * 2 * SUBLANES
    n_rows = x2.shape[0]
    assert seq % tm == 0 or (tm % seq == 0 and not mod_per_seq and rope_tabs is None)
    tiles_per_seq = max(seq // tm, 1)
    halo_per_tile = tm // SUBLANES
    n_halo = n_rows // SUBLANES
    if mod_per_seq:
        mod_row = lambda t: mod_row0 + t // tiles_per_seq
    else:
        mod_row = lambda t: mod_row0
    row = lambda w: pl.BlockSpec((tm, w), lambda t: (t, 0))
    in_specs = [
        row(D_MODEL),
        pl.BlockSpec((SUBLANES, D_MODEL),
                     lambda t: (jnp.maximum(t * halo_per_tile - 1, 0), 0)),
        pl.BlockSpec((SUBLANES, D_MODEL),
                     lambda t: (jnp.minimum((t + 1) * halo_per_tile, n_halo - 1), 0)),
        _layer_spec(layer, (1, D_MODEL)),
        _mod_spec(layer, mod_row, 0),
        _mod_spec(layer, mod_row, 1),
        _layer_spec(layer, (D_MODEL, IN_COLS)),
        _layer_spec(layer, (CONV_K, CONV_WIDTH)),
    ]
    args = [x2, x2, x2, norm_w.reshape(-1, 1, D_MODEL), mod, mod, w_in, conv_w]
    if rope_tabs is not None:
        in_specs += [pl.BlockSpec((tm, LANES), lambda t: (t % tiles_per_seq, 0))] * 3
        args += list(rope_tabs)
    widths = (CONV_WIDTH, ATTN_WIDTH, 2 * KV_WIDTH, 2 * KV_WIDTH, ATTN_WIDTH, D_MODEL, D_MODEL)
    return pl.pallas_call(
        functools.partial(_proj_kernel, rope=rope_tabs is not None,
                          seq=seq, subs=tuple(subs)),
        grid=(n_rows // tm,),
        in_specs=in_specs,
        out_specs=[row(w) for w in widths],
        out_shape=[jax.ShapeDtypeStruct((n_rows, w), bf16) for w in widths],
        scratch_shapes=[pltpu.VMEM((scratch_rows, D_MODEL), bf16),
                        pltpu.VMEM((scratch_rows, CONV_WIDTH), f32)],
        compiler_params=pltpu.CompilerParams(
            dimension_semantics=("arbitrary",), vmem_limit_bytes=VMEM_LIMIT),
        name="projection_rope" if rope_tabs is not None else "projection",
    )(*args)


def _kv_variants(a2):
    a, sw = a2[:, :LANES], a2[:, LANES:]
    low = lax.broadcasted_iota(jnp.int32, a.shape, 1) < HEAD_DIM
    z = jnp.zeros_like(a)
    return [jnp.where(low, a, z), jnp.where(low, z, sw),
            jnp.where(low, sw, z), jnp.where(low, z, a)]


def _mix_kernel(*refs, layer, tq, ctx_len, has_local, final):
    it = iter(refs)
    sink_ref = next(it)
    x_ref, gate_ref, ya_ref, q_ref, zb_ref, ga_ref, gb_ref = (next(it) for _ in range(7))
    if has_local:
        k_ref, kp_ref, kn_ref, v_ref, vp_ref, vn_ref = (next(it) for _ in range(6))
    kc_ref, vc_ref = next(it), next(it)
    wa_ref, wb_ref, wo_ref = (next(it) for _ in range(3))
    if final:
        fw_ref = next(it)
    o_ref = next(it)
    y_s, yb_s, k_s, vm_s, s_s, p_s, z_s = (next(it) for _ in range(7))
    if has_local:
        bias_s = next(it)

    i = pl.program_id(1)
    n_i = pl.num_programs(1)
    nblk = tq // BLOCK
    n_ct = kc_ref.shape[0] // ctx_len
    assert has_local <= (n_ct == 1)

    def seq_of(j):
        return j // (nblk // n_ct)
    n_loc = 3 * BLOCK if has_local else 0
    nk = ctx_len + n_loc
    n_slot = vm_s.shape[0]
    rows_a = SLABS_PER_KV * BLOCK

    @pl.when(i == 0)
    def _():
        kvar = _kv_variants(kc_ref[...])
        vvar = _kv_variants(vc_ref[...])
        low = lax.broadcasted_iota(jnp.int32, (nk, LANES), 1) < HEAD_DIM
        ones_on = [jnp.where(low, 1.0, 0.0).astype(bf16), jnp.where(low, 0.0, 1.0).astype(bf16)]
        for n in range(4):
            k_s[n, 0:n_ct * ctx_len, :] = kvar[n]
        for slot in range(n_slot):
            c0 = 0 if has_local else slot * ctx_len
            for kh in range(N_KV_HEADS):
                for e in range(HEADS_PER_SLAB):
                    vm_s[slot, kh, e * nk:e * nk + ctx_len, 0:LANES] = (
                        vvar[2 * kh + e][c0:c0 + ctx_len])
                    vm_s[slot, kh, e * nk:(e + 1) * nk, LANES:] = ones_on[e]

    def local_block(ref_prev, ref_tile, ref_next, t):
        if t == 0:
            return ref_prev[...]
        if t == nblk + 1:
            return ref_next[...]
        return ref_tile[(t - 1) * BLOCK:t * BLOCK, :]

    def fill_values(j, b):
        vvar = _kv_variants(local_block(vp_ref, v_ref, vn_ref, j + b))
        for kh in range(N_KV_HEADS):
            for e in range(HEADS_PER_SLAB):
                r = e * nk + ctx_len + b * BLOCK
                vm_s[j, kh, r:r + BLOCK, 0:LANES] = vvar[2 * kh + e]

    def fill_keys(t):
        for n, a in enumerate(_kv_variants(local_block(kp_ref, k_ref, kn_ref, t))):
            k_s[n, ctx_len + t * BLOCK:ctx_len + (t + 1) * BLOCK, :] = a

    def fill_bias():
        qi = lax.broadcasted_iota(jnp.int32, (rows_a, BLOCK), 0) % BLOCK
        ci = lax.broadcasted_iota(jnp.int32, (rows_a, BLOCK), 1)
        tri_prev = jnp.where(ci >= qi, 0.0, NEG)
        tri_next = jnp.where(ci <= qi, 0.0, NEG)
        bias_s[0] = tri_prev + jnp.where(i == 0, NEG, 0.0)
        bias_s[1] = tri_prev
        bias_s[2] = tri_next
        bias_s[3] = tri_next + jnp.where(i == n_i - 1, NEG, 0.0)

    if has_local:
        for t in range(nblk + 2):
            fill_keys(t)
        fill_bias()

    nt = (((1,), (1,)), ((), ()))

    def stage_a(j, kh, e):
        u = (j * N_KV_HEADS + kh) * HEADS_PER_SLAB + e
        rows = slice(j * BLOCK, (j + 1) * BLOCK)
        q2 = jnp.concatenate(
            [q_ref[rows, (kh * SLABS_PER_KV + s) * LANES:(kh * SLABS_PER_KV + s + 1) * LANES]
             for s in range(SLABS_PER_KV)], axis=0)
        var = 2 * kh + e
        kc0 = seq_of(j) * ctx_len
        s_s[u, :, 0:ctx_len] = lax.dot_general(
            q2, k_s[var, kc0:kc0 + ctx_len, :], nt, preferred_element_type=f32)
        if has_local:
            r0 = ctx_len + j * BLOCK
            s_loc = lax.dot_general(q2, k_s[var, r0:r0 + n_loc, :], nt,
                                    preferred_element_type=f32)
            c0 = ctx_len
            s_s[u, :, c0:c0 + BLOCK] = s_loc[:, :BLOCK] + bias_s[0 if j == 0 else 1]
            s_s[u, :, c0 + BLOCK:c0 + 2 * BLOCK] = s_loc[:, BLOCK:2 * BLOCK]
            s_s[u, :, c0 + 2 * BLOCK:] = s_loc[:, 2 * BLOCK:] + bias_s[3 if j == nblk - 1 else 2]

    low_half = lax.broadcasted_iota(jnp.int32, (BLOCK, LANES), 1) < HEAD_DIM

    def stage_b(j, kh):
        jk = j * N_KV_HEADS + kh
        for s in range(SLABS_PER_KV):
            rs = slice(s * BLOCK, (s + 1) * BLOCK)
            z = []
            for e in range(HEADS_PER_SLAB):
                u = jk * HEADS_PER_SLAB + e
                sk = sink_ref[layer, (kh * SLABS_PER_KV + s) * HEADS_PER_SLAB + e] * LOG2E
                mx = s_s[u, rs, 0:LANES]
                for n in range(1, nk // LANES):
                    mx = jnp.maximum(mx, s_s[u, rs, n * LANES:(n + 1) * LANES])
                m = jnp.maximum(jnp.max(mx, axis=1, keepdims=True), sk)
                for n in range(nk // LANES):
                    col = e * nk + n * LANES
                    p_s[jk, rs, col:col + LANES] = jnp.exp2(
                        s_s[u, rs, n * LANES:(n + 1) * LANES] - m).astype(bf16)
                z.append(jnp.exp2(sk - m))
            z_s[jk, rs, :] = jnp.where(low_half, z[0], z[1])

    def stage_c(j, kh):
        jk = j * N_KV_HEADS + kh
        res = jnp.dot(p_s[jk], vm_s[j if has_local else seq_of(j), kh],
                      preferred_element_type=f32)
        attn = res[:, :LANES] / (res[:, LANES:] + z_s[jk])
        rows = slice(j * BLOCK, (j + 1) * BLOCK)
        for s in range(SLABS_PER_KV):
            c = kh * SLABS_PER_KV + s
            lanes = slice(c * LANES, (c + 1) * LANES)
            yb_s[rows, lanes] = (attn[s * BLOCK:(s + 1) * BLOCK]
                                 * zb_ref[rows, lanes].astype(f32)).astype(bf16)

    for t in range(nblk + 2):
        for kh in range(N_KV_HEADS):
            if has_local and 0 <= t - 1 < nblk:
                for b in range(3)[kh::N_KV_HEADS]:
                    fill_values(t - 1, b)
            if 0 <= t - 1 < nblk:
                stage_b(t - 1, kh)
            if 0 <= t - 2 < nblk:
                stage_c(t - 2, kh)
            if t < nblk:
                for e in range(HEADS_PER_SLAB):
                    stage_a(t, kh, e)

    n_part = 4
    width = D_MODEL // n_part
    for n in range(n_part):
        cols = slice(n * width, (n + 1) * width)
        y_s[:, cols] = (
            jnp.dot(ya_ref[...], wa_ref[:, cols], preferred_element_type=f32)
            * ga_ref[:, cols].astype(f32)
            + jnp.dot(yb_s[...], wb_ref[:, cols], preferred_element_type=f32)
            * gb_ref[:, cols].astype(f32)).astype(bf16)
    y16 = y_s[...]
    if final:
        xn = x_ref[...] + gate_ref[...] * jnp.dot(y16, wo_ref[...], preferred_element_type=f32)
        ms = jnp.mean(xn * xn, axis=-1, keepdims=True)
        o_ref[...] = xn * lax.rsqrt(ms + EPS) * fw_ref[...]
    else:
        for n in range(n_part):
            cols = slice(n * width, (n + 1) * width)
            o_ref[:, cols] = x_ref[:, cols] + gate_ref[:, cols] * jnp.dot(
                y16, wo_ref[:, cols], preferred_element_type=f32)


def _mixer(x2, layer, mod, sink, proj_out, kc, vc, wa, wb, wo, final_w, *,
           seq, tq, ctx_len, has_local, gate_row0, gate_per_batch):
    ya, q, k, v, zb, ga, gb = proj_out
    n_rows = x2.shape[0]
    seqs_per_tile = max(tq // seq, 1)
    assert kc.shape[0] * seq == n_rows * ctx_len and not (has_local and seqs_per_tile > 1)
    n_tile_b = n_rows // (seq * seqs_per_tile)
    n_i = max(seq // tq, 1)
    nblk = tq // BLOCK
    final = final_w is not None

    def row(w):
        return pl.BlockSpec((tq, w), lambda b, i: (b * n_i + i, 0))

    def halo(rows_blk, w, side):
        per_tile = tq // rows_blk
        per_seq = seq // rows_blk
        if side < 0:
            return pl.BlockSpec((rows_blk, w), lambda b, i: (
                b * per_seq + jnp.maximum(i * per_tile - 1, 0), 0))
        return pl.BlockSpec((rows_blk, w), lambda b, i: (
            b * per_seq + jnp.minimum((i + 1) * per_tile, per_seq - 1), 0))

    def whole(shape):
        return pl.BlockSpec(shape, lambda b, i: (0,) * len(shape))

    if gate_per_batch:
        gate_spec = _mod_spec(layer, lambda b, i: gate_row0 + b, 2)
    else:
        gate_spec = _mod_spec(layer, lambda b, i: gate_row0, 2)

    kvw = 2 * KV_WIDTH
    in_specs = [pl.BlockSpec(memory_space=pltpu.SMEM),
                row(D_MODEL), gate_spec, row(CONV_WIDTH),
                row(ATTN_WIDTH), row(ATTN_WIDTH), row(D_MODEL), row(D_MODEL)]
    args = [sink, x2, mod, ya, q, zb, ga, gb]
    if has_local:
        in_specs += [row(kvw), halo(BLOCK, kvw, -1), halo(BLOCK, kvw, +1)] * 2
        args += [k, k, k, v, v, v]
    ctx_spec = pl.BlockSpec((seqs_per_tile * ctx_len, kvw), lambda b, i: (b, 0))
    in_specs += [ctx_spec, ctx_spec, _layer_spec(layer, (CONV_WIDTH, D_MODEL)),
                 _layer_spec(layer, (ATTN_WIDTH, D_MODEL)), _layer_spec(layer, (D_MODEL, D_MODEL))]
    args += [kc, vc, wa, wb, wo]
    if final:
        in_specs.append(whole((1, D_MODEL)))
        args.append(final_w.reshape(1, D_MODEL))

    nk = ctx_len + (3 * BLOCK if has_local else 0)
    k_rows = ctx_len + tq + 2 * BLOCK if has_local else seqs_per_tile * ctx_len
    n_slot = nblk if has_local else seqs_per_tile
    rows_a = SLABS_PER_KV * BLOCK
    scratch = [pltpu.VMEM((tq, D_MODEL), bf16),
               pltpu.VMEM((tq, ATTN_WIDTH), bf16),
               pltpu.VMEM((2 * N_KV_HEADS, k_rows, LANES), bf16),
               pltpu.VMEM((n_slot, N_KV_HEADS, HEADS_PER_SLAB * nk, 2 * LANES), bf16),
               pltpu.VMEM((nblk * N_KV_HEADS * HEADS_PER_SLAB, rows_a, nk), f32),
               pltpu.VMEM((nblk * N_KV_HEADS, rows_a, HEADS_PER_SLAB * nk), bf16),
               pltpu.VMEM((nblk * N_KV_HEADS, rows_a, LANES), f32)]
    if has_local:
        scratch += [pltpu.VMEM((4, rows_a, BLOCK), f32)]

    return pl.pallas_call(
        functools.partial(_mix_kernel, layer=layer, tq=tq, ctx_len=ctx_len,
                          has_local=has_local, final=final),
        grid=(n_tile_b, n_i),
        in_specs=in_specs,
        out_specs=row(D_MODEL),
        out_shape=jax.ShapeDtypeStruct((n_rows, D_MODEL), f32),
        scratch_shapes=scratch,
        compiler_params=pltpu.CompilerParams(
            dimension_semantics=("arbitrary", "arbitrary"), vmem_limit_bytes=VMEM_LIMIT),
        name="mixer_latent" if has_local else "mixer_context",
    )(*args)


def _rope_tables(seq):
    n_rows = seq // GRID_W
    inv_freq = jnp.asarray(ROPE_THETA, f32) ** (-jnp.arange(ROPE_FREQS, dtype=f32) / ROPE_FREQS)
    lane = np.arange(LANES)
    lane_freq = inv_freq[lane % ROPE_FREQS][None, :]
    use_col = ((lane % HEAD_DIM) >= HEAD_DIM // 2)[None, None, :]
    upper = ((lane % (2 * ROPE_FREQS)) >= ROPE_FREQS)[None, None, :]
    ang_r = jnp.arange(n_rows, dtype=f32)[:, None] * lane_freq
    ang_c = jnp.arange(GRID_W, dtype=f32)[:, None] * lane_freq
    shape = (n_rows, GRID_W, LANES)
    cos = jnp.where(use_col, jnp.cos(ang_c)[None], jnp.cos(ang_r)[:, None]).reshape(seq, LANES)
    sin = jnp.broadcast_to(jnp.where(use_col, jnp.sin(ang_c)[None], jnp.sin(ang_r)[:, None]),
                           shape)
    zero = jnp.zeros(shape, f32)
    sin_lo = jnp.where(upper, zero, -sin).reshape(seq, LANES)
    sin_hi = jnp.where(upper, sin, zero).reshape(seq, LANES)
    return cos, sin_lo, sin_hi


def kernel(x, c, ctx, c_ctx, norm_w, w_mod, b_mod, w_in, conv_w, w_a_out, w_b_out,
           attn_sink, w_o, final_norm_w):
    batch, seq, _ = x.shape
    ctx_len = ctx.shape[1]
    depth = w_in.shape[0]
    assert batch + 1 <= MOD_ROWS and seq % 512 == 0 and ctx_len % BLOCK == 0

    c_all = jnp.zeros((MOD_ROWS, D_MODEL), f32).at[:batch].set(c).at[batch].set(c_ctx)
    mod = _modulation(c_all, w_mod, b_mod).reshape(depth, MOD_ROWS, 3, 1, D_MODEL)
    rope_tabs = _rope_tables(seq)

    w_in_b = w_in.astype(bf16)
    wa_b, wb_b, wo_b = w_a_out.astype(bf16), w_b_out.astype(bf16), w_o.astype(bf16)

    x2 = x.reshape(batch * seq, D_MODEL)
    c2 = ctx.reshape(batch * ctx_len, D_MODEL)
    for l in range(depth):
        last = l == depth - 1
        pc = _projection(c2, l, norm_w, mod, w_in_b, conv_w, None,
                         subs=(ctx_len,) * 4, seq=ctx_len, mod_row0=batch, mod_per_seq=False)
        px = _projection(x2, l, norm_w, mod, w_in_b, conv_w, rope_tabs,
                         subs=(512, 512), seq=seq, mod_row0=0, mod_per_seq=True)
        kc, vc = pc[2], pc[3]
        x2 = _mixer(x2, l, mod, attn_sink, px, kc, vc, wa_b, wb_b, wo_b,
                    final_norm_w if last else None,
                    seq=seq, tq=512, ctx_len=ctx_len, has_local=True, gate_row0=0,
                    gate_per_batch=True)
        if not last:
            c2 = _mixer(c2, l, mod, attn_sink, pc, kc, vc, wa_b, wb_b, wo_b, None,
                        seq=ctx_len, tq=4 * ctx_len, ctx_len=ctx_len, has_local=False,
                        gate_row0=batch, gate_per_batch=False)
    return x2.reshape(batch, seq, D_MODEL)
```

```python
import functools
import math

import numpy as np
import jax
import jax.numpy as jnp
from jax import lax
from jax.experimental import pallas as pl
from jax.experimental.pallas import tpu as pltpu

D_MODEL = 1024
GRID_W = 64
CONV_WIDTH = 512
CONV_K = 3
N_HEADS = 8
N_KV_HEADS = 2
HEAD_DIM = 64
ATTN_WIDTH = N_HEADS * HEAD_DIM
KV_WIDTH = N_KV_HEADS * HEAD_DIM
BLOCK = 128
ROPE_THETA = 10000.0
ROPE_FREQS = HEAD_DIM // 4
EPS = 1e-6
NEG = -1e30
LOG2E = math.log2(math.e)

LANES = 128
SUBLANES = 8
MOD_ROWS = 24
VMEM_LIMIT = 56 * 1024 * 1024
N_SLAB = ATTN_WIDTH // LANES
HEADS_PER_SLAB = LANES // HEAD_DIM
SLABS_PER_KV = N_SLAB // N_KV_HEADS

_OFF_B = 0
_OFF_C = _OFF_B + CONV_WIDTH
_OFF_U = _OFF_C + CONV_WIDTH
_OFF_ZA = _OFF_U + CONV_WIDTH
_OFF_Q = _OFF_ZA + CONV_WIDTH
_OFF_K = _OFF_Q + ATTN_WIDTH
_OFF_V = _OFF_K + KV_WIDTH
_OFF_ZB = _OFF_V + KV_WIDTH
_OFF_GA = _OFF_ZB + ATTN_WIDTH
_OFF_GB = _OFF_GA + D_MODEL
IN_COLS = _OFF_GB + D_MODEL

f32 = jnp.float32
bf16 = jnp.bfloat16


def _sigmoid(x):
    return 0.5 * jnp.tanh(0.5 * x) + 0.5


def _silu(x):
    return x * _sigmoid(x)


def _mod_kernel(c_ref, w_ref, b_ref, o_ref):
    s = _silu(c_ref[...])
    o_ref[0] = jnp.dot(s, w_ref[0], preferred_element_type=f32,
                       precision=lax.Precision.HIGHEST) + b_ref[0]


def _modulation(c_all, w_mod, b_mod):
    depth = w_mod.shape[0]
    n_col = 3 * D_MODEL // D_MODEL
    return pl.pallas_call(
        _mod_kernel,
        grid=(depth, n_col),
        in_specs=[
            pl.BlockSpec((MOD_ROWS, D_MODEL), lambda l, j: (0, 0)),
            pl.BlockSpec((1, D_MODEL, D_MODEL), lambda l, j: (l, 0, j)),
            pl.BlockSpec((1, 1, D_MODEL), lambda l, j: (l, 0, j)),
        ],
        out_specs=pl.BlockSpec((1, MOD_ROWS, D_MODEL), lambda l, j: (l, 0, j)),
        out_shape=jax.ShapeDtypeStruct((depth, MOD_ROWS, 3 * D_MODEL), f32),
        name="modulation",
    )(c_all, w_mod, b_mod.reshape(depth, 1, 3 * D_MODEL))


def _rope(p, cos, sin_lo, sin_hi):
    outs = []
    for j in range(p.shape[1] // LANES):
        x = p[:, j * LANES:(j + 1) * LANES]
        outs.append(x * cos
                    + pltpu.roll(x, LANES - ROPE_FREQS, 1) * sin_lo
                    + pltpu.roll(x, ROPE_FREQS, 1) * sin_hi)
    return outs[0] if len(outs) == 1 else jnp.concatenate(outs, axis=1)


def _with_swapped_halves(a):
    return jnp.concatenate([a, pltpu.roll(a, HEAD_DIM, 1)], axis=1)


def _norm_mod(x, gain, shift):
    ms = jnp.mean(x * x, axis=-1, keepdims=True)
    return (x * lax.rsqrt(ms + EPS) * gain + shift).astype(bf16)


def _proj_kernel(*refs, rope, seq, subs):
    it = iter(refs)
    x_ref, xp_ref, xn_ref, nw_ref, shift_ref, scale_ref, w_ref, cw_ref = (
        next(it) for _ in range(8))
    if rope:
        cos_ref, slo_ref, shi_ref = (next(it) for _ in range(3))
    ya_ref, q_ref, k_ref, v_ref, zb_ref, ga_ref, gb_ref = (next(it) for _ in range(7))
    h_s, cu_s = next(it), next(it)
    tm = x_ref.shape[0]
    assert sum(subs) == tm and (seq % tm == 0 or tm % seq == 0)
    halo = 2 * SUBLANES
    gain, shift = nw_ref[...] * (1.0 + scale_ref[...]), shift_ref[...]
    tiles_per_seq = max(seq // tm, 1)
    t_in_seq = pl.program_id(0) % tiles_per_seq
    cw = cw_ref[...]
    zero = jnp.zeros((SUBLANES, CONV_WIDTH), f32)

    def buffers(n):
        s0 = sum(subs[:n]) + n * halo
        return h_s.at[s0:s0 + subs[n] + halo], cu_s.at[s0:s0 + subs[n] + halo]

    for n, sub in enumerate(subs):
        r0 = sum(subs[:n])
        rows = slice(r0, r0 + sub)
        h, cu_b = buffers(n)
        before = xp_ref[...] if r0 == 0 else x_ref[r0 - SUBLANES:r0, :]
        after = xn_ref[...] if r0 + sub == tm else x_ref[r0 + sub:r0 + sub + SUBLANES, :]
        h[0:sub, :] = _norm_mod(x_ref[rows, :], gain, shift)
        h[sub:sub + halo, :] = _norm_mod(jnp.concatenate([before, after], axis=0), gain, shift)

        def proj(off, width, n_rows=sub):
            return jnp.dot(h[0:n_rows, :], w_ref[:, off:off + width],
                           preferred_element_type=f32)

        cu = proj(_OFF_C, CONV_WIDTH, sub + halo) * proj(_OFF_U, CONV_WIDTH, sub + halo)
        cu_before, cu_after = cu[sub:sub + SUBLANES], cu[sub + SUBLANES:]
        if r0 % seq == 0:
            cu_before = jnp.where(t_in_seq > 0, cu_before, zero) if r0 == 0 else zero
        if (r0 + sub) % seq == 0 or r0 + sub == tm:
            cu_after = (jnp.where(t_in_seq < tiles_per_seq - 1, cu_after, zero)
                        if r0 + sub == tm else zero)
        cu_b[0:SUBLANES, :] = cu_before
        cu_b[SUBLANES:SUBLANES + sub, :] = cu[0:sub]
        cu_b[SUBLANES + sub:, :] = cu_after
        conv = (cu_b[SUBLANES - 1:SUBLANES - 1 + sub, :] * cw[0:1]
                + cu_b[SUBLANES:SUBLANES + sub, :] * cw[1:2]
                + cu_b[SUBLANES + 1:SUBLANES + 1 + sub, :] * cw[2:3])
        ya_ref[rows, :] = (proj(_OFF_B, CONV_WIDTH) * conv
                           * _silu(proj(_OFF_ZA, CONV_WIDTH))).astype(bf16)

        q = proj(_OFF_Q, ATTN_WIDTH)
        kv = proj(_OFF_K, 2 * KV_WIDTH)
        k = kv[:, :KV_WIDTH]
        if rope:
            cos, slo, shi = cos_ref[rows, :], slo_ref[rows, :], shi_ref[rows, :]
            q = _rope(q, cos, slo, shi)
            k = _rope(k, cos, slo, shi)
        q_ref[rows, :] = (q * (HEAD_DIM ** -0.5 * LOG2E)).astype(bf16)
        k_ref[rows, :] = _with_swapped_halves(k).astype(bf16)
        v_ref[rows, :] = _with_swapped_halves(kv[:, KV_WIDTH:]).astype(bf16)
        zb_ref[rows, :] = _silu(proj(_OFF_ZB, ATTN_WIDTH)).astype(bf16)
        part = 2 * LANES
        for c in range(D_MODEL // part):
            ga_ref[rows, c * part:(c + 1) * part] = _sigmoid(
                proj(_OFF_GA + c * part, part)).astype(bf16)
        for c in range(D_MODEL // part):
            gb_ref[rows, c * part:(c + 1) * part] = _sigmoid(
                proj(_OFF_GB + c * part, part)).astype(bf16)


def _mod_spec(layer, row_of, which):
    return pl.BlockSpec((None, None, None, 1, D_MODEL),
                        lambda *g: (layer, row_of(*g), which, 0, 0))


def _layer_spec(layer, shape):
    return pl.BlockSpec((None,) + tuple(shape), lambda *g: (layer,) + (0,) * len(shape))


def _projection(x2, layer, norm_w, mod, w_in, conv_w, rope_tabs, *, subs, seq, mod_row0,
                mod_per_seq):
    tm = sum(subs)
    scratch_rows = tm + len(subs) * 2 * SUBLANES
    n_rows = x2.shape[0]
    assert seq % tm == 0 or (tm % seq == 0 and not mod_per_seq and rope_tabs is None)
    tiles_per_seq = max(seq // tm, 1)
    halo_per_tile = tm // SUBLANES
    n_halo = n_rows // SUBLANES
    if mod_per_seq:
        mod_row = lambda t: mod_row0 + t // tiles_per_seq
    else:
        mod_row = lambda t: mod_row0
    row = lambda w: pl.BlockSpec((tm, w), lambda t: (t, 0))
    in_specs = [
        row(D_MODEL),
        pl.BlockSpec((SUBLANES, D_MODEL),
                     lambda t: (jnp.maximum(t * halo_per_tile - 1, 0), 0)),
        pl.BlockSpec((SUBLANES, D_MODEL),
                     lambda t: (jnp.minimum((t + 1) * halo_per_tile, n_halo - 1), 0)),
        _layer_spec(layer, (1, D_MODEL)),
        _mod_spec(layer, mod_row, 0),
        _mod_spec(layer, mod_row, 1),
        _layer_spec(layer, (D_MODEL, IN_COLS)),
        _layer_spec(layer, (CONV_K, CONV_WIDTH)),
    ]
    args = [x2, x2, x2, norm_w.reshape(-1, 1, D_MODEL), mod, mod, w_in, conv_w]
    if rope_tabs is not None:
        in_specs += [pl.BlockSpec((tm, LANES), lambda t: (t % tiles_per_seq, 0))] * 3
        args += list(rope_tabs)
    widths = (CONV_WIDTH, ATTN_WIDTH, 2 * KV_WIDTH, 2 * KV_WIDTH, ATTN_WIDTH, D_MODEL, D_MODEL)
    return pl.pallas_call(
        functools.partial(_proj_kernel, rope=rope_tabs is not None,
                          seq=seq, subs=tuple(subs)),
        grid=(n_rows // tm,),
        in_specs=in_specs,
        out_specs=[row(w) for w in widths],
        out_shape=[jax.ShapeDtypeStruct((n_rows, w), bf16) for w in widths],
        scratch_shapes=[pltpu.VMEM((scratch_rows, D_MODEL), bf16),
                        pltpu.VMEM((scratch_rows, CONV_WIDTH), f32)],
        compiler_params=pltpu.CompilerParams(
            dimension_semantics=("arbitrary",), vmem_limit_bytes=VMEM_LIMIT),
        name="projection_rope" if rope_tabs is not None else "projection",
    )(*args)


def _kv_variants(a2):
    a, sw = a2[:, :LANES], a2[:, LANES:]
    low = lax.broadcasted_iota(jnp.int32, a.shape, 1) < HEAD_DIM
    z = jnp.zeros_like(a)
    return [jnp.where(low, a, z), jnp.where(low, z, sw),
            jnp.where(low, sw, z), jnp.where(low, z, a)]


def _mix_kernel(*refs, layer, tq, ctx_len, has_local, final):
    it = iter(refs)
    sink_ref = next(it)
    x_ref, gate_ref, ya_ref, q_ref, zb_ref, ga_ref, gb_ref = (next(it) for _ in range(7))
    if has_local:
        k_ref, kp_ref, kn_ref, v_ref, vp_ref, vn_ref = (next(it) for _ in range(6))
    kc_ref, vc_ref = next(it), next(it)
    wa_ref, wb_ref, wo_ref = (next(it) for _ in range(3))
    if final:
        fw_ref = next(it)
    o_ref = next(it)
    y_s, ya_s, yb_s, k_s, vm_s, s_s, p_s, z_s = (next(it) for _ in range(8))
    if has_local:
        bias_s = next(it)

    i = pl.program_id(1)
    n_i = pl.num_programs(1)
    nblk = tq // BLOCK
    n_ct = kc_ref.shape[0] // ctx_len
    assert has_local <= (n_ct == 1)

    def seq_of(j):
        return j // (nblk // n_ct)
    n_loc = 3 * BLOCK if has_local else 0
    nk = ctx_len + n_loc
    n_slot = vm_s.shape[0]
    rows_a = SLABS_PER_KV * BLOCK

    @pl.when(i == 0)
    def _():
        kvar = _kv_variants(kc_ref[...])
        vvar = _kv_variants(vc_ref[...])
        low = lax.broadcasted_iota(jnp.int32, (nk, LANES), 1) < HEAD_DIM
        ones_on = [jnp.where(low, 1.0, 0.0).astype(bf16), jnp.where(low, 0.0, 1.0).astype(bf16)]
        for n in range(4):
            k_s[n, 0:n_ct * ctx_len, :] = kvar[n]
        for slot in range(n_slot):
            c0 = 0 if has_local else slot * ctx_len
            for kh in range(N_KV_HEADS):
                for e in range(HEADS_PER_SLAB):
                    vm_s[slot, kh, e * nk:e * nk + ctx_len, 0:LANES] = (
                        vvar[2 * kh + e][c0:c0 + ctx_len])
                    vm_s[slot, kh, e * nk:(e + 1) * nk, LANES:] = ones_on[e]

    def local_block(ref_prev, ref_tile, ref_next, t):
        if t == 0:
            return ref_prev[...]
        if t == nblk + 1:
            return ref_next[...]
        return ref_tile[(t - 1) * BLOCK:t * BLOCK, :]

    def fill_values(j, b):
        vvar = _kv_variants(local_block(vp_ref, v_ref, vn_ref, j + b))
        for kh in range(N_KV_HEADS):
            for e in range(HEADS_PER_SLAB):
                r = e * nk + ctx_len + b * BLOCK
                vm_s[j, kh, r:r + BLOCK, 0:LANES] = vvar[2 * kh + e]

    def fill_keys(t):
        for n, a in enumerate(_kv_variants(local_block(kp_ref, k_ref, kn_ref, t))):
            k_s[n, ctx_len + t * BLOCK:ctx_len + (t + 1) * BLOCK, :] = a

    def fill_bias():
        qi = lax.broadcasted_iota(jnp.int32, (rows_a, BLOCK), 0) % BLOCK
        ci = lax.broadcasted_iota(jnp.int32, (rows_a, BLOCK), 1)
        tri_prev = jnp.where(ci >= qi, 0.0, NEG)
        tri_next = jnp.where(ci <= qi, 0.0, NEG)
        bias_s[0] = tri_prev + jnp.where(i == 0, NEG, 0.0)
        bias_s[1] = tri_prev
        bias_s[2] = tri_next
        bias_s[3] = tri_next + jnp.where(i == n_i - 1, NEG, 0.0)

    if has_local:
        for t in range(nblk + 2):
            fill_keys(t)
        fill_bias()

    nt = (((1,), (1,)), ((), ()))

    def stage_a(j, kh, e):
        u = (j * N_KV_HEADS + kh) * HEADS_PER_SLAB + e
        rows = slice(j * BLOCK, (j + 1) * BLOCK)
        q2 = jnp.concatenate(
            [q_ref[rows, (kh * SLABS_PER_KV + s) * LANES:(kh * SLABS_PER_KV + s + 1) * LANES]
             for s in range(SLABS_PER_KV)], axis=0)
        var = 2 * kh + e
        kc0 = seq_of(j) * ctx_len
        s_s[u, :, 0:ctx_len] = lax.dot_general(
            q2, k_s[var, kc0:kc0 + ctx_len, :], nt, preferred_element_type=f32)
        if has_local:
            r0 = ctx_len + j * BLOCK
            s_loc = lax.dot_general(q2, k_s[var, r0:r0 + n_loc, :], nt,
                                    preferred_element_type=f32)
            c0 = ctx_len
            s_s[u, :, c0:c0 + BLOCK] = s_loc[:, :BLOCK] + bias_s[0 if j == 0 else 1]
            s_s[u, :, c0 + BLOCK:c0 + 2 * BLOCK] = s_loc[:, BLOCK:2 * BLOCK]
            s_s[u, :, c0 + 2 * BLOCK:] = s_loc[:, 2 * BLOCK:] + bias_s[3 if j == nblk - 1 else 2]

    low_half = lax.broadcasted_iota(jnp.int32, (BLOCK, LANES), 1) < HEAD_DIM

    def stage_b(j, kh):
        jk = j * N_KV_HEADS + kh
        for s in range(SLABS_PER_KV):
            rs = slice(s * BLOCK, (s + 1) * BLOCK)
            z = []
            for e in range(HEADS_PER_SLAB):
                u = jk * HEADS_PER_SLAB + e
                sk = sink_ref[layer, (kh * SLABS_PER_KV + s) * HEADS_PER_SLAB + e] * LOG2E
                mx = s_s[u, rs, 0:LANES]
                for n in range(1, nk // LANES):
                    mx = jnp.maximum(mx, s_s[u, rs, n * LANES:(n + 1) * LANES])
                m = jnp.maximum(jnp.max(mx, axis=1, keepdims=True), sk)
                for n in range(nk // LANES):
                    col = e * nk + n * LANES
                    p_s[jk, rs, col:col + LANES] = jnp.exp2(
                        s_s[u, rs, n * LANES:(n + 1) * LANES] - m).astype(bf16)
                z.append(jnp.exp2(sk - m))
            z_s[jk, rs, :] = jnp.where(low_half, z[0], z[1])

    def stage_c(j, kh):
        jk = j * N_KV_HEADS + kh
        res = jnp.dot(p_s[jk], vm_s[j if has_local else seq_of(j), kh],
                      preferred_element_type=f32)
        attn = res[:, :LANES] / (res[:, LANES:] + z_s[jk])
        rows = slice(j * BLOCK, (j + 1) * BLOCK)
        for s in range(SLABS_PER_KV):
            c = kh * SLABS_PER_KV + s
            lanes = slice(c * LANES, (c + 1) * LANES)
            yb_s[rows, lanes] = (attn[s * BLOCK:(s + 1) * BLOCK]
                                 * zb_ref[rows, lanes].astype(f32)).astype(bf16)

    n_part = 4
    width = D_MODEL // n_part

    def conv_branch(n):
        cols = slice(n * width, (n + 1) * width)
        ya_s[:, cols] = (jnp.dot(ya_ref[...], wa_ref[:, cols], preferred_element_type=f32)
                         * ga_ref[:, cols].astype(f32))

    n_slots = (nblk + 2) * N_KV_HEADS
    conv_at = {(2 * n + 1) * n_slots // (2 * n_part): n for n in range(n_part)}
    for t in range(nblk + 2):
        for kh in range(N_KV_HEADS):
            if t * N_KV_HEADS + kh in conv_at:
                conv_branch(conv_at[t * N_KV_HEADS + kh])
            if has_local and 0 <= t - 1 < nblk:
                for b in range(3)[kh::N_KV_HEADS]:
                    fill_values(t - 1, b)
            if 0 <= t - 1 < nblk:
                stage_b(t - 1, kh)
            if 0 <= t - 2 < nblk:
                stage_c(t - 2, kh)
            if t < nblk:
                for e in range(HEADS_PER_SLAB):
                    stage_a(t, kh, e)

    for n in range(n_part):
        cols = slice(n * width, (n + 1) * width)
        y_s[:, cols] = (
            ya_s[:, cols]
            + jnp.dot(yb_s[...], wb_ref[:, cols], preferred_element_type=f32)
            * gb_ref[:, cols].astype(f32)).astype(bf16)
    y16 = y_s[...]
    if final:
        xn = x_ref[...] + gate_ref[...] * jnp.dot(y16, wo_ref[...], preferred_element_type=f32)
        ms = jnp.mean(xn * xn, axis=-1, keepdims=True)
        o_ref[...] = xn * lax.rsqrt(ms + EPS) * fw_ref[...]
    else:
        for n in range(n_part):
            cols = slice(n * width, (n + 1) * width)
            o_ref[:, cols] = x_ref[:, cols] + gate_ref[:, cols] * jnp.dot(
                y16, wo_ref[:, cols], preferred_element_type=f32)


def _mixer(x2, layer, mod, sink, proj_out, kc, vc, wa, wb, wo, final_w, *,
           seq, tq, ctx_len, has_local, gate_row0, gate_per_batch):
    ya, q, k, v, zb, ga, gb = proj_out
    n_rows = x2.shape[0]
    seqs_per_tile = max(tq // seq, 1)
    assert kc.shape[0] * seq == n_rows * ctx_len and not (has_local and seqs_per_tile > 1)
    n_tile_b = n_rows // (seq * seqs_per_tile)
    n_i = max(seq // tq, 1)
    nblk = tq // BLOCK
    final = final_w is not None

    def row(w):
        return pl.BlockSpec((tq, w), lambda b, i: (b * n_i + i, 0))

    def halo(rows_blk, w, side):
        per_tile = tq // rows_blk
        per_seq = seq // rows_blk
        if side < 0:
            return pl.BlockSpec((rows_blk, w), lambda b, i: (
                b * per_seq + jnp.maximum(i * per_tile - 1, 0), 0))
        return pl.BlockSpec((rows_blk, w), lambda b, i: (
            b * per_seq + jnp.minimum((i + 1) * per_tile, per_seq - 1), 0))

    def whole(shape):
        return pl.BlockSpec(shape, lambda b, i: (0,) * len(shape))

    if gate_per_batch:
        gate_spec = _mod_spec(layer, lambda b, i: gate_row0 + b, 2)
    else:
        gate_spec = _mod_spec(layer, lambda b, i: gate_row0, 2)

    kvw = 2 * KV_WIDTH
    in_specs = [pl.BlockSpec(memory_space=pltpu.SMEM),
                row(D_MODEL), gate_spec, row(CONV_WIDTH),
                row(ATTN_WIDTH), row(ATTN_WIDTH), row(D_MODEL), row(D_MODEL)]
    args = [sink, x2, mod, ya, q, zb, ga, gb]
    if has_local:
        in_specs += [row(kvw), halo(BLOCK, kvw, -1), halo(BLOCK, kvw, +1)] * 2
        args += [k, k, k, v, v, v]
    ctx_spec = pl.BlockSpec((seqs_per_tile * ctx_len, kvw), lambda b, i: (b, 0))
    in_specs += [ctx_spec, ctx_spec, _layer_spec(layer, (CONV_WIDTH, D_MODEL)),
                 _layer_spec(layer, (ATTN_WIDTH, D_MODEL)), _layer_spec(layer, (D_MODEL, D_MODEL))]
    args += [kc, vc, wa, wb, wo]
    if final:
        in_specs.append(whole((1, D_MODEL)))
        args.append(final_w.reshape(1, D_MODEL))

    nk = ctx_len + (3 * BLOCK if has_local else 0)
    k_rows = ctx_len + tq + 2 * BLOCK if has_local else seqs_per_tile * ctx_len
    n_slot = nblk if has_local else seqs_per_tile
    rows_a = SLABS_PER_KV * BLOCK
    scratch = [pltpu.VMEM((tq, D_MODEL), bf16),
               pltpu.VMEM((tq, D_MODEL), f32),
               pltpu.VMEM((tq, ATTN_WIDTH), bf16),
               pltpu.VMEM((2 * N_KV_HEADS, k_rows, LANES), bf16),
               pltpu.VMEM((n_slot, N_KV_HEADS, HEADS_PER_SLAB * nk, 2 * LANES), bf16),
               pltpu.VMEM((nblk * N_KV_HEADS * HEADS_PER_SLAB, rows_a, nk), f32),
               pltpu.VMEM((nblk * N_KV_HEADS, rows_a, HEADS_PER_SLAB * nk), bf16),
               pltpu.VMEM((nblk * N_KV_HEADS, rows_a, LANES), f32)]
    if has_local:
        scratch += [pltpu.VMEM((4, rows_a, BLOCK), f32)]

    return pl.pallas_call(
        functools.partial(_mix_kernel, layer=layer, tq=tq, ctx_len=ctx_len,
                          has_local=has_local, final=final),
        grid=(n_tile_b, n_i),
        in_specs=in_specs,
        out_specs=row(D_MODEL),
        out_shape=jax.ShapeDtypeStruct((n_rows, D_MODEL), f32),
        scratch_shapes=scratch,
        compiler_params=pltpu.CompilerParams(
            dimension_semantics=("arbitrary", "arbitrary"), vmem_limit_bytes=VMEM_LIMIT),
        name="mixer_latent" if has_local else "mixer_context",
    )(*args)


def _rope_tables(seq):
    n_rows = seq // GRID_W
    inv_freq = jnp.asarray(ROPE_THETA, f32) ** (-jnp.arange(ROPE_FREQS, dtype=f32) / ROPE_FREQS)
    lane = np.arange(LANES)
    lane_freq = inv_freq[lane % ROPE_FREQS][None, :]
    use_col = ((lane % HEAD_DIM) >= HEAD_DIM // 2)[None, None, :]
    upper = ((lane % (2 * ROPE_FREQS)) >= ROPE_FREQS)[None, None, :]
    ang_r = jnp.arange(n_rows, dtype=f32)[:, None] * lane_freq
    ang_c = jnp.arange(GRID_W, dtype=f32)[:, None] * lane_freq
    shape = (n_rows, GRID_W, LANES)
    cos = jnp.where(use_col, jnp.cos(ang_c)[None], jnp.cos(ang_r)[:, None]).reshape(seq, LANES)
    sin = jnp.broadcast_to(jnp.where(use_col, jnp.sin(ang_c)[None], jnp.sin(ang_r)[:, None]),
                           shape)
    zero = jnp.zeros(shape, f32)
    sin_lo = jnp.where(upper, zero, -sin).reshape(seq, LANES)
    sin_hi = jnp.where(upper, sin, zero).reshape(seq, LANES)
    return cos, sin_lo, sin_hi


def kernel(x, c, ctx, c_ctx, norm_w, w_mod, b_mod, w_in, conv_w, w_a_out, w_b_out,
           attn_sink, w_o, final_norm_w):
    batch, seq, _ = x.shape
    ctx_len = ctx.shape[1]
    depth = w_in.shape[0]
    assert batch + 1 <= MOD_ROWS and seq % 512 == 0 and ctx_len % BLOCK == 0

    c_all = jnp.zeros((MOD_ROWS, D_MODEL), f32).at[:batch].set(c).at[batch].set(c_ctx)
    mod = _modulation(c_all, w_mod, b_mod).reshape(depth, MOD_ROWS, 3, 1, D_MODEL)
    rope_tabs = _rope_tables(seq)

    w_in_b = w_in.astype(bf16)
    wa_b, wb_b, wo_b = w_a_out.astype(bf16), w_b_out.astype(bf16), w_o.astype(bf16)

    x2 = x.reshape(batch * seq, D_MODEL)
    c2 = ctx.reshape(batch * ctx_len, D_MODEL)
    for l in range(depth):
        last = l == depth - 1
        pc = _projection(c2, l, norm_w, mod, w_in_b, conv_w, None,
                         subs=(ctx_len,) * 4, seq=ctx_len, mod_row0=batch, mod_per_seq=False)
        px = _projection(x2, l, norm_w, mod, w_in_b, conv_w, rope_tabs,
                         subs=(512, 512), seq=seq, mod_row0=0, mod_per_seq=True)
        kc, vc = pc[2], pc[3]
        x2 = _mixer(x2, l, mod, attn_sink, px, kc, vc, wa_b, wb_b, wo_b,
                    final_norm_w if last else None,
                    seq=seq, tq=512, ctx_len=ctx_len, has_local=True, gate_row0=0,
                    gate_per_batch=True)
        if not last:
            c2 = _mixer(c2, l, mod, attn_sink, pc, kc, vc, wa_b, wb_b, wo_b, None,
                        seq=ctx_len, tq=ctx_len, ctx_len=ctx_len, has_local=False,
                        gate_row0=batch, gate_per_batch=False)
    return x2.reshape(batch, seq, D_MODEL)
```

```python
import functools
import math

import numpy as np
import jax
import jax.numpy as jnp
from jax import lax
from jax.experimental import pallas as pl
from jax.experimental.pallas import tpu as pltpu

D_MODEL = 1024
GRID_W = 64
CONV_WIDTH = 512
CONV_K = 3
N_HEADS = 8
N_KV_HEADS = 2
HEAD_DIM = 64
ATTN_WIDTH = N_HEADS * HEAD_DIM
KV_WIDTH = N_KV_HEADS * HEAD_DIM
BLOCK = 128
ROPE_THETA = 10000.0
ROPE_FREQS = HEAD_DIM // 4
EPS = 1e-6
NEG = -1e30
LOG2E = math.log2(math.e)

LANES = 128
SUBLANES = 8
MOD_ROWS = 24
VMEM_LIMIT = 56 * 1024 * 1024
N_SLAB = ATTN_WIDTH // LANES
HEADS_PER_SLAB = LANES // HEAD_DIM
SLABS_PER_KV = N_SLAB // N_KV_HEADS

_OFF_B = 0
_OFF_C = _OFF_B + CONV_WIDTH
_OFF_U = _OFF_C + CONV_WIDTH
_OFF_ZA = _OFF_U + CONV_WIDTH
_OFF_Q = _OFF_ZA + CONV_WIDTH
_OFF_K = _OFF_Q + ATTN_WIDTH
_OFF_V = _OFF_K + KV_WIDTH
_OFF_ZB = _OFF_V + KV_WIDTH
_OFF_GA = _OFF_ZB + ATTN_WIDTH
_OFF_GB = _OFF_GA + D_MODEL
IN_COLS = _OFF_GB + D_MODEL

f32 = jnp.float32
bf16 = jnp.bfloat16


def _sigmoid(x):
    return 0.5 * jnp.tanh(0.5 * x) + 0.5


def _silu(x):
    return x * _sigmoid(x)


def _mod_kernel(c_ref, w_ref, b_ref, o_ref):
    s = _silu(c_ref[...]).astype(bf16)
    res = jnp.dot(s, w_ref[...].astype(bf16), preferred_element_type=f32) + b_ref[...]
    for r in range(MOD_ROWS):
        o_ref[r] = res[r:r + 1, :]


def _modulation(c_all, w_mod, b_mod):
    depth = w_mod.shape[0]
    return pl.pallas_call(
        _mod_kernel,
        grid=(depth, 3),
        in_specs=[
            pl.BlockSpec((MOD_ROWS, D_MODEL), lambda l, j: (0, 0)),
            pl.BlockSpec((None, D_MODEL, D_MODEL), lambda l, j: (l, 0, j)),
            pl.BlockSpec((None, 1, D_MODEL), lambda l, j: (l, 0, j)),
        ],
        out_specs=pl.BlockSpec((None, MOD_ROWS, None, 1, D_MODEL), lambda l, j: (l, 0, j, 0, 0)),
        out_shape=jax.ShapeDtypeStruct((depth, MOD_ROWS, 3, 1, D_MODEL), f32),
        name="modulation",
    )(c_all, w_mod, b_mod.reshape(depth, 1, 3 * D_MODEL))


def _rope(p, cos, sin_lo, sin_hi):
    outs = []
    for j in range(p.shape[1] // LANES):
        x = p[:, j * LANES:(j + 1) * LANES]
        outs.append(x * cos
                    + pltpu.roll(x, LANES - ROPE_FREQS, 1) * sin_lo
                    + pltpu.roll(x, ROPE_FREQS, 1) * sin_hi)
    return outs[0] if len(outs) == 1 else jnp.concatenate(outs, axis=1)


def _with_swapped_halves(a):
    return jnp.concatenate([a, pltpu.roll(a, HEAD_DIM, 1)], axis=1)


def _norm_mod(x, gain, shift):
    ms = jnp.mean(x * x, axis=-1, keepdims=True)
    return (x * lax.rsqrt(ms + EPS) * gain + shift).astype(bf16)


def _proj_kernel(*refs, rope, seq, subs):
    it = iter(refs)
    x_ref, xp_ref, xn_ref, nw_ref, shift_ref, scale_ref, w_ref, cw_ref = (
        next(it) for _ in range(8))
    if rope:
        cos_ref, slo_ref, shi_ref = (next(it) for _ in range(3))
    ya_ref, q_ref, k_ref, v_ref, zb_ref, ga_ref, gb_ref = (next(it) for _ in range(7))
    h_s, cu_s = next(it), next(it)
    tm = x_ref.shape[0]
    assert sum(subs) == tm and (seq % tm == 0 or tm % seq == 0)
    halo = 2 * SUBLANES
    gain, shift = nw_ref[...] * (1.0 + scale_ref[...]), shift_ref[...]
    tiles_per_seq = max(seq // tm, 1)
    t_in_seq = pl.program_id(0) % tiles_per_seq
    cw = cw_ref[...]
    zero = jnp.zeros((SUBLANES, CONV_WIDTH), f32)

    def buffers(n):
        s0 = sum(subs[:n]) + n * halo
        return h_s.at[s0:s0 + subs[n] + halo], cu_s.at[s0:s0 + subs[n] + halo]

    for n, sub in enumerate(subs):
        r0 = sum(subs[:n])
        rows = slice(r0, r0 + sub)
        h, cu_b = buffers(n)
        before = xp_ref[...] if r0 == 0 else x_ref[r0 - SUBLANES:r0, :]
        after = xn_ref[...] if r0 + sub == tm else x_ref[r0 + sub:r0 + sub + SUBLANES, :]
        h[0:sub, :] = _norm_mod(x_ref[rows, :], gain, shift)
        h[sub:sub + halo, :] = _norm_mod(jnp.concatenate([before, after], axis=0), gain, shift)

        def proj(off, width, n_rows=sub):
            return jnp.dot(h[0:n_rows, :], w_ref[:, off:off + width],
                           preferred_element_type=f32)

        cu = proj(_OFF_C, CONV_WIDTH, sub + halo) * proj(_OFF_U, CONV_WIDTH, sub + halo)
        cu_before, cu_after = cu[sub:sub + SUBLANES], cu[sub + SUBLANES:]
        if r0 % seq == 0:
            cu_before = jnp.where(t_in_seq > 0, cu_before, zero) if r0 == 0 else zero
        if (r0 + sub) % seq == 0 or r0 + sub == tm:
            cu_after = (jnp.where(t_in_seq < tiles_per_seq - 1, cu_after, zero)
                        if r0 + sub == tm else zero)
        cu_b[0:SUBLANES, :] = cu_before
        cu_b[SUBLANES:SUBLANES + sub, :] = cu[0:sub]
        cu_b[SUBLANES + sub:, :] = cu_after
        conv = (cu_b[SUBLANES - 1:SUBLANES - 1 + sub, :] * cw[0:1]
                + cu_b[SUBLANES:SUBLANES + sub, :] * cw[1:2]
                + cu_b[SUBLANES + 1:SUBLANES + 1 + sub, :] * cw[2:3])
        ya_ref[rows, :] = (proj(_OFF_B, CONV_WIDTH) * conv
                           * _silu(proj(_OFF_ZA, CONV_WIDTH))).astype(bf16)

        q = proj(_OFF_Q, ATTN_WIDTH)
        kv = proj(_OFF_K, 2 * KV_WIDTH)
        k = kv[:, :KV_WIDTH]
        if rope:
            cos, slo, shi = cos_ref[rows, :], slo_ref[rows, :], shi_ref[rows, :]
            q = _rope(q, cos, slo, shi)
            k = _rope(k, cos, slo, shi)
        q_ref[rows, :] = (q * (HEAD_DIM ** -0.5 * LOG2E)).astype(bf16)
        k_ref[rows, :] = _with_swapped_halves(k).astype(bf16)
        v_ref[rows, :] = _with_swapped_halves(kv[:, KV_WIDTH:]).astype(bf16)
        zb_ref[rows, :] = _silu(proj(_OFF_ZB, ATTN_WIDTH)).astype(bf16)
        part = 2 * LANES
        for c in range(D_MODEL // part):
            ga_ref[rows, c * part:(c + 1) * part] = _sigmoid(
                proj(_OFF_GA + c * part, part)).astype(bf16)
        for c in range(D_MODEL // part):
            gb_ref[rows, c * part:(c + 1) * part] = _sigmoid(
                proj(_OFF_GB + c * part, part)).astype(bf16)


def _mod_spec(layer, row_of, which):
    return pl.BlockSpec((None, None, None, 1, D_MODEL),
                        lambda *g: (layer, row_of(*g), which, 0, 0))


def _layer_spec(layer, shape):
    return pl.BlockSpec((None,) + tuple(shape), lambda *g: (layer,) + (0,) * len(shape))


def _projection(x2, layer, norm_w, mod, w_in, conv_w, rope_tabs, *, subs, seq, mod_row0,
                mod_per_seq):
    tm = sum(subs)
    scratch_rows = tm + len(subs) * 2 * SUBLANES
    n_rows = x2.shape[0]
    assert seq % tm == 0 or (tm % seq == 0 and not mod_per_seq and rope_tabs is None)
    tiles_per_seq = max(seq // tm, 1)
    halo_per_tile = tm // SUBLANES
    n_halo = n_rows // SUBLANES
    if mod_per_seq:
        mod_row = lambda t: mod_row0 + t // tiles_per_seq
    else:
        mod_row = lambda t: mod_row0
    row = lambda w: pl.BlockSpec((tm, w), lambda t: (t, 0))
    in_specs = [
        row(D_MODEL),
        pl.BlockSpec((SUBLANES, D_MODEL),
                     lambda t: (jnp.maximum(t * halo_per_tile - 1, 0), 0)),
        pl.BlockSpec((SUBLANES, D_MODEL),
                     lambda t: (jnp.minimum((t + 1) * halo_per_tile, n_halo - 1), 0)),
        _layer_spec(layer, (1, D_MODEL)),
        _mod_spec(layer, mod_row, 0),
        _mod_spec(layer, mod_row, 1),
        _layer_spec(layer, (D_MODEL, IN_COLS)),
        _layer_spec(layer, (CONV_K, CONV_WIDTH)),
    ]
    args = [x2, x2, x2, norm_w.reshape(-1, 1, D_MODEL), mod, mod, w_in, conv_w]
    if rope_tabs is not None:
        in_specs += [pl.BlockSpec((tm, LANES), lambda t: (t % tiles_per_seq, 0))] * 3
        args += list(rope_tabs)
    widths = (CONV_WIDTH, ATTN_WIDTH, 2 * KV_WIDTH, 2 * KV_WIDTH, ATTN_WIDTH, D_MODEL, D_MODEL)
    return pl.pallas_call(
        functools.partial(_proj_kernel, rope=rope_tabs is not None,
                          seq=seq, subs=tuple(subs)),
        grid=(n_rows // tm,),
        in_specs=in_specs,
        out_specs=[row(w) for w in widths],
        out_shape=[jax.ShapeDtypeStruct((n_rows, w), bf16) for w in widths],
        scratch_shapes=[pltpu.VMEM((scratch_rows, D_MODEL), bf16),
                        pltpu.VMEM((scratch_rows, CONV_WIDTH), f32)],
        compiler_params=pltpu.CompilerParams(
            dimension_semantics=("arbitrary",), vmem_limit_bytes=VMEM_LIMIT),
        name="projection_rope" if rope_tabs is not None else "projection",
    )(*args)


def _kv_variants(a2):
    a, sw = a2[:, :LANES], a2[:, LANES:]
    low = lax.broadcasted_iota(jnp.int32, a.shape, 1) < HEAD_DIM
    z = jnp.zeros_like(a)
    return [jnp.where(low, a, z), jnp.where(low, z, sw),
            jnp.where(low, sw, z), jnp.where(low, z, a)]


def _mix_kernel(*refs, layer, tq, ctx_len, has_local, final):
    it = iter(refs)
    sink_ref = next(it)
    x_ref, gate_ref, ya_ref, q_ref, zb_ref, ga_ref, gb_ref = (next(it) for _ in range(7))
    if has_local:
        k_ref, kp_ref, kn_ref, v_ref, vp_ref, vn_ref = (next(it) for _ in range(6))
    kc_ref, vc_ref = next(it), next(it)
    wa_ref, wb_ref, wo_ref = (next(it) for _ in range(3))
    if final:
        fw_ref = next(it)
    o_ref = next(it)
    y_s, ya_s, yb_s, k_s, vm_s, s_s, p_s, z_s = (next(it) for _ in range(8))
    if has_local:
        bias_s = next(it)

    i = pl.program_id(1)
    n_i = pl.num_programs(1)
    nblk = tq // BLOCK
    n_ct = kc_ref.shape[0] // ctx_len
    assert has_local <= (n_ct == 1)

    def seq_of(j):
        return j // (nblk // n_ct)
    n_loc = 3 * BLOCK if has_local else 0
    nk = ctx_len + n_loc
    n_slot = vm_s.shape[0]
    rows_a = SLABS_PER_KV * BLOCK

    @pl.when(i == 0)
    def _():
        kvar = _kv_variants(kc_ref[...])
        vvar = _kv_variants(vc_ref[...])
        low = lax.broadcasted_iota(jnp.int32, (nk, LANES), 1) < HEAD_DIM
        ones_on = [jnp.where(low, 1.0, 0.0).astype(bf16), jnp.where(low, 0.0, 1.0).astype(bf16)]
        for n in range(4):
            k_s[n, 0:n_ct * ctx_len, :] = kvar[n]
        for slot in range(n_slot):
            c0 = 0 if has_local else slot * ctx_len
            for kh in range(N_KV_HEADS):
                for e in range(HEADS_PER_SLAB):
                    vm_s[slot, kh, e * nk:e * nk + ctx_len, 0:LANES] = (
                        vvar[2 * kh + e][c0:c0 + ctx_len])
                    vm_s[slot, kh, e * nk:(e + 1) * nk, LANES:] = ones_on[e]

    def local_block(ref_prev, ref_tile, ref_next, t):
        if t == 0:
            return ref_prev[...]
        if t == nblk + 1:
            return ref_next[...]
        return ref_tile[(t - 1) * BLOCK:t * BLOCK, :]

    def fill_values(j, b):
        vvar = _kv_variants(local_block(vp_ref, v_ref, vn_ref, j + b))
        for kh in range(N_KV_HEADS):
            for e in range(HEADS_PER_SLAB):
                r = e * nk + ctx_len + b * BLOCK
                vm_s[j, kh, r:r + BLOCK, 0:LANES] = vvar[2 * kh + e]

    def fill_keys(t):
        for n, a in enumerate(_kv_variants(local_block(kp_ref, k_ref, kn_ref, t))):
            k_s[n, ctx_len + t * BLOCK:ctx_len + (t + 1) * BLOCK, :] = a

    def fill_bias():
        qi = lax.broadcasted_iota(jnp.int32, (rows_a, BLOCK), 0) % BLOCK
        ci = lax.broadcasted_iota(jnp.int32, (rows_a, BLOCK), 1)
        tri_prev = jnp.where(ci >= qi, 0.0, NEG)
        tri_next = jnp.where(ci <= qi, 0.0, NEG)
        bias_s[0] = tri_prev + jnp.where(i == 0, NEG, 0.0)
        bias_s[1] = tri_prev
        bias_s[2] = tri_next
        bias_s[3] = tri_next + jnp.where(i == n_i - 1, NEG, 0.0)

    if has_local:
        for t in range(nblk + 2):
            fill_keys(t)
        fill_bias()

    nt = (((1,), (1,)), ((), ()))

    def stage_a(j, kh, e):
        u = (j * N_KV_HEADS + kh) * HEADS_PER_SLAB + e
        rows = slice(j * BLOCK, (j + 1) * BLOCK)
        q2 = jnp.concatenate(
            [q_ref[rows, (kh * SLABS_PER_KV + s) * LANES:(kh * SLABS_PER_KV + s + 1) * LANES]
             for s in range(SLABS_PER_KV)], axis=0)
        var = 2 * kh + e
        kc0 = seq_of(j) * ctx_len
        s_s[u, :, 0:ctx_len] = lax.dot_general(
            q2, k_s[var, kc0:kc0 + ctx_len, :], nt, preferred_element_type=f32)
        if has_local:
            r0 = ctx_len + j * BLOCK
            s_loc = lax.dot_general(q2, k_s[var, r0:r0 + n_loc, :], nt,
                                    preferred_element_type=f32)
            c0 = ctx_len
            s_s[u, :, c0:c0 + BLOCK] = s_loc[:, :BLOCK] + bias_s[0 if j == 0 else 1]
            s_s[u, :, c0 + BLOCK:c0 + 2 * BLOCK] = s_loc[:, BLOCK:2 * BLOCK]
            s_s[u, :, c0 + 2 * BLOCK:] = s_loc[:, 2 * BLOCK:] + bias_s[3 if j == nblk - 1 else 2]

    low_half = lax.broadcasted_iota(jnp.int32, (BLOCK, LANES), 1) < HEAD_DIM

    def stage_b(j, kh):
        jk = j * N_KV_HEADS + kh
        for s in range(SLABS_PER_KV):
            rs = slice(s * BLOCK, (s + 1) * BLOCK)
            z = []
            for e in range(HEADS_PER_SLAB):
                u = jk * HEADS_PER_SLAB + e
                sk = sink_ref[layer, (kh * SLABS_PER_KV + s) * HEADS_PER_SLAB + e] * LOG2E
                mx = s_s[u, rs, 0:LANES]
                for n in range(1, nk // LANES):
                    mx = jnp.maximum(mx, s_s[u, rs, n * LANES:(n + 1) * LANES])
                m = jnp.maximum(jnp.max(mx, axis=1, keepdims=True), sk)
                for n in range(nk // LANES):
                    col = e * nk + n * LANES
                    p_s[jk, rs, col:col + LANES] = jnp.exp2(
                        s_s[u, rs, n * LANES:(n + 1) * LANES] - m).astype(bf16)
                z.append(jnp.exp2(sk - m))
            z_s[jk, rs, :] = jnp.where(low_half, z[0], z[1])

    def stage_c(j, kh):
        jk = j * N_KV_HEADS + kh
        res = jnp.dot(p_s[jk], vm_s[j if has_local else seq_of(j), kh],
                      preferred_element_type=f32)
        attn = res[:, :LANES] / (res[:, LANES:] + z_s[jk])
        rows = slice(j * BLOCK, (j + 1) * BLOCK)
        for s in range(SLABS_PER_KV):
            c = kh * SLABS_PER_KV + s
            lanes = slice(c * LANES, (c + 1) * LANES)
            yb_s[rows, lanes] = (attn[s * BLOCK:(s + 1) * BLOCK]
                                 * zb_ref[rows, lanes].astype(f32)).astype(bf16)

    n_part = 4
    width = D_MODEL // n_part

    def conv_branch(n):
        cols = slice(n * width, (n + 1) * width)
        ya_s[:, cols] = (jnp.dot(ya_ref[...], wa_ref[:, cols], preferred_element_type=f32)
                         * ga_ref[:, cols].astype(f32))

    n_slots = (nblk + 2) * N_KV_HEADS
    conv_at = {(2 * n + 1) * n_slots // (2 * n_part): n for n in range(n_part)}
    for t in range(nblk + 2):
        for kh in range(N_KV_HEADS):
            if t * N_KV_HEADS + kh in conv_at:
                conv_branch(conv_at[t * N_KV_HEADS + kh])
            if has_local and 0 <= t - 1 < nblk:
                for b in range(3)[kh::N_KV_HEADS]:
                    fill_values(t - 1, b)
            if 0 <= t - 1 < nblk:
                stage_b(t - 1, kh)
            if 0 <= t - 2 < nblk:
                stage_c(t - 2, kh)
            if t < nblk:
                for e in range(HEADS_PER_SLAB):
                    stage_a(t, kh, e)

    for n in range(n_part):
        cols = slice(n * width, (n + 1) * width)
        y_s[:, cols] = (
            ya_s[:, cols]
            + jnp.dot(yb_s[...], wb_ref[:, cols], preferred_element_type=f32)
            * gb_ref[:, cols].astype(f32)).astype(bf16)
    y16 = y_s[...]
    if final:
        xn = x_ref[...] + gate_ref[...] * jnp.dot(y16, wo_ref[...], preferred_element_type=f32)
        ms = jnp.mean(xn * xn, axis=-1, keepdims=True)
        o_ref[...] = xn * lax.rsqrt(ms + EPS) * fw_ref[...]
    else:
        for n in range(n_part):
            cols = slice(n * width, (n + 1) * width)
            o_ref[:, cols] = x_ref[:, cols] + gate_ref[:, cols] * jnp.dot(
                y16, wo_ref[:, cols], preferred_element_type=f32)


def _mixer(x2, layer, mod, sink, proj_out, kc, vc, wa, wb, wo, final_w, *,
           seq, tq, ctx_len, has_local, gate_row0, gate_per_batch):
    ya, q, k, v, zb, ga, gb = proj_out
    n_rows = x2.shape[0]
    seqs_per_tile = max(tq // seq, 1)
    assert kc.shape[0] * seq == n_rows * ctx_len and not (has_local and seqs_per_tile > 1)
    n_tile_b = n_rows // (seq * seqs_per_tile)
    n_i = max(seq // tq, 1)
    nblk = tq // BLOCK
    final = final_w is not None

    def row(w):
        return pl.BlockSpec((tq, w), lambda b, i: (b * n_i + i, 0))

    def halo(rows_blk, w, side):
        per_tile = tq // rows_blk
        per_seq = seq // rows_blk
        if side < 0:
            return pl.BlockSpec((rows_blk, w), lambda b, i: (
                b * per_seq + jnp.maximum(i * per_tile - 1, 0), 0))
        return pl.BlockSpec((rows_blk, w), lambda b, i: (
            b * per_seq + jnp.minimum((i + 1) * per_tile, per_seq - 1), 0))

    def whole(shape):
        return pl.BlockSpec(shape, lambda b, i: (0,) * len(shape))

    if gate_per_batch:
        gate_spec = _mod_spec(layer, lambda b, i: gate_row0 + b, 2)
    else:
        gate_spec = _mod_spec(layer, lambda b, i: gate_row0, 2)

    kvw = 2 * KV_WIDTH
    in_specs = [pl.BlockSpec(memory_space=pltpu.SMEM),
                row(D_MODEL), gate_spec, row(CONV_WIDTH),
                row(ATTN_WIDTH), row(ATTN_WIDTH), row(D_MODEL), row(D_MODEL)]
    args = [sink, x2, mod, ya, q, zb, ga, gb]
    if has_local:
        in_specs += [row(kvw), halo(BLOCK, kvw, -1), halo(BLOCK, kvw, +1)] * 2
        args += [k, k, k, v, v, v]
    ctx_spec = pl.BlockSpec((seqs_per_tile * ctx_len, kvw), lambda b, i: (b, 0))
    in_specs += [ctx_spec, ctx_spec, _layer_spec(layer, (CONV_WIDTH, D_MODEL)),
                 _layer_spec(layer, (ATTN_WIDTH, D_MODEL)), _layer_spec(layer, (D_MODEL, D_MODEL))]
    args += [kc, vc, wa, wb, wo]
    if final:
        in_specs.append(whole((1, D_MODEL)))
        args.append(final_w.reshape(1, D_MODEL))

    nk = ctx_len + (3 * BLOCK if has_local else 0)
    k_rows = ctx_len + tq + 2 * BLOCK if has_local else seqs_per_tile * ctx_len
    n_slot = nblk if has_local else seqs_per_tile
    rows_a = SLABS_PER_KV * BLOCK
    scratch = [pltpu.VMEM((tq, D_MODEL), bf16),
               pltpu.VMEM((tq, D_MODEL), f32),
               pltpu.VMEM((tq, ATTN_WIDTH), bf16),
               pltpu.VMEM((2 * N_KV_HEADS, k_rows, LANES), bf16),
               pltpu.VMEM((n_slot, N_KV_HEADS, HEADS_PER_SLAB * nk, 2 * LANES), bf16),
               pltpu.VMEM((nblk * N_KV_HEADS * HEADS_PER_SLAB, rows_a, nk), f32),
               pltpu.VMEM((nblk * N_KV_HEADS, rows_a, HEADS_PER_SLAB * nk), bf16),
               pltpu.VMEM((nblk * N_KV_HEADS, rows_a, LANES), f32)]
    if has_local:
        scratch += [pltpu.VMEM((4, rows_a, BLOCK), f32)]

    return pl.pallas_call(
        functools.partial(_mix_kernel, layer=layer, tq=tq, ctx_len=ctx_len,
                          has_local=has_local, final=final),
        grid=(n_tile_b, n_i),
        in_specs=in_specs,
        out_specs=row(D_MODEL),
        out_shape=jax.ShapeDtypeStruct((n_rows, D_MODEL), f32),
        scratch_shapes=scratch,
        compiler_params=pltpu.CompilerParams(
            dimension_semantics=("arbitrary", "arbitrary"), vmem_limit_bytes=VMEM_LIMIT),
        name="mixer_latent" if has_local else "mixer_context",
    )(*args)


def _rope_tables(seq):
    n_rows = seq // GRID_W
    inv_freq = jnp.asarray(ROPE_THETA, f32) ** (-jnp.arange(ROPE_FREQS, dtype=f32) / ROPE_FREQS)
    lane = np.arange(LANES)
    lane_freq = inv_freq[lane % ROPE_FREQS][None, :]
    use_col = ((lane % HEAD_DIM) >= HEAD_DIM // 2)[None, None, :]
    upper = ((lane % (2 * ROPE_FREQS)) >= ROPE_FREQS)[None, None, :]
    ang_r = jnp.arange(n_rows, dtype=f32)[:, None] * lane_freq
    ang_c = jnp.arange(GRID_W, dtype=f32)[:, None] * lane_freq
    shape = (n_rows, GRID_W, LANES)
    cos = jnp.where(use_col, jnp.cos(ang_c)[None], jnp.cos(ang_r)[:, None]).reshape(seq, LANES)
    sin = jnp.broadcast_to(jnp.where(use_col, jnp.sin(ang_c)[None], jnp.sin(ang_r)[:, None]),
                           shape)
    zero = jnp.zeros(shape, f32)
    sin_lo = jnp.where(upper, zero, -sin).reshape(seq, LANES)
    sin_hi = jnp.where(upper, sin, zero).reshape(seq, LANES)
    return cos, sin_lo, sin_hi


def kernel(x, c, ctx, c_ctx, norm_w, w_mod, b_mod, w_in, conv_w, w_a_out, w_b_out,
           attn_sink, w_o, final_norm_w):
    batch, seq, _ = x.shape
    ctx_len = ctx.shape[1]
    depth = w_in.shape[0]
    assert batch + 1 <= MOD_ROWS and seq % 512 == 0 and ctx_len % BLOCK == 0

    c_all = jnp.zeros((MOD_ROWS, D_MODEL), f32).at[:batch].set(c).at[batch].set(c_ctx)
    mod = _modulation(c_all, w_mod, b_mod)
    rope_tabs = _rope_tables(seq)

    w_in_b = w_in.astype(bf16)
    wa_b, wb_b, wo_b = w_a_out.astype(bf16), w_b_out.astype(bf16), w_o.astype(bf16)

    x2 = x.reshape(batch * seq, D_MODEL)
    c2 = ctx.reshape(batch * ctx_len, D_MODEL)
    for l in range(depth):
        last = l == depth - 1
        pc = _projection(c2, l, norm_w, mod, w_in_b, conv_w, None,
                         subs=(ctx_len,) * 4, seq=ctx_len, mod_row0=batch, mod_per_seq=False)
        px = _projection(x2, l, norm_w, mod, w_in_b, conv_w, rope_tabs,
                         subs=(512, 512), seq=seq, mod_row0=0, mod_per_seq=True)
        kc, vc = pc[2], pc[3]
        x2 = _mixer(x2, l, mod, attn_sink, px, kc, vc, wa_b, wb_b, wo_b,
                    final_norm_w if last else None,
                    seq=seq, tq=512, ctx_len=ctx_len, has_local=True, gate_row0=0,
                    gate_per_batch=True)
        if not last:
            c2 = _mixer(c2, l, mod, attn_sink, pc, kc, vc, wa_b, wb_b, wo_b, None,
                        seq=ctx_len, tq=ctx_len, ctx_len=ctx_len, has_local=False,
                        gate_row0=batch, gate_per_batch=False)
    return x2.reshape(batch, seq, D_MODEL)
```

```python
import functools
import math

import numpy as np
import jax
import jax.numpy as jnp
from jax import lax
from jax.experimental import pallas as pl
from jax.experimental.pallas import tpu as pltpu

D_MODEL = 1024
GRID_W = 64
CONV_WIDTH = 512
CONV_K = 3
N_HEADS = 8
N_KV_HEADS = 2
HEAD_DIM = 64
ATTN_WIDTH = N_HEADS * HEAD_DIM
KV_WIDTH = N_KV_HEADS * HEAD_DIM
BLOCK = 128
ROPE_THETA = 10000.0
ROPE_FREQS = HEAD_DIM // 4
EPS = 1e-6
NEG = -1e30
LOG2E = math.log2(math.e)

LANES = 128
SUBLANES = 8
MOD_ROWS = 24
VMEM_LIMIT = 56 * 1024 * 1024
N_SLAB = ATTN_WIDTH // LANES
HEADS_PER_SLAB = LANES // HEAD_DIM
SLABS_PER_KV = N_SLAB // N_KV_HEADS

_OFF_B = 0
_OFF_C = _OFF_B + CONV_WIDTH
_OFF_U = _OFF_C + CONV_WIDTH
_OFF_ZA = _OFF_U + CONV_WIDTH
_OFF_Q = _OFF_ZA + CONV_WIDTH
_OFF_K = _OFF_Q + ATTN_WIDTH
_OFF_V = _OFF_K + KV_WIDTH
_OFF_ZB = _OFF_V + KV_WIDTH
_OFF_GA = _OFF_ZB + ATTN_WIDTH
_OFF_GB = _OFF_GA + D_MODEL
IN_COLS = _OFF_GB + D_MODEL

f32 = jnp.float32
bf16 = jnp.bfloat16


def _sigmoid(x):
    return 0.5 * jnp.tanh(0.5 * x) + 0.5


def _silu(x):
    return x * _sigmoid(x)


def _mod_kernel(c_ref, w_ref, b_ref, o_ref):
    s = _silu(c_ref[...]).astype(bf16)
    res = jnp.dot(s, w_ref[...].astype(bf16), preferred_element_type=f32) + b_ref[...]
    for r in range(MOD_ROWS):
        o_ref[r] = res[r:r + 1, :]


def _modulation(c_all, w_mod, b_mod):
    depth = w_mod.shape[0]
    return pl.pallas_call(
        _mod_kernel,
        grid=(depth, 3),
        in_specs=[
            pl.BlockSpec((MOD_ROWS, D_MODEL), lambda l, j: (0, 0)),
            pl.BlockSpec((None, D_MODEL, D_MODEL), lambda l, j: (l, 0, j)),
            pl.BlockSpec((None, 1, D_MODEL), lambda l, j: (l, 0, j)),
        ],
        out_specs=pl.BlockSpec((None, MOD_ROWS, None, 1, D_MODEL), lambda l, j: (l, 0, j, 0, 0)),
        out_shape=jax.ShapeDtypeStruct((depth, MOD_ROWS, 3, 1, D_MODEL), f32),
        name="modulation",
    )(c_all, w_mod, b_mod.reshape(depth, 1, 3 * D_MODEL))


def _rope(p, cos, sin_lo, sin_hi):
    outs = []
    for j in range(p.shape[1] // LANES):
        x = p[:, j * LANES:(j + 1) * LANES]
        outs.append(x * cos
                    + pltpu.roll(x, LANES - ROPE_FREQS, 1) * sin_lo
                    + pltpu.roll(x, ROPE_FREQS, 1) * sin_hi)
    return outs[0] if len(outs) == 1 else jnp.concatenate(outs, axis=1)


def _with_swapped_halves(a):
    return jnp.concatenate([a, pltpu.roll(a, HEAD_DIM, 1)], axis=1)


def _norm_mod(x, gain, shift):
    ms = jnp.mean(x * x, axis=-1, keepdims=True)
    return (x * lax.rsqrt(ms + EPS) * gain + shift).astype(bf16)


def _proj_kernel(*refs, rope, seq, subs):
    it = iter(refs)
    x_ref, xp_ref, xn_ref, nw_ref, shift_ref, scale_ref, w_ref, cw_ref = (
        next(it) for _ in range(8))
    if rope:
        cos_ref, slo_ref, shi_ref = (next(it) for _ in range(3))
    ya_ref, q_ref, k_ref, v_ref, zb_ref, ga_ref, gb_ref = (next(it) for _ in range(7))
    h_s, cu_s = next(it), next(it)
    tm = x_ref.shape[0]
    assert sum(subs) == tm and (seq % tm == 0 or tm % seq == 0)
    halo = 2 * SUBLANES
    gain, shift = nw_ref[...] * (1.0 + scale_ref[...]), shift_ref[...]
    tiles_per_seq = max(seq // tm, 1)
    t_in_seq = pl.program_id(0) % tiles_per_seq
    cw = cw_ref[...]
    zero = jnp.zeros((SUBLANES, CONV_WIDTH), f32)

    def buffers(n):
        s0 = sum(subs[:n]) + n * halo
        return h_s.at[s0:s0 + subs[n] + halo], cu_s.at[s0:s0 + subs[n] + halo]

    for n, sub in enumerate(subs):
        r0 = sum(subs[:n])
        rows = slice(r0, r0 + sub)
        h, cu_b = buffers(n)
        before = xp_ref[...] if r0 == 0 else x_ref[r0 - SUBLANES:r0, :]
        after = xn_ref[...] if r0 + sub == tm else x_ref[r0 + sub:r0 + sub + SUBLANES, :]
        h[0:sub, :] = _norm_mod(x_ref[rows, :], gain, shift)
        h[sub:sub + halo, :] = _norm_mod(jnp.concatenate([before, after], axis=0), gain, shift)

        def proj(off, width, n_rows=sub):
            return jnp.dot(h[0:n_rows, :], w_ref[:, off:off + width],
                           preferred_element_type=f32)

        cu = proj(_OFF_C, CONV_WIDTH, sub + halo) * proj(_OFF_U, CONV_WIDTH, sub + halo)
        cu_before, cu_after = cu[sub:sub + SUBLANES], cu[sub + SUBLANES:]
        if r0 % seq == 0:
            cu_before = jnp.where(t_in_seq > 0, cu_before, zero) if r0 == 0 else zero
        if (r0 + sub) % seq == 0 or r0 + sub == tm:
            cu_after = (jnp.where(t_in_seq < tiles_per_seq - 1, cu_after, zero)
                        if r0 + sub == tm else zero)
        cu_b[0:SUBLANES, :] = cu_before
        cu_b[SUBLANES:SUBLANES + sub, :] = cu[0:sub]
        cu_b[SUBLANES + sub:, :] = cu_after
        conv = (cu_b[SUBLANES - 1:SUBLANES - 1 + sub, :] * cw[0:1]
                + cu_b[SUBLANES:SUBLANES + sub, :] * cw[1:2]
                + cu_b[SUBLANES + 1:SUBLANES + 1 + sub, :] * cw[2:3])
        ya_ref[rows, :] = (proj(_OFF_B, CONV_WIDTH) * conv
                           * _silu(proj(_OFF_ZA, CONV_WIDTH))).astype(bf16)

        q = proj(_OFF_Q, ATTN_WIDTH)
        kv = proj(_OFF_K, 2 * KV_WIDTH)
        k = kv[:, :KV_WIDTH]
        if rope:
            cos, slo, shi = cos_ref[rows, :], slo_ref[rows, :], shi_ref[rows, :]
            q = _rope(q, cos, slo, shi)
            k = _rope(k, cos, slo, shi)
        q_ref[rows, :] = (q * (HEAD_DIM ** -0.5 * LOG2E)).astype(bf16)
        k_ref[rows, :] = _with_swapped_halves(k).astype(bf16)
        v_ref[rows, :] = _with_swapped_halves(kv[:, KV_WIDTH:]).astype(bf16)
        zb_ref[rows, :] = _silu(proj(_OFF_ZB, ATTN_WIDTH)).astype(bf16)
        part = 2 * LANES
        for c in range(D_MODEL // part):
            ga_ref[rows, c * part:(c + 1) * part] = _sigmoid(
                proj(_OFF_GA + c * part, part)).astype(bf16)
        for c in range(D_MODEL // part):
            gb_ref[rows, c * part:(c + 1) * part] = _sigmoid(
                proj(_OFF_GB + c * part, part)).astype(bf16)


def _mod_spec(layer, row_of, which):
    return pl.BlockSpec((None, None, None, 1, D_MODEL),
                        lambda *g: (layer, row_of(*g), which, 0, 0))


def _layer_spec(layer, shape):
    return pl.BlockSpec((None,) + tuple(shape), lambda *g: (layer,) + (0,) * len(shape))


def _projection(x2, layer, norm_w, mod, w_in, conv_w, rope_tabs, *, subs, seq, mod_row0,
                mod_per_seq):
    tm = sum(subs)
    scratch_rows = tm + len(subs) * 2 * SUBLANES
    n_rows = x2.shape[0]
    assert seq % tm == 0 or (tm % seq == 0 and not mod_per_seq and rope_tabs is None)
    tiles_per_seq = max(seq // tm, 1)
    halo_per_tile = tm // SUBLANES
    n_halo = n_rows // SUBLANES
    if mod_per_seq:
        mod_row = lambda t: mod_row0 + t // tiles_per_seq
    else:
        mod_row = lambda t: mod_row0
    row = lambda w: pl.BlockSpec((tm, w), lambda t: (t, 0))
    in_specs = [
        row(D_MODEL),
        pl.BlockSpec((SUBLANES, D_MODEL),
                     lambda t: (jnp.maximum(t * halo_per_tile - 1, 0), 0)),
        pl.BlockSpec((SUBLANES, D_MODEL),
                     lambda t: (jnp.minimum((t + 1) * halo_per_tile, n_halo - 1), 0)),
        _layer_spec(layer, (1, D_MODEL)),
        _mod_spec(layer, mod_row, 0),
        _mod_spec(layer, mod_row, 1),
        _layer_spec(layer, (D_MODEL, IN_COLS)),
        _layer_spec(layer, (CONV_K, CONV_WIDTH)),
    ]
    args = [x2, x2, x2, norm_w.reshape(-1, 1, D_MODEL), mod, mod, w_in, conv_w]
    if rope_tabs is not None:
        in_specs += [pl.BlockSpec((tm, LANES), lambda t: (t % tiles_per_seq, 0))] * 3
        args += list(rope_tabs)
    widths = (CONV_WIDTH, ATTN_WIDTH, 2 * KV_WIDTH, 2 * KV_WIDTH, ATTN_WIDTH, D_MODEL, D_MODEL)
    return pl.pallas_call(
        functools.partial(_proj_kernel, rope=rope_tabs is not None,
                          seq=seq, subs=tuple(subs)),
        grid=(n_rows // tm,),
        in_specs=in_specs,
        out_specs=[row(w) for w in widths],
        out_shape=[jax.ShapeDtypeStruct((n_rows, w), bf16) for w in widths],
        scratch_shapes=[pltpu.VMEM((scratch_rows, D_MODEL), bf16),
                        pltpu.VMEM((scratch_rows, CONV_WIDTH), f32)],
        compiler_params=pltpu.CompilerParams(
            dimension_semantics=("arbitrary",), vmem_limit_bytes=VMEM_LIMIT),
        name="projection_rope" if rope_tabs is not None else "projection",
    )(*args)


def _kv_variants(a2):
    a, sw = a2[:, :LANES], a2[:, LANES:]
    low = lax.broadcasted_iota(jnp.int32, a.shape, 1) < HEAD_DIM
    z = jnp.zeros_like(a)
    return [jnp.where(low, a, z), jnp.where(low, z, sw),
            jnp.where(low, sw, z), jnp.where(low, z, a)]


def _mix_kernel(*refs, layer, tq, ctx_len, has_local, final):
    it = iter(refs)
    sink_ref = next(it)
    x_ref, gate_ref, ya_ref, q_ref, zb_ref, ga_ref, gb_ref = (next(it) for _ in range(7))
    if has_local:
        k_ref, kp_ref, kn_ref, v_ref, vp_ref, vn_ref = (next(it) for _ in range(6))
    kc_ref, vc_ref = next(it), next(it)
    wa_ref, wb_ref, wo_ref = (next(it) for _ in range(3))
    if final:
        fw_ref = next(it)
    o_ref = next(it)
    y_s, ya_s, yb_s, k_s, vm_s, s_s, p_s, z_s = (next(it) for _ in range(8))
    if has_local:
        bias_s = next(it)

    i = pl.program_id(1)
    n_i = pl.num_programs(1)
    nblk = tq // BLOCK
    n_ct = kc_ref.shape[0] // ctx_len
    assert has_local <= (n_ct == 1)

    def seq_of(j):
        return j // (nblk // n_ct)
    n_loc = 3 * BLOCK if has_local else 0
    nk = ctx_len + n_loc
    n_slot = vm_s.shape[0]
    rows_a = SLABS_PER_KV * BLOCK

    @pl.when(i == 0)
    def _():
        kvar = _kv_variants(kc_ref[...])
        vvar = _kv_variants(vc_ref[...])
        low = lax.broadcasted_iota(jnp.int32, (nk, LANES), 1) < HEAD_DIM
        ones_on = [jnp.where(low, 1.0, 0.0).astype(bf16), jnp.where(low, 0.0, 1.0).astype(bf16)]
        for n in range(4):
            k_s[n, 0:n_ct * ctx_len, :] = kvar[n]
        for slot in range(n_slot):
            c0 = 0 if has_local else slot * ctx_len
            for kh in range(N_KV_HEADS):
                for e in range(HEADS_PER_SLAB):
                    vm_s[slot, kh, e * nk:e * nk + ctx_len, 0:LANES] = (
                        vvar[2 * kh + e][c0:c0 + ctx_len])
                    vm_s[slot, kh, e * nk:(e + 1) * nk, LANES:] = ones_on[e]

    def local_block(ref_prev, ref_tile, ref_next, t):
        if t == 0:
            return ref_prev[...]
        if t == nblk + 1:
            return ref_next[...]
        return ref_tile[(t - 1) * BLOCK:t * BLOCK, :]

    def fill_values(j, b):
        vvar = _kv_variants(local_block(vp_ref, v_ref, vn_ref, j + b))
        for kh in range(N_KV_HEADS):
            for e in range(HEADS_PER_SLAB):
                r = e * nk + ctx_len + b * BLOCK
                vm_s[j, kh, r:r + BLOCK, 0:LANES] = vvar[2 * kh + e]

    def fill_keys(t):
        for n, a in enumerate(_kv_variants(local_block(kp_ref, k_ref, kn_ref, t))):
            k_s[n, ctx_len + t * BLOCK:ctx_len + (t + 1) * BLOCK, :] = a

    def fill_bias():
        qi = lax.broadcasted_iota(jnp.int32, (rows_a, BLOCK), 0) % BLOCK
        ci = lax.broadcasted_iota(jnp.int32, (rows_a, BLOCK), 1)
        tri_prev = jnp.where(ci >= qi, 0.0, NEG)
        tri_next = jnp.where(ci <= qi, 0.0, NEG)
        bias_s[0] = tri_prev + jnp.where(i == 0, NEG, 0.0)
        bias_s[1] = tri_prev
        bias_s[2] = tri_next
        bias_s[3] = tri_next + jnp.where(i == n_i - 1, NEG, 0.0)

    if has_local:
        for t in range(nblk + 2):
            fill_keys(t)
        fill_bias()

    nt = (((1,), (1,)), ((), ()))

    def stage_a(j, kh, e):
        u = (j * N_KV_HEADS + kh) * HEADS_PER_SLAB + e
        rows = slice(j * BLOCK, (j + 1) * BLOCK)
        q2 = jnp.concatenate(
            [q_ref[rows, (kh * SLABS_PER_KV + s) * LANES:(kh * SLABS_PER_KV + s + 1) * LANES]
             for s in range(SLABS_PER_KV)], axis=0)
        var = 2 * kh + e
        kc0 = seq_of(j) * ctx_len
        s_s[u, :, 0:ctx_len] = lax.dot_general(
            q2, k_s[var, kc0:kc0 + ctx_len, :], nt, preferred_element_type=f32)
        if has_local:
            r0 = ctx_len + j * BLOCK
            s_loc = lax.dot_general(q2, k_s[var, r0:r0 + n_loc, :], nt,
                                    preferred_element_type=f32)
            c0 = ctx_len
            s_s[u, :, c0:c0 + BLOCK] = s_loc[:, :BLOCK] + bias_s[0 if j == 0 else 1]
            s_s[u, :, c0 + BLOCK:c0 + 2 * BLOCK] = s_loc[:, BLOCK:2 * BLOCK]
            s_s[u, :, c0 + 2 * BLOCK:] = s_loc[:, 2 * BLOCK:] + bias_s[3 if j == nblk - 1 else 2]

    low_half = lax.broadcasted_iota(jnp.int32, (BLOCK, LANES), 1) < HEAD_DIM

    def stage_b(j, kh):
        jk = j * N_KV_HEADS + kh
        for s in range(SLABS_PER_KV):
            rs = slice(s * BLOCK, (s + 1) * BLOCK)
            z = []
            for e in range(HEADS_PER_SLAB):
                u = jk * HEADS_PER_SLAB + e
                sk = sink_ref[layer, (kh * SLABS_PER_KV + s) * HEADS_PER_SLAB + e] * LOG2E
                mx = s_s[u, rs, 0:LANES]
                for n in range(1, nk // LANES):
                    mx = jnp.maximum(mx, s_s[u, rs, n * LANES:(n + 1) * LANES])
                m = jnp.maximum(jnp.max(mx, axis=1, keepdims=True), sk)
                for n in range(nk // LANES):
                    col = e * nk + n * LANES
                    p_s[jk, rs, col:col + LANES] = jnp.exp2(
                        s_s[u, rs, n * LANES:(n + 1) * LANES] - m).astype(bf16)
                z.append(jnp.exp2(sk - m))
            z_s[jk, rs, :] = jnp.where(low_half, z[0], z[1])

    def stage_c(j, kh):
        jk = j * N_KV_HEADS + kh
        res = jnp.dot(p_s[jk], vm_s[j if has_local else seq_of(j), kh],
                      preferred_element_type=f32)
        attn = res[:, :LANES] / (res[:, LANES:] + z_s[jk])
        rows = slice(j * BLOCK, (j + 1) * BLOCK)
        for s in range(SLABS_PER_KV):
            c = kh * SLABS_PER_KV + s
            lanes = slice(c * LANES, (c + 1) * LANES)
            yb_s[rows, lanes] = (attn[s * BLOCK:(s + 1) * BLOCK]
                                 * zb_ref[rows, lanes].astype(f32)).astype(bf16)

    n_part = 4
    width = D_MODEL // n_part

    def conv_branch(n):
        cols = slice(n * width, (n + 1) * width)
        ya_s[:, cols] = (jnp.dot(ya_ref[...], wa_ref[:, cols], preferred_element_type=f32)
                         * ga_ref[:, cols].astype(f32))

    n_slots = (nblk + 2) * N_KV_HEADS
    conv_at = {(2 * n + 1) * n_slots // (2 * n_part): n for n in range(n_part)}
    for t in range(nblk + 2):
        for kh in range(N_KV_HEADS):
            if t * N_KV_HEADS + kh in conv_at:
                conv_branch(conv_at[t * N_KV_HEADS + kh])
            if has_local and 0 <= t - 1 < nblk:
                for b in range(3)[kh::N_KV_HEADS]:
                    fill_values(t - 1, b)
            if 0 <= t - 1 < nblk:
                stage_b(t - 1, kh)
            if 0 <= t - 2 < nblk:
                stage_c(t - 2, kh)
            if t < nblk:
                for e in range(HEADS_PER_SLAB):
                    stage_a(t, kh, e)

    for n in range(n_part):
        cols = slice(n * width, (n + 1) * width)
        y_s[:, cols] = (
            ya_s[:, cols]
            + jnp.dot(yb_s[...], wb_ref[:, cols], preferred_element_type=f32)
            * gb_ref[:, cols].astype(f32)).astype(bf16)
    y16 = y_s[...]
    if final:
        xn = x_ref[...] + gate_ref[...] * jnp.dot(y16, wo_ref[...], preferred_element_type=f32)
        ms = jnp.mean(xn * xn, axis=-1, keepdims=True)
        o_ref[...] = xn * lax.rsqrt(ms + EPS) * fw_ref[...]
    else:
        for n in range(n_part):
            cols = slice(n * width, (n + 1) * width)
            o_ref[:, cols] = x_ref[:, cols] + gate_ref[:, cols] * jnp.dot(
                y16, wo_ref[:, cols], preferred_element_type=f32)


def _mixer(x2, layer, mod, sink, proj_out, kc, vc, wa, wb, wo, final_w, *,
           seq, tq, ctx_len, has_local, gate_row0, gate_per_batch):
    ya, q, k, v, zb, ga, gb = proj_out
    n_rows = x2.shape[0]
    seqs_per_tile = max(tq // seq, 1)
    assert kc.shape[0] * seq == n_rows * ctx_len and not (has_local and seqs_per_tile > 1)
    n_tile_b = n_rows // (seq * seqs_per_tile)
    n_i = max(seq // tq, 1)
    nblk = tq // BLOCK
    final = final_w is not None

    def row(w):
        return pl.BlockSpec((tq, w), lambda b, i: (b * n_i + i, 0))

    def halo(rows_blk, w, side):
        per_tile = tq // rows_blk
        per_seq = seq // rows_blk
        if side < 0:
            return pl.BlockSpec((rows_blk, w), lambda b, i: (
                b * per_seq + jnp.maximum(i * per_tile - 1, 0), 0))
        return pl.BlockSpec((rows_blk, w), lambda b, i: (
            b * per_seq + jnp.minimum((i + 1) * per_tile, per_seq - 1), 0))

    def whole(shape):
        return pl.BlockSpec(shape, lambda b, i: (0,) * len(shape))

    if gate_per_batch:
        gate_spec = _mod_spec(layer, lambda b, i: gate_row0 + b, 2)
    else:
        gate_spec = _mod_spec(layer, lambda b, i: gate_row0, 2)

    kvw = 2 * KV_WIDTH
    in_specs = [pl.BlockSpec(memory_space=pltpu.SMEM),
                row(D_MODEL), gate_spec, row(CONV_WIDTH),
                row(ATTN_WIDTH), row(ATTN_WIDTH), row(D_MODEL), row(D_MODEL)]
    args = [sink, x2, mod, ya, q, zb, ga, gb]
    if has_local:
        in_specs += [row(kvw), halo(BLOCK, kvw, -1), halo(BLOCK, kvw, +1)] * 2
        args += [k, k, k, v, v, v]
    ctx_spec = pl.BlockSpec((seqs_per_tile * ctx_len, kvw), lambda b, i: (b, 0))
    in_specs += [ctx_spec, ctx_spec, _layer_spec(layer, (CONV_WIDTH, D_MODEL)),
                 _layer_spec(layer, (ATTN_WIDTH, D_MODEL)), _layer_spec(layer, (D_MODEL, D_MODEL))]
    args += [kc, vc, wa, wb, wo]
    if final:
        in_specs.append(whole((1, D_MODEL)))
        args.append(final_w.reshape(1, D_MODEL))

    nk = ctx_len + (3 * BLOCK if has_local else 0)
    k_rows = ctx_len + tq + 2 * BLOCK if has_local else seqs_per_tile * ctx_len
    n_slot = nblk if has_local else seqs_per_tile
    rows_a = SLABS_PER_KV * BLOCK
    scratch = [pltpu.VMEM((tq, D_MODEL), bf16),
               pltpu.VMEM((tq, D_MODEL), f32),
               pltpu.VMEM((tq, ATTN_WIDTH), bf16),
               pltpu.VMEM((2 * N_KV_HEADS, k_rows, LANES), bf16),
               pltpu.VMEM((n_slot, N_KV_HEADS, HEADS_PER_SLAB * nk, 2 * LANES), bf16),
               pltpu.VMEM((nblk * N_KV_HEADS * HEADS_PER_SLAB, rows_a, nk), f32),
               pltpu.VMEM((nblk * N_KV_HEADS, rows_a, HEADS_PER_SLAB * nk), bf16),
               pltpu.VMEM((nblk * N_KV_HEADS, rows_a, LANES), f32)]
    if has_local:
        scratch += [pltpu.VMEM((4, rows_a, BLOCK), f32)]

    return pl.pallas_call(
        functools.partial(_mix_kernel, layer=layer, tq=tq, ctx_len=ctx_len,
                          has_local=has_local, final=final),
        grid=(n_tile_b, n_i),
        in_specs=in_specs,
        out_specs=row(D_MODEL),
        out_shape=jax.ShapeDtypeStruct((n_rows, D_MODEL), f32),
        scratch_shapes=scratch,
        compiler_params=pltpu.CompilerParams(
            dimension_semantics=("arbitrary", "arbitrary"), vmem_limit_bytes=VMEM_LIMIT),
        name="mixer_latent" if has_local else "mixer_context",
    )(*args)


def _rope_tables(seq):
    n_rows = seq // GRID_W
    lane = np.arange(LANES)
    lane_freq = (jnp.asarray(ROPE_THETA, f32)
                 ** (-jnp.asarray(lane % ROPE_FREQS, f32) / ROPE_FREQS))[None, :]
    use_col = ((lane % HEAD_DIM) >= HEAD_DIM // 2)[None, None, :]
    upper = ((lane % (2 * ROPE_FREQS)) >= ROPE_FREQS)[None, None, :]
    ang_r = jnp.arange(n_rows, dtype=f32)[:, None] * lane_freq
    ang_c = jnp.arange(GRID_W, dtype=f32)[:, None] * lane_freq
    shape = (n_rows, GRID_W, LANES)
    cos = jnp.where(use_col, jnp.cos(ang_c)[None], jnp.cos(ang_r)[:, None]).reshape(seq, LANES)
    sin = jnp.broadcast_to(jnp.where(use_col, jnp.sin(ang_c)[None], jnp.sin(ang_r)[:, None]),
                           shape)
    zero = jnp.zeros(shape, f32)
    sin_lo = jnp.where(upper, zero, -sin).reshape(seq, LANES)
    sin_hi = jnp.where(upper, sin, zero).reshape(seq, LANES)
    return cos, sin_lo, sin_hi


def kernel(x, c, ctx, c_ctx, norm_w, w_mod, b_mod, w_in, conv_w, w_a_out, w_b_out,
           attn_sink, w_o, final_norm_w):
    batch, seq, _ = x.shape
    ctx_len = ctx.shape[1]
    depth = w_in.shape[0]
    assert batch + 1 <= MOD_ROWS and seq % 512 == 0 and ctx_len % BLOCK == 0

    c_all = jnp.zeros((MOD_ROWS, D_MODEL), f32).at[:batch].set(c).at[batch].set(c_ctx)
    mod = _modulation(c_all, w_mod, b_mod)
    rope_tabs = _rope_tables(seq)

    w_in_b = w_in.astype(bf16)
    wa_b, wb_b, wo_b = w_a_out.astype(bf16), w_b_out.astype(bf16), w_o.astype(bf16)

    x2 = x.reshape(batch * seq, D_MODEL)
    c2 = ctx.reshape(batch * ctx_len, D_MODEL)
    for l in range(depth):
        last = l == depth - 1
        pc = _projection(c2, l, norm_w, mod, w_in_b, conv_w, None,
                         subs=(ctx_len,) * 4, seq=ctx_len, mod_row0=batch, mod_per_seq=False)
        px = _projection(x2, l, norm_w, mod, w_in_b, conv_w, rope_tabs,
                         subs=(512, 512), seq=seq, mod_row0=0, mod_per_seq=True)
        kc, vc = pc[2], pc[3]
        x2 = _mixer(x2, l, mod, attn_sink, px, kc, vc, wa_b, wb_b, wo_b,
                    final_norm_w if last else None,
                    seq=seq, tq=512, ctx_len=ctx_len, has_local=True, gate_row0=0,
                    gate_per_batch=True)
        if not last:
            c2 = _mixer(c2, l, mod, attn_sink, pc, kc, vc, wa_b, wb_b, wo_b, None,
                        seq=ctx_len, tq=ctx_len, ctx_len=ctx_len, has_local=False,
                        gate_row0=batch, gate_per_batch=False)
    return x2.reshape(batch, seq, D_MODEL)
```

```python
import functools
import math

import numpy as np
import jax
import jax.numpy as jnp
from jax import lax
from jax.experimental import pallas as pl
from jax.experimental.pallas import tpu as pltpu

D_MODEL = 1024
GRID_W = 64
CONV_WIDTH = 512
CONV_K = 3
N_HEADS = 8
N_KV_HEADS = 2
HEAD_DIM = 64
ATTN_WIDTH = N_HEADS * HEAD_DIM
KV_WIDTH = N_KV_HEADS * HEAD_DIM
BLOCK = 128
ROPE_THETA = 10000.0
ROPE_FREQS = HEAD_DIM // 4
EPS = 1e-6
NEG = -1e30
LOG2E = math.log2(math.e)

LANES = 128
SUBLANES = 8
MOD_ROWS = 24
VMEM_LIMIT = 56 * 1024 * 1024
N_SLAB = ATTN_WIDTH // LANES
HEADS_PER_SLAB = LANES // HEAD_DIM
SLABS_PER_KV = N_SLAB // N_KV_HEADS

_OFF_B = 0
_OFF_C = _OFF_B + CONV_WIDTH
_OFF_U = _OFF_C + CONV_WIDTH
_OFF_ZA = _OFF_U + CONV_WIDTH
_OFF_Q = _OFF_ZA + CONV_WIDTH
_OFF_K = _OFF_Q + ATTN_WIDTH
_OFF_V = _OFF_K + KV_WIDTH
_OFF_ZB = _OFF_V + KV_WIDTH
_OFF_GA = _OFF_ZB + ATTN_WIDTH
_OFF_GB = _OFF_GA + D_MODEL
IN_COLS = _OFF_GB + D_MODEL

f32 = jnp.float32
bf16 = jnp.bfloat16


def _sigmoid(x):
    return 0.5 * jnp.tanh(0.5 * x) + 0.5


def _silu(x):
    return x * _sigmoid(x)


def _mod_kernel(c_ref, w_ref, b_ref, o_ref):
    s = _silu(c_ref[...]).astype(bf16)
    res = jnp.dot(s, w_ref[...].astype(bf16), preferred_element_type=f32) + b_ref[...]
    for r in range(MOD_ROWS):
        o_ref[r] = res[r:r + 1, :]


def _modulation(c_all, w_mod, b_mod):
    depth = w_mod.shape[0]
    return pl.pallas_call(
        _mod_kernel,
        grid=(depth, 3),
        in_specs=[
            pl.BlockSpec((MOD_ROWS, D_MODEL), lambda l, j: (0, 0)),
            pl.BlockSpec((None, D_MODEL, D_MODEL), lambda l, j: (l, 0, j)),
            pl.BlockSpec((None, 1, D_MODEL), lambda l, j: (l, 0, j)),
        ],
        out_specs=pl.BlockSpec((None, MOD_ROWS, None, 1, D_MODEL), lambda l, j: (l, 0, j, 0, 0)),
        out_shape=jax.ShapeDtypeStruct((depth, MOD_ROWS, 3, 1, D_MODEL), f32),
        name="modulation",
    )(c_all, w_mod, b_mod.reshape(depth, 1, 3 * D_MODEL))


def _rope(p, cos, sin_lo, sin_hi):
    outs = []
    for j in range(p.shape[1] // LANES):
        x = p[:, j * LANES:(j + 1) * LANES]
        outs.append(x * cos
                    + pltpu.roll(x, LANES - ROPE_FREQS, 1) * sin_lo
                    + pltpu.roll(x, ROPE_FREQS, 1) * sin_hi)
    return outs[0] if len(outs) == 1 else jnp.concatenate(outs, axis=1)


def _with_swapped_halves(a):
    return jnp.concatenate([a, pltpu.roll(a, HEAD_DIM, 1)], axis=1)


def _norm_mod(x, gain, shift):
    ms = jnp.mean(x * x, axis=-1, keepdims=True)
    return (x * lax.rsqrt(ms + EPS) * gain + shift).astype(bf16)


def _kv_proj_kernel(x_ref, nw_ref, shift_ref, scale_ref, w_ref, k_ref, v_ref):
    gain, shift = nw_ref[...] * (1.0 + scale_ref[...]), shift_ref[...]
    kv = jnp.dot(_norm_mod(x_ref[...], gain, shift), w_ref[...], preferred_element_type=f32)
    k_ref[...] = _with_swapped_halves(kv[:, :KV_WIDTH]).astype(bf16)
    v_ref[...] = _with_swapped_halves(kv[:, KV_WIDTH:]).astype(bf16)


def _proj_kernel(*refs, rope, seq, subs):
    it = iter(refs)
    x_ref, xp_ref, xn_ref, nw_ref, shift_ref, scale_ref, w_ref, cw_ref = (
        next(it) for _ in range(8))
    if rope:
        cos_ref, slo_ref, shi_ref = (next(it) for _ in range(3))
    ya_ref, q_ref, k_ref, v_ref, zb_ref, ga_ref, gb_ref = (next(it) for _ in range(7))
    h_s, cu_s = next(it), next(it)
    tm = x_ref.shape[0]
    assert sum(subs) == tm and (seq % tm == 0 or tm % seq == 0)
    halo = 2 * SUBLANES
    gain, shift = nw_ref[...] * (1.0 + scale_ref[...]), shift_ref[...]
    tiles_per_seq = max(seq // tm, 1)
    t_in_seq = pl.program_id(0) % tiles_per_seq
    cw = cw_ref[...]
    zero = jnp.zeros((SUBLANES, CONV_WIDTH), f32)

    def buffers(n):
        s0 = sum(subs[:n]) + n * halo
        return h_s.at[s0:s0 + subs[n] + halo], cu_s.at[s0:s0 + subs[n] + halo]

    for n, sub in enumerate(subs):
        r0 = sum(subs[:n])
        rows = slice(r0, r0 + sub)
        h, cu_b = buffers(n)
        before = xp_ref[...] if r0 == 0 else x_ref[r0 - SUBLANES:r0, :]
        after = xn_ref[...] if r0 + sub == tm else x_ref[r0 + sub:r0 + sub + SUBLANES, :]
        h[0:sub, :] = _norm_mod(x_ref[rows, :], gain, shift)
        h[sub:sub + halo, :] = _norm_mod(jnp.concatenate([before, after], axis=0), gain, shift)

        def proj(off, width, n_rows=sub):
            return jnp.dot(h[0:n_rows, :], w_ref[:, off:off + width],
                           preferred_element_type=f32)

        cu = proj(_OFF_C, CONV_WIDTH, sub + halo) * proj(_OFF_U, CONV_WIDTH, sub + halo)
        cu_before, cu_after = cu[sub:sub + SUBLANES], cu[sub + SUBLANES:]
        if r0 % seq == 0:
            cu_before = jnp.where(t_in_seq > 0, cu_before, zero) if r0 == 0 else zero
        if (r0 + sub) % seq == 0 or r0 + sub == tm:
            cu_after = (jnp.where(t_in_seq < tiles_per_seq - 1, cu_after, zero)
                        if r0 + sub == tm else zero)
        cu_b[0:SUBLANES, :] = cu_before
        cu_b[SUBLANES:SUBLANES + sub, :] = cu[0:sub]
        cu_b[SUBLANES + sub:, :] = cu_after
        conv = (cu_b[SUBLANES - 1:SUBLANES - 1 + sub, :] * cw[0:1]
                + cu_b[SUBLANES:SUBLANES + sub, :] * cw[1:2]
                + cu_b[SUBLANES + 1:SUBLANES + 1 + sub, :] * cw[2:3])
        ya_ref[rows, :] = (proj(_OFF_B, CONV_WIDTH) * conv
                           * _silu(proj(_OFF_ZA, CONV_WIDTH))).astype(bf16)

        q = proj(_OFF_Q, ATTN_WIDTH)
        kv = proj(_OFF_K, 2 * KV_WIDTH)
        k = kv[:, :KV_WIDTH]
        if rope:
            cos, slo, shi = cos_ref[rows, :], slo_ref[rows, :], shi_ref[rows, :]
            q = _rope(q, cos, slo, shi)
            k = _rope(k, cos, slo, shi)
        q_ref[rows, :] = (q * (HEAD_DIM ** -0.5 * LOG2E)).astype(bf16)
        k_ref[rows, :] = _with_swapped_halves(k).astype(bf16)
        v_ref[rows, :] = _with_swapped_halves(kv[:, KV_WIDTH:]).astype(bf16)
        zb_ref[rows, :] = _silu(proj(_OFF_ZB, ATTN_WIDTH)).astype(bf16)
        part = 2 * LANES
        for c in range(D_MODEL // part):
            ga_ref[rows, c * part:(c + 1) * part] = _sigmoid(
                proj(_OFF_GA + c * part, part)).astype(bf16)
        for c in range(D_MODEL // part):
            gb_ref[rows, c * part:(c + 1) * part] = _sigmoid(
                proj(_OFF_GB + c * part, part)).astype(bf16)


def _mod_spec(layer, row_of, which):
    return pl.BlockSpec((None, None, None, 1, D_MODEL),
                        lambda *g: (layer, row_of(*g), which, 0, 0))


def _layer_spec(layer, shape):
    return pl.BlockSpec((None,) + tuple(shape), lambda *g: (layer,) + (0,) * len(shape))


def _projection(x2, layer, norm_w, mod, w_in, conv_w, rope_tabs, *, subs, seq, mod_row0,
                mod_per_seq):
    tm = sum(subs)
    scratch_rows = tm + len(subs) * 2 * SUBLANES
    n_rows = x2.shape[0]
    assert seq % tm == 0 or (tm % seq == 0 and not mod_per_seq and rope_tabs is None)
    tiles_per_seq = max(seq // tm, 1)
    halo_per_tile = tm // SUBLANES
    n_halo = n_rows // SUBLANES
    if mod_per_seq:
        mod_row = lambda t: mod_row0 + t // tiles_per_seq
    else:
        mod_row = lambda t: mod_row0
    row = lambda w: pl.BlockSpec((tm, w), lambda t: (t, 0))
    in_specs = [
        row(D_MODEL),
        pl.BlockSpec((SUBLANES, D_MODEL),
                     lambda t: (jnp.maximum(t * halo_per_tile - 1, 0), 0)),
        pl.BlockSpec((SUBLANES, D_MODEL),
                     lambda t: (jnp.minimum((t + 1) * halo_per_tile, n_halo - 1), 0)),
        _layer_spec(layer, (1, D_MODEL)),
        _mod_spec(layer, mod_row, 0),
        _mod_spec(layer, mod_row, 1),
        _layer_spec(layer, (D_MODEL, IN_COLS)),
        _layer_spec(layer, (CONV_K, CONV_WIDTH)),
    ]
    args = [x2, x2, x2, norm_w.reshape(-1, 1, D_MODEL), mod, mod, w_in, conv_w]
    if rope_tabs is not None:
        in_specs += [pl.BlockSpec((tm, LANES), lambda t: (t % tiles_per_seq, 0))] * 3
        args += list(rope_tabs)
    widths = (CONV_WIDTH, ATTN_WIDTH, 2 * KV_WIDTH, 2 * KV_WIDTH, ATTN_WIDTH, D_MODEL, D_MODEL)
    return pl.pallas_call(
        functools.partial(_proj_kernel, rope=rope_tabs is not None,
                          seq=seq, subs=tuple(subs)),
        grid=(n_rows // tm,),
        in_specs=in_specs,
        out_specs=[row(w) for w in widths],
        out_shape=[jax.ShapeDtypeStruct((n_rows, w), bf16) for w in widths],
        scratch_shapes=[pltpu.VMEM((scratch_rows, D_MODEL), bf16),
                        pltpu.VMEM((scratch_rows, CONV_WIDTH), f32)],
        compiler_params=pltpu.CompilerParams(
            dimension_semantics=("arbitrary",), vmem_limit_bytes=VMEM_LIMIT),
        name="projection_rope" if rope_tabs is not None else "projection",
    )(*args)


def _kv_projection(x2, layer, norm_w, mod, w_in, *, tm, mod_row):
    n_rows = x2.shape[0]
    kvw = 2 * KV_WIDTH
    assert _OFF_K % kvw == 0 and _OFF_V == _OFF_K + KV_WIDTH
    row = lambda w: pl.BlockSpec((tm, w), lambda t: (t, 0))
    return pl.pallas_call(
        _kv_proj_kernel,
        grid=(n_rows // tm,),
        in_specs=[row(D_MODEL),
                  _layer_spec(layer, (1, D_MODEL)),
                  _mod_spec(layer, lambda t: mod_row, 0),
                  _mod_spec(layer, lambda t: mod_row, 1),
                  pl.BlockSpec((None, D_MODEL, kvw), lambda t: (layer, 0, _OFF_K // kvw))],
        out_specs=[row(kvw), row(kvw)],
        out_shape=[jax.ShapeDtypeStruct((n_rows, kvw), bf16)] * 2,
        compiler_params=pltpu.CompilerParams(
            dimension_semantics=("arbitrary",), vmem_limit_bytes=VMEM_LIMIT),
        name="projection_kv",
    )(x2, norm_w.reshape(-1, 1, D_MODEL), mod, mod, w_in)


def _kv_variants(a2):
    a, sw = a2[:, :LANES], a2[:, LANES:]
    low = lax.broadcasted_iota(jnp.int32, a.shape, 1) < HEAD_DIM
    z = jnp.zeros_like(a)
    return [jnp.where(low, a, z), jnp.where(low, z, sw),
            jnp.where(low, sw, z), jnp.where(low, z, a)]


def _mix_kernel(*refs, layer, tq, ctx_len, has_local, final):
    it = iter(refs)
    sink_ref = next(it)
    x_ref, gate_ref, ya_ref, q_ref, zb_ref, ga_ref, gb_ref = (next(it) for _ in range(7))
    if has_local:
        k_ref, kp_ref, kn_ref, v_ref, vp_ref, vn_ref = (next(it) for _ in range(6))
    kc_ref, vc_ref = next(it), next(it)
    wa_ref, wb_ref, wo_ref = (next(it) for _ in range(3))
    if final:
        fw_ref = next(it)
    o_ref = next(it)
    y_s, ya_s, yb_s, k_s, vm_s, s_s, p_s, z_s = (next(it) for _ in range(8))
    if has_local:
        bias_s = next(it)

    i = pl.program_id(1)
    n_i = pl.num_programs(1)
    nblk = tq // BLOCK
    n_ct = kc_ref.shape[0] // ctx_len
    assert has_local <= (n_ct == 1)

    def seq_of(j):
        return j // (nblk // n_ct)
    n_loc = 3 * BLOCK if has_local else 0
    nk = ctx_len + n_loc
    n_slot = vm_s.shape[0]
    rows_a = SLABS_PER_KV * BLOCK

    @pl.when(i == 0)
    def _():
        kvar = _kv_variants(kc_ref[...])
        vvar = _kv_variants(vc_ref[...])
        low = lax.broadcasted_iota(jnp.int32, (nk, LANES), 1) < HEAD_DIM
        ones_on = [jnp.where(low, 1.0, 0.0).astype(bf16), jnp.where(low, 0.0, 1.0).astype(bf16)]
        for n in range(4):
            k_s[n, 0:n_ct * ctx_len, :] = kvar[n]
        for slot in range(n_slot):
            c0 = 0 if has_local else slot * ctx_len
            for kh in range(N_KV_HEADS):
                for e in range(HEADS_PER_SLAB):
                    vm_s[slot, kh, e * nk:e * nk + ctx_len, 0:LANES] = (
                        vvar[2 * kh + e][c0:c0 + ctx_len])
                    vm_s[slot, kh, e * nk:(e + 1) * nk, LANES:] = ones_on[e]

    def local_block(ref_prev, ref_tile, ref_next, t):
        if t == 0:
            return ref_prev[...]
        if t == nblk + 1:
            return ref_next[...]
        return ref_tile[(t - 1) * BLOCK:t * BLOCK, :]

    def fill_values(j, b):
        vvar = _kv_variants(local_block(vp_ref, v_ref, vn_ref, j + b))
        for kh in range(N_KV_HEADS):
            for e in range(HEADS_PER_SLAB):
                r = e * nk + ctx_len + b * BLOCK
                vm_s[j, kh, r:r + BLOCK, 0:LANES] = vvar[2 * kh + e]

    def fill_keys(t):
        for n, a in enumerate(_kv_variants(local_block(kp_ref, k_ref, kn_ref, t))):
            k_s[n, ctx_len + t * BLOCK:ctx_len + (t + 1) * BLOCK, :] = a

    def fill_bias():
        qi = lax.broadcasted_iota(jnp.int32, (rows_a, BLOCK), 0) % BLOCK
        ci = lax.broadcasted_iota(jnp.int32, (rows_a, BLOCK), 1)
        tri_prev = jnp.where(ci >= qi, 0.0, NEG)
        tri_next = jnp.where(ci <= qi, 0.0, NEG)
        bias_s[0] = tri_prev + jnp.where(i == 0, NEG, 0.0)
        bias_s[1] = tri_prev
        bias_s[2] = tri_next
        bias_s[3] = tri_next + jnp.where(i == n_i - 1, NEG, 0.0)

    if has_local:
        for t in range(nblk + 2):
            fill_keys(t)
        fill_bias()

    nt = (((1,), (1,)), ((), ()))

    def stage_a(j, kh, e):
        u = (j * N_KV_HEADS + kh) * HEADS_PER_SLAB + e
        rows = slice(j * BLOCK, (j + 1) * BLOCK)
        q2 = jnp.concatenate(
            [q_ref[rows, (kh * SLABS_PER_KV + s) * LANES:(kh * SLABS_PER_KV + s + 1) * LANES]
             for s in range(SLABS_PER_KV)], axis=0)
        var = 2 * kh + e
        kc0 = seq_of(j) * ctx_len
        s_s[u, :, 0:ctx_len] = lax.dot_general(
            q2, k_s[var, kc0:kc0 + ctx_len, :], nt, preferred_element_type=f32)
        if has_local:
            r0 = ctx_len + j * BLOCK
            s_loc = lax.dot_general(q2, k_s[var, r0:r0 + n_loc, :], nt,
                                    preferred_element_type=f32)
            c0 = ctx_len
            s_s[u, :, c0:c0 + BLOCK] = s_loc[:, :BLOCK] + bias_s[0 if j == 0 else 1]
            s_s[u, :, c0 + BLOCK:c0 + 2 * BLOCK] = s_loc[:, BLOCK:2 * BLOCK]
            s_s[u, :, c0 + 2 * BLOCK:] = s_loc[:, 2 * BLOCK:] + bias_s[3 if j == nblk - 1 else 2]

    low_half = lax.broadcasted_iota(jnp.int32, (BLOCK, LANES), 1) < HEAD_DIM

    def stage_b(j, kh):
        jk = j * N_KV_HEADS + kh
        for s in range(SLABS_PER_KV):
            rs = slice(s * BLOCK, (s + 1) * BLOCK)
            z = []
            for e in range(HEADS_PER_SLAB):
                u = jk * HEADS_PER_SLAB + e
                sk = sink_ref[layer, (kh * SLABS_PER_KV + s) * HEADS_PER_SLAB + e] * LOG2E
                mx = s_s[u, rs, 0:LANES]
                for n in range(1, nk // LANES):
                    mx = jnp.maximum(mx, s_s[u, rs, n * LANES:(n + 1) * LANES])
                m = jnp.maximum(jnp.max(mx, axis=1, keepdims=True), sk)
                for n in range(nk // LANES):
                    col = e * nk + n * LANES
                    p_s[jk, rs, col:col + LANES] = jnp.exp2(
                        s_s[u, rs, n * LANES:(n + 1) * LANES] - m).astype(bf16)
                z.append(jnp.exp2(sk - m))
            z_s[jk, rs, :] = jnp.where(low_half, z[0], z[1])

    def stage_c(j, kh):
        jk = j * N_KV_HEADS + kh
        res = jnp.dot(p_s[jk], vm_s[j if has_local else seq_of(j), kh],
                      preferred_element_type=f32)
        attn = res[:, :LANES] / (res[:, LANES:] + z_s[jk])
        rows = slice(j * BLOCK, (j + 1) * BLOCK)
        for s in range(SLABS_PER_KV):
            c = kh * SLABS_PER_KV + s
            lanes = slice(c * LANES, (c + 1) * LANES)
            yb_s[rows, lanes] = (attn[s * BLOCK:(s + 1) * BLOCK]
                                 * zb_ref[rows, lanes].astype(f32)).astype(bf16)

    n_part = 4
    width = D_MODEL // n_part

    def conv_branch(n):
        cols = slice(n * width, (n + 1) * width)
        ya_s[:, cols] = (jnp.dot(ya_ref[...], wa_ref[:, cols], preferred_element_type=f32)
                         * ga_ref[:, cols].astype(f32))

    n_slots = (nblk + 2) * N_KV_HEADS
    conv_at = {(2 * n + 1) * n_slots // (2 * n_part): n for n in range(n_part)}
    for t in range(nblk + 2):
        for kh in range(N_KV_HEADS):
            if t * N_KV_HEADS + kh in conv_at:
                conv_branch(conv_at[t * N_KV_HEADS + kh])
            if has_local and 0 <= t - 1 < nblk:
                for b in range(3)[kh::N_KV_HEADS]:
                    fill_values(t - 1, b)
            if 0 <= t - 1 < nblk:
                stage_b(t - 1, kh)
            if 0 <= t - 2 < nblk:
                stage_c(t - 2, kh)
            if t < nblk:
                for e in range(HEADS_PER_SLAB):
                    stage_a(t, kh, e)

    for n in range(n_part):
        cols = slice(n * width, (n + 1) * width)
        y_s[:, cols] = (
            ya_s[:, cols]
            + jnp.dot(yb_s[...], wb_ref[:, cols], preferred_element_type=f32)
            * gb_ref[:, cols].astype(f32)).astype(bf16)
    y16 = y_s[...]
    if final:
        xn = x_ref[...] + gate_ref[...] * jnp.dot(y16, wo_ref[...], preferred_element_type=f32)
        ms = jnp.mean(xn * xn, axis=-1, keepdims=True)
        o_ref[...] = xn * lax.rsqrt(ms + EPS) * fw_ref[...]
    else:
        for n in range(n_part):
            cols = slice(n * width, (n + 1) * width)
            o_ref[:, cols] = x_ref[:, cols] + gate_ref[:, cols] * jnp.dot(
                y16, wo_ref[:, cols], preferred_element_type=f32)


def _mixer(x2, layer, mod, sink, proj_out, kc, vc, wa, wb, wo, final_w, *,
           seq, tq, ctx_len, has_local, gate_row0, gate_per_batch):
    ya, q, k, v, zb, ga, gb = proj_out
    n_rows = x2.shape[0]
    seqs_per_tile = max(tq // seq, 1)
    assert kc.shape[0] * seq == n_rows * ctx_len and not (has_local and seqs_per_tile > 1)
    n_tile_b = n_rows // (seq * seqs_per_tile)
    n_i = max(seq // tq, 1)
    nblk = tq // BLOCK
    final = final_w is not None

    def row(w):
        return pl.BlockSpec((tq, w), lambda b, i: (b * n_i + i, 0))

    def halo(rows_blk, w, side):
        per_tile = tq // rows_blk
        per_seq = seq // rows_blk
        if side < 0:
            return pl.BlockSpec((rows_blk, w), lambda b, i: (
                b * per_seq + jnp.maximum(i * per_tile - 1, 0), 0))
        return pl.BlockSpec((rows_blk, w), lambda b, i: (
            b * per_seq + jnp.minimum((i + 1) * per_tile, per_seq - 1), 0))

    def whole(shape):
        return pl.BlockSpec(shape, lambda b, i: (0,) * len(shape))

    if gate_per_batch:
        gate_spec = _mod_spec(layer, lambda b, i: gate_row0 + b, 2)
    else:
        gate_spec = _mod_spec(layer, lambda b, i: gate_row0, 2)

    kvw = 2 * KV_WIDTH
    in_specs = [pl.BlockSpec(memory_space=pltpu.SMEM),
                row(D_MODEL), gate_spec, row(CONV_WIDTH),
                row(ATTN_WIDTH), row(ATTN_WIDTH), row(D_MODEL), row(D_MODEL)]
    args = [sink, x2, mod, ya, q, zb, ga, gb]
    if has_local:
        in_specs += [row(kvw), halo(BLOCK, kvw, -1), halo(BLOCK, kvw, +1)] * 2
        args += [k, k, k, v, v, v]
    ctx_spec = pl.BlockSpec((seqs_per_tile * ctx_len, kvw), lambda b, i: (b, 0))
    in_specs += [ctx_spec, ctx_spec, _layer_spec(layer, (CONV_WIDTH, D_MODEL)),
                 _layer_spec(layer, (ATTN_WIDTH, D_MODEL)), _layer_spec(layer, (D_MODEL, D_MODEL))]
    args += [kc, vc, wa, wb, wo]
    if final:
        in_specs.append(whole((1, D_MODEL)))
        args.append(final_w.reshape(1, D_MODEL))

    nk = ctx_len + (3 * BLOCK if has_local else 0)
    k_rows = ctx_len + tq + 2 * BLOCK if has_local else seqs_per_tile * ctx_len
    n_slot = nblk if has_local else seqs_per_tile
    rows_a = SLABS_PER_KV * BLOCK
    scratch = [pltpu.VMEM((tq, D_MODEL), bf16),
               pltpu.VMEM((tq, D_MODEL), f32),
               pltpu.VMEM((tq, ATTN_WIDTH), bf16),
               pltpu.VMEM((2 * N_KV_HEADS, k_rows, LANES), bf16),
               pltpu.VMEM((n_slot, N_KV_HEADS, HEADS_PER_SLAB * nk, 2 * LANES), bf16),
               pltpu.VMEM((nblk * N_KV_HEADS * HEADS_PER_SLAB, rows_a, nk), f32),
               pltpu.VMEM((nblk * N_KV_HEADS, rows_a, HEADS_PER_SLAB * nk), bf16),
               pltpu.VMEM((nblk * N_KV_HEADS, rows_a, LANES), f32)]
    if has_local:
        scratch += [pltpu.VMEM((4, rows_a, BLOCK), f32)]

    return pl.pallas_call(
        functools.partial(_mix_kernel, layer=layer, tq=tq, ctx_len=ctx_len,
                          has_local=has_local, final=final),
        grid=(n_tile_b, n_i),
        in_specs=in_specs,
        out_specs=row(D_MODEL),
        out_shape=jax.ShapeDtypeStruct((n_rows, D_MODEL), f32),
        scratch_shapes=scratch,
        compiler_params=pltpu.CompilerParams(
            dimension_semantics=("arbitrary", "arbitrary"), vmem_limit_bytes=VMEM_LIMIT),
        name="mixer_latent" if has_local else "mixer_context",
    )(*args)


def _rope_tables(seq):
    n_rows = seq // GRID_W
    lane = np.arange(LANES)
    lane_freq = (jnp.asarray(ROPE_THETA, f32)
                 ** (-jnp.asarray(lane % ROPE_FREQS, f32) / ROPE_FREQS))[None, :]
    use_col = ((lane % HEAD_DIM) >= HEAD_DIM // 2)[None, None, :]
    upper = ((lane % (2 * ROPE_FREQS)) >= ROPE_FREQS)[None, None, :]
    ang_r = jnp.arange(n_rows, dtype=f32)[:, None] * lane_freq
    ang_c = jnp.arange(GRID_W, dtype=f32)[:, None] * lane_freq
    shape = (n_rows, GRID_W, LANES)
    cos = jnp.where(use_col, jnp.cos(ang_c)[None], jnp.cos(ang_r)[:, None]).reshape(seq, LANES)
    sin = jnp.broadcast_to(jnp.where(use_col, jnp.sin(ang_c)[None], jnp.sin(ang_r)[:, None]),
                           shape)
    zero = jnp.zeros(shape, f32)
    sin_lo = jnp.where(upper, zero, -sin).reshape(seq, LANES)
    sin_hi = jnp.where(upper, sin, zero).reshape(seq, LANES)
    return cos, sin_lo, sin_hi


def kernel(x, c, ctx, c_ctx, norm_w, w_mod, b_mod, w_in, conv_w, w_a_out, w_b_out,
           attn_sink, w_o, final_norm_w):
    batch, seq, _ = x.shape
    ctx_len = ctx.shape[1]
    depth = w_in.shape[0]
    assert batch + 1 <= MOD_ROWS and seq % 512 == 0 and ctx_len % BLOCK == 0

    c_all = jnp.zeros((MOD_ROWS, D_MODEL), f32).at[:batch].set(c).at[batch].set(c_ctx)
    mod = _modulation(c_all, w_mod, b_mod)
    rope_tabs = _rope_tables(seq)

    w_in_b = w_in.astype(bf16)
    wa_b, wb_b, wo_b = w_a_out.astype(bf16), w_b_out.astype(bf16), w_o.astype(bf16)

    x2 = x.reshape(batch * seq, D_MODEL)
    c2 = ctx.reshape(batch * ctx_len, D_MODEL)
    for l in range(depth):
        last = l == depth - 1
        if last:
            kc, vc = _kv_projection(c2, l, norm_w, mod, w_in_b, tm=4 * ctx_len, mod_row=batch)
        else:
            pc = _projection(c2, l, norm_w, mod, w_in_b, conv_w, None, subs=(ctx_len,) * 4,
                             seq=ctx_len, mod_row0=batch, mod_per_seq=False)
            kc, vc = pc[2], pc[3]
        px = _projection(x2, l, norm_w, mod, w_in_b, conv_w, rope_tabs,
                         subs=(512, 512), seq=seq, mod_row0=0, mod_per_seq=True)
        x2 = _mixer(x2, l, mod, attn_sink, px, kc, vc, wa_b, wb_b, wo_b,
                    final_norm_w if last else None,
                    seq=seq, tq=512, ctx_len=ctx_len, has_local=True, gate_row0=0,
                    gate_per_batch=True)
        if not last:
            c2 = _mixer(c2, l, mod, attn_sink, pc, kc, vc, wa_b, wb_b, wo_b, None,
                        seq=ctx_len, tq=ctx_len, ctx_len=ctx_len, has_local=False,
                        gate_row0=batch, gate_per_batch=False)
    return x2.reshape(batch, seq, D_MODEL)
```

```python
import functools
import math

import numpy as np
import jax
import jax.numpy as jnp
from jax import lax
from jax.experimental import pallas as pl
from jax.experimental.pallas import tpu as pltpu

D_MODEL = 1024
GRID_W = 64
CONV_WIDTH = 512
CONV_K = 3
N_HEADS = 8
N_KV_HEADS = 2
HEAD_DIM = 64
ATTN_WIDTH = N_HEADS * HEAD_DIM
KV_WIDTH = N_KV_HEADS * HEAD_DIM
BLOCK = 128
ROPE_THETA = 10000.0
ROPE_FREQS = HEAD_DIM // 4
EPS = 1e-6
NEG = -1e30
LOG2E = math.log2(math.e)

LANES = 128
SUBLANES = 8
MOD_ROWS = 24
VMEM_LIMIT = 56 * 1024 * 1024
N_SLAB = ATTN_WIDTH // LANES
HEADS_PER_SLAB = LANES // HEAD_DIM
SLABS_PER_KV = N_SLAB // N_KV_HEADS

_OFF_B = 0
_OFF_C = _OFF_B + CONV_WIDTH
_OFF_U = _OFF_C + CONV_WIDTH
_OFF_ZA = _OFF_U + CONV_WIDTH
_OFF_Q = _OFF_ZA + CONV_WIDTH
_OFF_K = _OFF_Q + ATTN_WIDTH
_OFF_V = _OFF_K + KV_WIDTH
_OFF_ZB = _OFF_V + KV_WIDTH
_OFF_GA = _OFF_ZB + ATTN_WIDTH
_OFF_GB = _OFF_GA + D_MODEL
IN_COLS = _OFF_GB + D_MODEL

f32 = jnp.float32
bf16 = jnp.bfloat16


def _sigmoid(x):
    return 0.5 * jnp.tanh(0.5 * x) + 0.5


def _silu(x):
    return x * _sigmoid(x)


def _mod_kernel(c_ref, w_ref, b_ref, o_ref):
    s = _silu(c_ref[...]).astype(bf16)
    res = jnp.dot(s, w_ref[...].astype(bf16), preferred_element_type=f32) + b_ref[...]
    for r in range(MOD_ROWS):
        o_ref[r] = res[r:r + 1, :]


def _modulation(c_all, w_mod, b_mod):
    depth = w_mod.shape[0]
    return pl.pallas_call(
        _mod_kernel,
        grid=(depth, 3),
        in_specs=[
            pl.BlockSpec((MOD_ROWS, D_MODEL), lambda l, j: (0, 0)),
            pl.BlockSpec((None, D_MODEL, D_MODEL), lambda l, j: (l, 0, j)),
            pl.BlockSpec((None, 1, D_MODEL), lambda l, j: (l, 0, j)),
        ],
        out_specs=pl.BlockSpec((None, MOD_ROWS, None, 1, D_MODEL), lambda l, j: (l, 0, j, 0, 0)),
        out_shape=jax.ShapeDtypeStruct((depth, MOD_ROWS, 3, 1, D_MODEL), f32),
        name="modulation",
    )(c_all, w_mod, b_mod.reshape(depth, 1, 3 * D_MODEL))


def _rope(p, cos, sin_lo, sin_hi):
    outs = []
    for j in range(p.shape[1] // LANES):
        x = p[:, j * LANES:(j + 1) * LANES]
        outs.append(x * cos
                    + pltpu.roll(x, LANES - ROPE_FREQS, 1) * sin_lo
                    + pltpu.roll(x, ROPE_FREQS, 1) * sin_hi)
    return outs[0] if len(outs) == 1 else jnp.concatenate(outs, axis=1)


def _with_swapped_halves(a):
    return jnp.concatenate([a, pltpu.roll(a, HEAD_DIM, 1)], axis=1)


def _norm_mod(x, gain, shift):
    ms = jnp.mean(x * x, axis=-1, keepdims=True)
    return (x * lax.rsqrt(ms + EPS) * gain + shift).astype(bf16)


def _kv_proj_kernel(x_ref, nw_ref, shift_ref, scale_ref, w_ref, k_ref, v_ref):
    gain, shift = nw_ref[...] * (1.0 + scale_ref[...]), shift_ref[...]
    kv = jnp.dot(_norm_mod(x_ref[...], gain, shift), w_ref[...], preferred_element_type=f32)
    k_ref[...] = _with_swapped_halves(kv[:, :KV_WIDTH]).astype(bf16)
    v_ref[...] = _with_swapped_halves(kv[:, KV_WIDTH:]).astype(bf16)


def _proj_kernel(*refs, rope, seq, subs):
    it = iter(refs)
    x_ref, xp_ref, xn_ref, nw_ref, shift_ref, scale_ref, w_ref, cw_ref = (
        next(it) for _ in range(8))
    if rope:
        cos_ref, slo_ref, shi_ref = (next(it) for _ in range(3))
    ya_ref, q_ref, k_ref, v_ref, zb_ref, ga_ref, gb_ref = (next(it) for _ in range(7))
    h_s, cu_s = next(it), next(it)
    tm = x_ref.shape[0]
    assert sum(subs) == tm and (seq % tm == 0 or tm % seq == 0)
    halo = 2 * SUBLANES
    gain, shift = nw_ref[...] * (1.0 + scale_ref[...]), shift_ref[...]
    tiles_per_seq = max(seq // tm, 1)
    t_in_seq = pl.program_id(0) % tiles_per_seq
    cw = cw_ref[...]
    zero = jnp.zeros((SUBLANES, CONV_WIDTH), f32)

    def buffers(n):
        s0 = sum(subs[:n]) + n * halo
        return h_s.at[s0:s0 + subs[n] + halo], cu_s.at[s0:s0 + subs[n] + halo]

    for n, sub in enumerate(subs):
        r0 = sum(subs[:n])
        rows = slice(r0, r0 + sub)
        h, cu_b = buffers(n)
        before = xp_ref[...] if r0 == 0 else x_ref[r0 - SUBLANES:r0, :]
        after = xn_ref[...] if r0 + sub == tm else x_ref[r0 + sub:r0 + sub + SUBLANES, :]
        h[0:sub, :] = _norm_mod(x_ref[rows, :], gain, shift)
        h[sub:sub + halo, :] = _norm_mod(jnp.concatenate([before, after], axis=0), gain, shift)

        def proj(off, width, n_rows=sub):
            return jnp.dot(h[0:n_rows, :], w_ref[:, off:off + width],
                           preferred_element_type=f32)

        cu = proj(_OFF_C, CONV_WIDTH, sub + halo) * proj(_OFF_U, CONV_WIDTH, sub + halo)
        cu_before, cu_after = cu[sub:sub + SUBLANES], cu[sub + SUBLANES:]
        if r0 % seq == 0:
            cu_before = jnp.where(t_in_seq > 0, cu_before, zero) if r0 == 0 else zero
        if (r0 + sub) % seq == 0 or r0 + sub == tm:
            cu_after = (jnp.where(t_in_seq < tiles_per_seq - 1, cu_after, zero)
                        if r0 + sub == tm else zero)
        cu_b[0:SUBLANES, :] = cu_before
        cu_b[SUBLANES:SUBLANES + sub, :] = cu[0:sub]
        cu_b[SUBLANES + sub:, :] = cu_after
        conv = (cu_b[SUBLANES - 1:SUBLANES - 1 + sub, :] * cw[0:1]
                + cu_b[SUBLANES:SUBLANES + sub, :] * cw[1:2]
                + cu_b[SUBLANES + 1:SUBLANES + 1 + sub, :] * cw[2:3])
        ya_ref[rows, :] = (proj(_OFF_B, CONV_WIDTH) * conv
                           * _silu(proj(_OFF_ZA, CONV_WIDTH))).astype(bf16)

        q = proj(_OFF_Q, ATTN_WIDTH)
        kv = proj(_OFF_K, 2 * KV_WIDTH)
        k = kv[:, :KV_WIDTH]
        if rope:
            cos, slo, shi = cos_ref[rows, :], slo_ref[rows, :], shi_ref[rows, :]
            q = _rope(q, cos, slo, shi)
            k = _rope(k, cos, slo, shi)
        q_ref[rows, :] = (q * (HEAD_DIM ** -0.5 * LOG2E)).astype(bf16)
        k_ref[rows, :] = _with_swapped_halves(k).astype(bf16)
        v_ref[rows, :] = _with_swapped_halves(kv[:, KV_WIDTH:]).astype(bf16)
        zb_ref[rows, :] = _silu(proj(_OFF_ZB, ATTN_WIDTH)).astype(bf16)
        part = 2 * LANES
        for c in range(D_MODEL // part):
            ga_ref[rows, c * part:(c + 1) * part] = _sigmoid(
                proj(_OFF_GA + c * part, part)).astype(bf16)
        for c in range(D_MODEL // part):
            gb_ref[rows, c * part:(c + 1) * part] = _sigmoid(
                proj(_OFF_GB + c * part, part)).astype(bf16)


def _mod_spec(layer, row_of, which):
    return pl.BlockSpec((None, None, None, 1, D_MODEL),
                        lambda *g: (layer, row_of(*g), which, 0, 0))


def _layer_spec(layer, shape):
    return pl.BlockSpec((None,) + tuple(shape), lambda *g: (layer,) + (0,) * len(shape))


def _projection(x2, layer, norm_w, mod, w_in, conv_w, rope_tabs, *, subs, seq, mod_row0,
                mod_per_seq):
    tm = sum(subs)
    scratch_rows = tm + len(subs) * 2 * SUBLANES
    n_rows = x2.shape[0]
    assert seq % tm == 0 or (tm % seq == 0 and not mod_per_seq and rope_tabs is None)
    tiles_per_seq = max(seq // tm, 1)
    halo_per_tile = tm // SUBLANES
    n_halo = n_rows // SUBLANES
    if mod_per_seq:
        mod_row = lambda t: mod_row0 + t // tiles_per_seq
    else:
        mod_row = lambda t: mod_row0
    row = lambda w: pl.BlockSpec((tm, w), lambda t: (t, 0))
    in_specs = [
        row(D_MODEL),
        pl.BlockSpec((SUBLANES, D_MODEL),
                     lambda t: (jnp.maximum(t * halo_per_tile - 1, 0), 0)),
        pl.BlockSpec((SUBLANES, D_MODEL),
                     lambda t: (jnp.minimum((t + 1) * halo_per_tile, n_halo - 1), 0)),
        _layer_spec(layer, (1, D_MODEL)),
        _mod_spec(layer, mod_row, 0),
        _mod_spec(layer, mod_row, 1),
        _layer_spec(layer, (D_MODEL, IN_COLS)),
        _layer_spec(layer, (CONV_K, CONV_WIDTH)),
    ]
    args = [x2, x2, x2, norm_w.reshape(-1, 1, D_MODEL), mod, mod, w_in, conv_w]
    if rope_tabs is not None:
        in_specs += [pl.BlockSpec((tm, LANES), lambda t: (t % tiles_per_seq, 0))] * 3
        args += list(rope_tabs)
    widths = (CONV_WIDTH, ATTN_WIDTH, 2 * KV_WIDTH, 2 * KV_WIDTH, ATTN_WIDTH, D_MODEL, D_MODEL)
    return pl.pallas_call(
        functools.partial(_proj_kernel, rope=rope_tabs is not None,
                          seq=seq, subs=tuple(subs)),
        grid=(n_rows // tm,),
        in_specs=in_specs,
        out_specs=[row(w) for w in widths],
        out_shape=[jax.ShapeDtypeStruct((n_rows, w), bf16) for w in widths],
        scratch_shapes=[pltpu.VMEM((scratch_rows, D_MODEL), bf16),
                        pltpu.VMEM((scratch_rows, CONV_WIDTH), f32)],
        compiler_params=pltpu.CompilerParams(
            dimension_semantics=("arbitrary",), vmem_limit_bytes=VMEM_LIMIT),
        name="projection_rope" if rope_tabs is not None else "projection",
    )(*args)


def _kv_projection(x2, layer, norm_w, mod, w_in, *, tm, mod_row):
    n_rows = x2.shape[0]
    kvw = 2 * KV_WIDTH
    assert _OFF_K % kvw == 0 and _OFF_V == _OFF_K + KV_WIDTH
    row = lambda w: pl.BlockSpec((tm, w), lambda t: (t, 0))
    return pl.pallas_call(
        _kv_proj_kernel,
        grid=(n_rows // tm,),
        in_specs=[row(D_MODEL),
                  _layer_spec(layer, (1, D_MODEL)),
                  _mod_spec(layer, lambda t: mod_row, 0),
                  _mod_spec(layer, lambda t: mod_row, 1),
                  pl.BlockSpec((None, D_MODEL, kvw), lambda t: (layer, 0, _OFF_K // kvw))],
        out_specs=[row(kvw), row(kvw)],
        out_shape=[jax.ShapeDtypeStruct((n_rows, kvw), bf16)] * 2,
        compiler_params=pltpu.CompilerParams(
            dimension_semantics=("arbitrary",), vmem_limit_bytes=VMEM_LIMIT),
        name="projection_kv",
    )(x2, norm_w.reshape(-1, 1, D_MODEL), mod, mod, w_in)


def _kv_variants(a2):
    a, sw = a2[:, :LANES], a2[:, LANES:]
    low = lax.broadcasted_iota(jnp.int32, a.shape, 1) < HEAD_DIM
    z = jnp.zeros_like(a)
    return [jnp.where(low, a, z), jnp.where(low, z, sw),
            jnp.where(low, sw, z), jnp.where(low, z, a)]


def _mix_kernel(*refs, layer, tq, ctx_len, has_local, final):
    it = iter(refs)
    sink_ref = next(it)
    x_ref, gate_ref, ya_ref, q_ref, zb_ref, ga_ref, gb_ref = (next(it) for _ in range(7))
    if has_local:
        k_ref, kp_ref, kn_ref, v_ref, vp_ref, vn_ref = (next(it) for _ in range(6))
    kc_ref, vc_ref = next(it), next(it)
    wa_ref, wb_ref, wo_ref = (next(it) for _ in range(3))
    if final:
        fw_ref = next(it)
    o_ref = next(it)
    y_s, ya_s, yb_s, k_s, vm_s, s_s, p_s, z_s = (next(it) for _ in range(8))
    if has_local:
        bias_s = next(it)

    i = pl.program_id(1)
    n_i = pl.num_programs(1)
    nblk = tq // BLOCK
    n_ct = kc_ref.shape[0] // ctx_len
    assert has_local <= (n_ct == 1)

    def seq_of(j):
        return j // (nblk // n_ct)
    n_loc = 3 * BLOCK if has_local else 0
    nk = ctx_len + n_loc
    n_slot = vm_s.shape[0]
    rows_a = SLABS_PER_KV * BLOCK

    @pl.when(i == 0)
    def _():
        kvar = _kv_variants(kc_ref[...])
        vvar = _kv_variants(vc_ref[...])
        low = lax.broadcasted_iota(jnp.int32, (nk, LANES), 1) < HEAD_DIM
        ones_on = [jnp.where(low, 1.0, 0.0).astype(bf16), jnp.where(low, 0.0, 1.0).astype(bf16)]
        for n in range(4):
            k_s[n, 0:n_ct * ctx_len, :] = kvar[n]
        for slot in range(n_slot):
            c0 = 0 if has_local else slot * ctx_len
            for kh in range(N_KV_HEADS):
                for e in range(HEADS_PER_SLAB):
                    vm_s[slot, kh, e * nk:e * nk + ctx_len, 0:LANES] = (
                        vvar[2 * kh + e][c0:c0 + ctx_len])
                    vm_s[slot, kh, e * nk:(e + 1) * nk, LANES:] = ones_on[e]

    def local_block(ref_prev, ref_tile, ref_next, t):
        if t == 0:
            return ref_prev[...]
        if t == nblk + 1:
            return ref_next[...]
        return ref_tile[(t - 1) * BLOCK:t * BLOCK, :]

    def fill_values(j, b):
        vvar = _kv_variants(local_block(vp_ref, v_ref, vn_ref, j + b))
        for kh in range(N_KV_HEADS):
            for e in range(HEADS_PER_SLAB):
                r = e * nk + ctx_len + b * BLOCK
                vm_s[j, kh, r:r + BLOCK, 0:LANES] = vvar[2 * kh + e]

    def fill_keys(t):
        for n, a in enumerate(_kv_variants(local_block(kp_ref, k_ref, kn_ref, t))):
            k_s[n, ctx_len + t * BLOCK:ctx_len + (t + 1) * BLOCK, :] = a

    def fill_bias():
        qi = lax.broadcasted_iota(jnp.int32, (rows_a, BLOCK), 0) % BLOCK
        ci = lax.broadcasted_iota(jnp.int32, (rows_a, BLOCK), 1)
        tri_prev = jnp.where(ci >= qi, 0.0, NEG)
        tri_next = jnp.where(ci <= qi, 0.0, NEG)
        bias_s[0] = tri_prev + jnp.where(i == 0, NEG, 0.0)
        bias_s[1] = tri_prev
        bias_s[2] = tri_next
        bias_s[3] = tri_next + jnp.where(i == n_i - 1, NEG, 0.0)

    if has_local:
        for t in range(nblk + 2):
            fill_keys(t)
        fill_bias()

    nt = (((1,), (1,)), ((), ()))

    def stage_a(j, kh, e):
        u = (j * N_KV_HEADS + kh) * HEADS_PER_SLAB + e
        rows = slice(j * BLOCK, (j + 1) * BLOCK)
        q2 = jnp.concatenate(
            [q_ref[rows, (kh * SLABS_PER_KV + s) * LANES:(kh * SLABS_PER_KV + s + 1) * LANES]
             for s in range(SLABS_PER_KV)], axis=0)
        var = 2 * kh + e
        kc0 = seq_of(j) * ctx_len
        s_s[u, :, 0:ctx_len] = lax.dot_general(
            q2, k_s[var, kc0:kc0 + ctx_len, :], nt, preferred_element_type=f32)
        if has_local:
            r0 = ctx_len + j * BLOCK
            s_loc = lax.dot_general(q2, k_s[var, r0:r0 + n_loc, :], nt,
                                    preferred_element_type=f32)
            c0 = ctx_len
            s_s[u, :, c0:c0 + BLOCK] = s_loc[:, :BLOCK] + bias_s[0 if j == 0 else 1]
            s_s[u, :, c0 + BLOCK:c0 + 2 * BLOCK] = s_loc[:, BLOCK:2 * BLOCK]
            s_s[u, :, c0 + 2 * BLOCK:] = s_loc[:, 2 * BLOCK:] + bias_s[3 if j == nblk - 1 else 2]

    low_half = lax.broadcasted_iota(jnp.int32, (BLOCK, LANES), 1) < HEAD_DIM

    def stage_b(j, kh):
        jk = j * N_KV_HEADS + kh
        for s in range(SLABS_PER_KV):
            rs = slice(s * BLOCK, (s + 1) * BLOCK)
            z = []
            for e in range(HEADS_PER_SLAB):
                u = jk * HEADS_PER_SLAB + e
                sk = sink_ref[layer, (kh * SLABS_PER_KV + s) * HEADS_PER_SLAB + e] * LOG2E
                mx = s_s[u, rs, 0:LANES]
                for n in range(1, nk // LANES):
                    mx = jnp.maximum(mx, s_s[u, rs, n * LANES:(n + 1) * LANES])
                m = jnp.maximum(jnp.max(mx, axis=1, keepdims=True), sk)
                for n in range(nk // LANES):
                    col = e * nk + n * LANES
                    p_s[jk, rs, col:col + LANES] = jnp.exp2(
                        s_s[u, rs, n * LANES:(n + 1) * LANES] - m).astype(bf16)
                z.append(jnp.exp2(sk - m))
            z_s[jk, rs, :] = jnp.where(low_half, z[0], z[1])

    def stage_c(j, kh):
        jk = j * N_KV_HEADS + kh
        res = jnp.dot(p_s[jk], vm_s[j if has_local else seq_of(j), kh],
                      preferred_element_type=f32)
        attn = res[:, :LANES] / (res[:, LANES:] + z_s[jk])
        rows = slice(j * BLOCK, (j + 1) * BLOCK)
        for s in range(SLABS_PER_KV):
            c = kh * SLABS_PER_KV + s
            lanes = slice(c * LANES, (c + 1) * LANES)
            yb_s[rows, lanes] = (attn[s * BLOCK:(s + 1) * BLOCK]
                                 * zb_ref[rows, lanes].astype(f32)).astype(bf16)

    n_part = 4
    width = D_MODEL // n_part

    def conv_branch(n):
        cols = slice(n * width, (n + 1) * width)
        ya_s[:, cols] = (jnp.dot(ya_ref[...], wa_ref[:, cols], preferred_element_type=f32)
                         * ga_ref[:, cols].astype(f32))

    n_slots = (nblk + 2) * N_KV_HEADS
    conv_at = {(2 * n + 1) * n_slots // (2 * n_part): n for n in range(n_part)}
    for t in range(nblk + 2):
        for kh in range(N_KV_HEADS):
            if t * N_KV_HEADS + kh in conv_at:
                conv_branch(conv_at[t * N_KV_HEADS + kh])
            if has_local and 0 <= t - 1 < nblk:
                for b in range(3)[kh::N_KV_HEADS]:
                    fill_values(t - 1, b)
            if 0 <= t - 1 < nblk:
                stage_b(t - 1, kh)
            if 0 <= t - 2 < nblk:
                stage_c(t - 2, kh)
            if t < nblk:
                for e in range(HEADS_PER_SLAB):
                    stage_a(t, kh, e)

    for n in range(n_part):
        cols = slice(n * width, (n + 1) * width)
        y_s[:, cols] = (
            ya_s[:, cols]
            + jnp.dot(yb_s[...], wb_ref[:, cols], preferred_element_type=f32)
            * gb_ref[:, cols].astype(f32)).astype(bf16)
    y16 = y_s[...]
    for n in range(n_part):
        cols = slice(n * width, (n + 1) * width)
        o_ref[:, cols] = x_ref[:, cols] + gate_ref[:, cols] * jnp.dot(
            y16, wo_ref[:, cols], preferred_element_type=f32)
    if final:
        xn = o_ref[...]
        ms = jnp.mean(xn * xn, axis=-1, keepdims=True)
        o_ref[...] = xn * lax.rsqrt(ms + EPS) * fw_ref[...]


def _mixer(x2, layer, mod, sink, proj_out, kc, vc, wa, wb, wo, final_w, *,
           seq, tq, ctx_len, has_local, gate_row0, gate_per_batch):
    ya, q, k, v, zb, ga, gb = proj_out
    n_rows = x2.shape[0]
    seqs_per_tile = max(tq // seq, 1)
    assert kc.shape[0] * seq == n_rows * ctx_len and not (has_local and seqs_per_tile > 1)
    n_tile_b = n_rows // (seq * seqs_per_tile)
    n_i = max(seq // tq, 1)
    nblk = tq // BLOCK
    final = final_w is not None

    def row(w):
        return pl.BlockSpec((tq, w), lambda b, i: (b * n_i + i, 0))

    def halo(rows_blk, w, side):
        per_tile = tq // rows_blk
        per_seq = seq // rows_blk
        if side < 0:
            return pl.BlockSpec((rows_blk, w), lambda b, i: (
                b * per_seq + jnp.maximum(i * per_tile - 1, 0), 0))
        return pl.BlockSpec((rows_blk, w), lambda b, i: (
            b * per_seq + jnp.minimum((i + 1) * per_tile, per_seq - 1), 0))

    def whole(shape):
        return pl.BlockSpec(shape, lambda b, i: (0,) * len(shape))

    if gate_per_batch:
        gate_spec = _mod_spec(layer, lambda b, i: gate_row0 + b, 2)
    else:
        gate_spec = _mod_spec(layer, lambda b, i: gate_row0, 2)

    kvw = 2 * KV_WIDTH
    in_specs = [pl.BlockSpec(memory_space=pltpu.SMEM),
                row(D_MODEL), gate_spec, row(CONV_WIDTH),
                row(ATTN_WIDTH), row(ATTN_WIDTH), row(D_MODEL), row(D_MODEL)]
    args = [sink, x2, mod, ya, q, zb, ga, gb]
    if has_local:
        in_specs += [row(kvw), halo(BLOCK, kvw, -1), halo(BLOCK, kvw, +1)] * 2
        args += [k, k, k, v, v, v]
    ctx_spec = pl.BlockSpec((seqs_per_tile * ctx_len, kvw), lambda b, i: (b, 0))
    in_specs += [ctx_spec, ctx_spec, _layer_spec(layer, (CONV_WIDTH, D_MODEL)),
                 _layer_spec(layer, (ATTN_WIDTH, D_MODEL)), _layer_spec(layer, (D_MODEL, D_MODEL))]
    args += [kc, vc, wa, wb, wo]
    if final:
        in_specs.append(whole((1, D_MODEL)))
        args.append(final_w.reshape(1, D_MODEL))

    nk = ctx_len + (3 * BLOCK if has_local else 0)
    k_rows = ctx_len + tq + 2 * BLOCK if has_local else seqs_per_tile * ctx_len
    n_slot = nblk if has_local else seqs_per_tile
    rows_a = SLABS_PER_KV * BLOCK
    scratch = [pltpu.VMEM((tq, D_MODEL), bf16),
               pltpu.VMEM((tq, D_MODEL), f32),
               pltpu.VMEM((tq, ATTN_WIDTH), bf16),
               pltpu.VMEM((2 * N_KV_HEADS, k_rows, LANES), bf16),
               pltpu.VMEM((n_slot, N_KV_HEADS, HEADS_PER_SLAB * nk, 2 * LANES), bf16),
               pltpu.VMEM((nblk * N_KV_HEADS * HEADS_PER_SLAB, rows_a, nk), f32),
               pltpu.VMEM((nblk * N_KV_HEADS, rows_a, HEADS_PER_SLAB * nk), bf16),
               pltpu.VMEM((nblk * N_KV_HEADS, rows_a, LANES), f32)]
    if has_local:
        scratch += [pltpu.VMEM((4, rows_a, BLOCK), f32)]

    return pl.pallas_call(
        functools.partial(_mix_kernel, layer=layer, tq=tq, ctx_len=ctx_len,
                          has_local=has_local, final=final),
        grid=(n_tile_b, n_i),
        in_specs=in_specs,
        out_specs=row(D_MODEL),
        out_shape=jax.ShapeDtypeStruct((n_rows, D_MODEL), f32),
        scratch_shapes=scratch,
        compiler_params=pltpu.CompilerParams(
            dimension_semantics=("arbitrary", "arbitrary"), vmem_limit_bytes=VMEM_LIMIT),
        name="mixer_latent" if has_local else "mixer_context",
    )(*args)


def _rope_tables(seq):
    n_rows = seq // GRID_W
    lane = np.arange(LANES)
    lane_freq = (jnp.asarray(ROPE_THETA, f32)
                 ** (-jnp.asarray(lane % ROPE_FREQS, f32) / ROPE_FREQS))[None, :]
    use_col = ((lane % HEAD_DIM) >= HEAD_DIM // 2)[None, None, :]
    upper = ((lane % (2 * ROPE_FREQS)) >= ROPE_FREQS)[None, None, :]
    ang_r = jnp.arange(n_rows, dtype=f32)[:, None] * lane_freq
    ang_c = jnp.arange(GRID_W, dtype=f32)[:, None] * lane_freq
    shape = (n_rows, GRID_W, LANES)
    cos = jnp.where(use_col, jnp.cos(ang_c)[None], jnp.cos(ang_r)[:, None]).reshape(seq, LANES)
    sin = jnp.broadcast_to(jnp.where(use_col, jnp.sin(ang_c)[None], jnp.sin(ang_r)[:, None]),
                           shape)
    zero = jnp.zeros(shape, f32)
    sin_lo = jnp.where(upper, zero, -sin).reshape(seq, LANES)
    sin_hi = jnp.where(upper, sin, zero).reshape(seq, LANES)
    return cos, sin_lo, sin_hi


def kernel(x, c, ctx, c_ctx, norm_w, w_mod, b_mod, w_in, conv_w, w_a_out, w_b_out,
           attn_sink, w_o, final_norm_w):
    batch, seq, _ = x.shape
    ctx_len = ctx.shape[1]
    depth = w_in.shape[0]
    assert batch + 1 <= MOD_ROWS and seq % 512 == 0 and ctx_len % BLOCK == 0

    c_all = jnp.zeros((MOD_ROWS, D_MODEL), f32).at[:batch].set(c).at[batch].set(c_ctx)
    mod = _modulation(c_all, w_mod, b_mod)
    rope_tabs = _rope_tables(seq)

    w_in_b = w_in.astype(bf16)
    wa_b, wb_b, wo_b = w_a_out.astype(bf16), w_b_out.astype(bf16), w_o.astype(bf16)

    x2 = x.reshape(batch * seq, D_MODEL)
    c2 = ctx.reshape(batch * ctx_len, D_MODEL)
    for l in range(depth):
        last = l == depth - 1
        if last:
            kc, vc = _kv_projection(c2, l, norm_w, mod, w_in_b, tm=4 * ctx_len, mod_row=batch)
        else:
            pc = _projection(c2, l, norm_w, mod, w_in_b, conv_w, None, subs=(ctx_len,) * 4,
                             seq=ctx_len, mod_row0=batch, mod_per_seq=False)
            kc, vc = pc[2], pc[3]
        px = _projection(x2, l, norm_w, mod, w_in_b, conv_w, rope_tabs,
                         subs=(512, 512), seq=seq, mod_row0=0, mod_per_seq=True)
        x2 = _mixer(x2, l, mod, attn_sink, px, kc, vc, wa_b, wb_b, wo_b,
                    final_norm_w if last else None,
                    seq=seq, tq=512, ctx_len=ctx_len, has_local=True, gate_row0=0,
                    gate_per_batch=True)
        if not last:
            c2 = _mixer(c2, l, mod, attn_sink, pc, kc, vc, wa_b, wb_b, wo_b, None,
                        seq=ctx_len, tq=ctx_len, ctx_len=ctx_len, has_local=False,
                        gate_row0=batch, gate_per_batch=False)
    return x2.reshape(batch, seq, D_MODEL)
```

```python
import functools
import math

import numpy as np
import jax
import jax.numpy as jnp
from jax import lax
from jax.experimental import pallas as pl
from jax.experimental.pallas import tpu as pltpu

D_MODEL = 1024
GRID_W = 64
CONV_WIDTH = 512
CONV_K = 3
N_HEADS = 8
N_KV_HEADS = 2
HEAD_DIM = 64
ATTN_WIDTH = N_HEADS * HEAD_DIM
KV_WIDTH = N_KV_HEADS * HEAD_DIM
BLOCK = 128
ROPE_THETA = 10000.0
ROPE_FREQS = HEAD_DIM // 4
EPS = 1e-6
NEG = -1e30
LOG2E = math.log2(math.e)

LANES = 128
SUBLANES = 8
MOD_ROWS = 24
VMEM_LIMIT = 56 * 1024 * 1024
N_SLAB = ATTN_WIDTH // LANES
HEADS_PER_SLAB = LANES // HEAD_DIM
SLABS_PER_KV = N_SLAB // N_KV_HEADS

_OFF_B = 0
_OFF_C = _OFF_B + CONV_WIDTH
_OFF_U = _OFF_C + CONV_WIDTH
_OFF_ZA = _OFF_U + CONV_WIDTH
_OFF_Q = _OFF_ZA + CONV_WIDTH
_OFF_K = _OFF_Q + ATTN_WIDTH
_OFF_V = _OFF_K + KV_WIDTH
_OFF_ZB = _OFF_V + KV_WIDTH
_OFF_GA = _OFF_ZB + ATTN_WIDTH
_OFF_GB = _OFF_GA + D_MODEL
IN_COLS = _OFF_GB + D_MODEL

f32 = jnp.float32
bf16 = jnp.bfloat16


def _sigmoid(x):
    return 0.5 * jnp.tanh(0.5 * x) + 0.5


def _silu(x):
    return x * _sigmoid(x)


def _mod_kernel(c_ref, w_ref, b_ref, o_ref):
    s = _silu(c_ref[...]).astype(bf16)
    res = jnp.dot(s, w_ref[...].astype(bf16), preferred_element_type=f32) + b_ref[...]
    for r in range(MOD_ROWS):
        o_ref[r] = res[r:r + 1, :]


def _modulation(c_all, w_mod, b_mod):
    depth = w_mod.shape[0]
    return pl.pallas_call(
        _mod_kernel,
        grid=(depth, 3),
        in_specs=[
            pl.BlockSpec((MOD_ROWS, D_MODEL), lambda l, j: (0, 0)),
            pl.BlockSpec((None, D_MODEL, D_MODEL), lambda l, j: (l, 0, j)),
            pl.BlockSpec((None, 1, D_MODEL), lambda l, j: (l, 0, j)),
        ],
        out_specs=pl.BlockSpec((None, MOD_ROWS, None, 1, D_MODEL), lambda l, j: (l, 0, j, 0, 0)),
        out_shape=jax.ShapeDtypeStruct((depth, MOD_ROWS, 3, 1, D_MODEL), f32),
        name="modulation",
    )(c_all, w_mod, b_mod.reshape(depth, 1, 3 * D_MODEL))


def _rope(p, cos, sin_lo, sin_hi):
    outs = []
    for j in range(p.shape[1] // LANES):
        x = p[:, j * LANES:(j + 1) * LANES]
        outs.append(x * cos
                    + pltpu.roll(x, LANES - ROPE_FREQS, 1) * sin_lo
                    + pltpu.roll(x, ROPE_FREQS, 1) * sin_hi)
    return outs[0] if len(outs) == 1 else jnp.concatenate(outs, axis=1)


def _with_swapped_halves(a):
    return jnp.concatenate([a, pltpu.roll(a, HEAD_DIM, 1)], axis=1)


def _norm_mod(x, gain, shift):
    ms = jnp.mean(x * x, axis=-1, keepdims=True)
    return (x * lax.rsqrt(ms + EPS) * gain + shift).astype(bf16)


def _kv_proj_kernel(x_ref, nw_ref, shift_ref, scale_ref, w_ref, k_ref, v_ref):
    gain, shift = nw_ref[...] * (1.0 + scale_ref[...]), shift_ref[...]
    kv = jnp.dot(_norm_mod(x_ref[...], gain, shift), w_ref[...], preferred_element_type=f32)
    k_ref[...] = _with_swapped_halves(kv[:, :KV_WIDTH]).astype(bf16)
    v_ref[...] = _with_swapped_halves(kv[:, KV_WIDTH:]).astype(bf16)


def _proj_kernel(*refs, rope, seq, subs):
    it = iter(refs)
    x_ref, xp_ref, xn_ref, nw_ref, shift_ref, scale_ref, w_ref, cw_ref = (
        next(it) for _ in range(8))
    if rope:
        cos_ref, slo_ref, shi_ref = (next(it) for _ in range(3))
    ya_ref, q_ref, k_ref, v_ref, zb_ref, ga_ref, gb_ref = (next(it) for _ in range(7))
    h_s, cu_s = next(it), next(it)
    tm = x_ref.shape[0]
    assert sum(subs) == tm and (seq % tm == 0 or tm % seq == 0)
    halo = 2 * SUBLANES
    gain, shift = nw_ref[...] * (1.0 + scale_ref[...]), shift_ref[...]
    tiles_per_seq = max(seq // tm, 1)
    t_in_seq = pl.program_id(0) % tiles_per_seq
    cw = cw_ref[...]
    zero = jnp.zeros((SUBLANES, CONV_WIDTH), f32)

    def buffers(n):
        s0 = sum(subs[:n]) + n * halo
        return h_s.at[s0:s0 + subs[n] + halo], cu_s.at[s0:s0 + subs[n] + halo]

    for n, sub in enumerate(subs):
        r0 = sum(subs[:n])
        rows = slice(r0, r0 + sub)
        h, cu_b = buffers(n)
        before = xp_ref[...] if r0 == 0 else x_ref[r0 - SUBLANES:r0, :]
        after = xn_ref[...] if r0 + sub == tm else x_ref[r0 + sub:r0 + sub + SUBLANES, :]
        h[0:sub, :] = _norm_mod(x_ref[rows, :], gain, shift)
        h[sub:sub + halo, :] = _norm_mod(jnp.concatenate([before, after], axis=0), gain, shift)

        def proj(off, width, n_rows=sub):
            return jnp.dot(h[0:n_rows, :], w_ref[:, off:off + width],
                           preferred_element_type=f32)

        cu = proj(_OFF_C, CONV_WIDTH, sub + halo) * proj(_OFF_U, CONV_WIDTH, sub + halo)
        cu_before, cu_after = cu[sub:sub + SUBLANES], cu[sub + SUBLANES:]
        if r0 % seq == 0:
            cu_before = jnp.where(t_in_seq > 0, cu_before, zero) if r0 == 0 else zero
        if (r0 + sub) % seq == 0 or r0 + sub == tm:
            cu_after = (jnp.where(t_in_seq < tiles_per_seq - 1, cu_after, zero)
                        if r0 + sub == tm else zero)
        cu_b[0:SUBLANES, :] = cu_before
        cu_b[SUBLANES:SUBLANES + sub, :] = cu[0:sub]
        cu_b[SUBLANES + sub:, :] = cu_after
        conv = (cu_b[SUBLANES - 1:SUBLANES - 1 + sub, :] * cw[0:1]
                + cu_b[SUBLANES:SUBLANES + sub, :] * cw[1:2]
                + cu_b[SUBLANES + 1:SUBLANES + 1 + sub, :] * cw[2:3])
        ya_ref[rows, :] = (proj(_OFF_B, CONV_WIDTH) * conv
                           * _silu(proj(_OFF_ZA, CONV_WIDTH))).astype(bf16)

        q = proj(_OFF_Q, ATTN_WIDTH)
        kv = proj(_OFF_K, 2 * KV_WIDTH)
        k = kv[:, :KV_WIDTH]
        if rope:
            cos, slo, shi = cos_ref[rows, :], slo_ref[rows, :], shi_ref[rows, :]
            q = _rope(q, cos, slo, shi)
            k = _rope(k, cos, slo, shi)
        q_ref[rows, :] = (q * (HEAD_DIM ** -0.5 * LOG2E)).astype(bf16)
        k_ref[rows, :] = _with_swapped_halves(k).astype(bf16)
        v_ref[rows, :] = _with_swapped_halves(kv[:, KV_WIDTH:]).astype(bf16)
        zb_ref[rows, :] = _silu(proj(_OFF_ZB, ATTN_WIDTH)).astype(bf16)
        part = 2 * LANES
        for c in range(D_MODEL // part):
            ga_ref[rows, c * part:(c + 1) * part] = _sigmoid(
                proj(_OFF_GA + c * part, part)).astype(bf16)
        for c in range(D_MODEL // part):
            gb_ref[rows, c * part:(c + 1) * part] = _sigmoid(
                proj(_OFF_GB + c * part, part)).astype(bf16)


def _mod_spec(layer, row_of, which):
    return pl.BlockSpec((None, None, None, 1, D_MODEL),
                        lambda *g: (layer, row_of(*g), which, 0, 0))


def _layer_spec(layer, shape):
    return pl.BlockSpec((None,) + tuple(shape), lambda *g: (layer,) + (0,) * len(shape))


def _projection(x2, layer, norm_w, mod, w_in, conv_w, rope_tabs, *, subs, seq, mod_row0,
                mod_per_seq):
    tm = sum(subs)
    scratch_rows = tm + len(subs) * 2 * SUBLANES
    n_rows = x2.shape[0]
    assert seq % tm == 0 or (tm % seq == 0 and not mod_per_seq and rope_tabs is None)
    tiles_per_seq = max(seq // tm, 1)
    halo_per_tile = tm // SUBLANES
    n_halo = n_rows // SUBLANES
    if mod_per_seq:
        mod_row = lambda t: mod_row0 + t // tiles_per_seq
    else:
        mod_row = lambda t: mod_row0
    row = lambda w: pl.BlockSpec((tm, w), lambda t: (t, 0))
    in_specs = [
        row(D_MODEL),
        pl.BlockSpec((SUBLANES, D_MODEL),
                     lambda t: (jnp.maximum(t * halo_per_tile - 1, 0), 0)),
        pl.BlockSpec((SUBLANES, D_MODEL),
                     lambda t: (jnp.minimum((t + 1) * halo_per_tile, n_halo - 1), 0)),
        _layer_spec(layer, (1, D_MODEL)),
        _mod_spec(layer, mod_row, 0),
        _mod_spec(layer, mod_row, 1),
        _layer_spec(layer, (D_MODEL, IN_COLS)),
        _layer_spec(layer, (CONV_K, CONV_WIDTH)),
    ]
    args = [x2, x2, x2, norm_w.reshape(-1, 1, D_MODEL), mod, mod, w_in, conv_w]
    if rope_tabs is not None:
        in_specs += [pl.BlockSpec((tm, LANES), lambda t: (t % tiles_per_seq, 0))] * 3
        args += list(rope_tabs)
    widths = (CONV_WIDTH, ATTN_WIDTH, 2 * KV_WIDTH, 2 * KV_WIDTH, ATTN_WIDTH, D_MODEL, D_MODEL)
    return pl.pallas_call(
        functools.partial(_proj_kernel, rope=rope_tabs is not None,
                          seq=seq, subs=tuple(subs)),
        grid=(n_rows // tm,),
        in_specs=in_specs,
        out_specs=[row(w) for w in widths],
        out_shape=[jax.ShapeDtypeStruct((n_rows, w), bf16) for w in widths],
        scratch_shapes=[pltpu.VMEM((scratch_rows, D_MODEL), bf16),
                        pltpu.VMEM((scratch_rows, CONV_WIDTH), f32)],
        compiler_params=pltpu.CompilerParams(
            dimension_semantics=("arbitrary",), vmem_limit_bytes=VMEM_LIMIT),
        name="projection_rope" if rope_tabs is not None else "projection",
    )(*args)


def _kv_projection(x2, layer, norm_w, mod, w_in, *, tm, mod_row):
    n_rows = x2.shape[0]
    kvw = 2 * KV_WIDTH
    assert _OFF_K % kvw == 0 and _OFF_V == _OFF_K + KV_WIDTH
    row = lambda w: pl.BlockSpec((tm, w), lambda t: (t, 0))
    return pl.pallas_call(
        _kv_proj_kernel,
        grid=(n_rows // tm,),
        in_specs=[row(D_MODEL),
                  _layer_spec(layer, (1, D_MODEL)),
                  _mod_spec(layer, lambda t: mod_row, 0),
                  _mod_spec(layer, lambda t: mod_row, 1),
                  pl.BlockSpec((None, D_MODEL, kvw), lambda t: (layer, 0, _OFF_K // kvw))],
        out_specs=[row(kvw), row(kvw)],
        out_shape=[jax.ShapeDtypeStruct((n_rows, kvw), bf16)] * 2,
        compiler_params=pltpu.CompilerParams(
            dimension_semantics=("arbitrary",), vmem_limit_bytes=VMEM_LIMIT),
        name="projection_kv",
    )(x2, norm_w.reshape(-1, 1, D_MODEL), mod, mod, w_in)


def _kv_variants(a2):
    a, sw = a2[:, :LANES], a2[:, LANES:]
    low = lax.broadcasted_iota(jnp.int32, a.shape, 1) < HEAD_DIM
    z = jnp.zeros_like(a)
    return [jnp.where(low, a, z), jnp.where(low, z, sw),
            jnp.where(low, sw, z), jnp.where(low, z, a)]


def _mix_kernel(*refs, layer, tq, ctx_len, has_local, final):
    it = iter(refs)
    sink_ref = next(it)
    x_ref, gate_ref, ya_ref, q_ref, zb_ref, ga_ref, gb_ref = (next(it) for _ in range(7))
    if has_local:
        k_ref, kp_ref, kn_ref, v_ref, vp_ref, vn_ref = (next(it) for _ in range(6))
    kc_ref, vc_ref = next(it), next(it)
    wa_ref, wb_ref, wo_ref = (next(it) for _ in range(3))
    if final:
        fw_ref = next(it)
    o_ref = next(it)
    y_s, ya_s, yb_s, k_s, vm_s, s_s, p_s, z_s = (next(it) for _ in range(8))
    if has_local:
        bias_s = next(it)

    i = pl.program_id(1)
    n_i = pl.num_programs(1)
    nblk = tq // BLOCK
    n_ct = kc_ref.shape[0] // ctx_len
    assert has_local <= (n_ct == 1)

    def seq_of(j):
        return j // (nblk // n_ct)
    n_loc = 3 * BLOCK if has_local else 0
    nk = ctx_len + n_loc
    n_slot = vm_s.shape[0]
    ring = p_s.shape[0] // N_KV_HEADS
    rows_a = SLABS_PER_KV * BLOCK

    @pl.when(i == 0)
    def _():
        kvar = _kv_variants(kc_ref[...])
        vvar = _kv_variants(vc_ref[...])
        low = lax.broadcasted_iota(jnp.int32, (nk, LANES), 1) < HEAD_DIM
        ones_on = [jnp.where(low, 1.0, 0.0).astype(bf16), jnp.where(low, 0.0, 1.0).astype(bf16)]
        for n in range(4):
            k_s[n, 0:n_ct * ctx_len, :] = kvar[n]
        for slot in range(n_slot):
            c0 = 0 if has_local else slot * ctx_len
            for kh in range(N_KV_HEADS):
                for e in range(HEADS_PER_SLAB):
                    vm_s[slot, kh, e * nk:e * nk + ctx_len, 0:LANES] = (
                        vvar[2 * kh + e][c0:c0 + ctx_len])
                    vm_s[slot, kh, e * nk:(e + 1) * nk, LANES:] = ones_on[e]

    def local_block(ref_prev, ref_tile, ref_next, t):
        if t == 0:
            return ref_prev[...]
        if t == nblk + 1:
            return ref_next[...]
        return ref_tile[(t - 1) * BLOCK:t * BLOCK, :]

    def fill_values(j, b):
        vvar = _kv_variants(local_block(vp_ref, v_ref, vn_ref, j + b))
        for kh in range(N_KV_HEADS):
            for e in range(HEADS_PER_SLAB):
                r = e * nk + ctx_len + b * BLOCK
                vm_s[j % n_slot, kh, r:r + BLOCK, 0:LANES] = vvar[2 * kh + e]

    def fill_keys(t):
        for n, a in enumerate(_kv_variants(local_block(kp_ref, k_ref, kn_ref, t))):
            k_s[n, ctx_len + t * BLOCK:ctx_len + (t + 1) * BLOCK, :] = a

    def fill_bias():
        qi = lax.broadcasted_iota(jnp.int32, (rows_a, BLOCK), 0) % BLOCK
        ci = lax.broadcasted_iota(jnp.int32, (rows_a, BLOCK), 1)
        tri_prev = jnp.where(ci >= qi, 0.0, NEG)
        tri_next = jnp.where(ci <= qi, 0.0, NEG)
        bias_s[0] = tri_prev + jnp.where(i == 0, NEG, 0.0)
        bias_s[1] = tri_prev
        bias_s[2] = tri_next
        bias_s[3] = tri_next + jnp.where(i == n_i - 1, NEG, 0.0)

    if has_local:
        for t in range(nblk + 2):
            fill_keys(t)
        fill_bias()

    nt = (((1,), (1,)), ((), ()))

    def stage_a(j, kh, e):
        u = ((j % ring) * N_KV_HEADS + kh) * HEADS_PER_SLAB + e
        rows = slice(j * BLOCK, (j + 1) * BLOCK)
        q2 = jnp.concatenate(
            [q_ref[rows, (kh * SLABS_PER_KV + s) * LANES:(kh * SLABS_PER_KV + s + 1) * LANES]
             for s in range(SLABS_PER_KV)], axis=0)
        var = 2 * kh + e
        kc0 = seq_of(j) * ctx_len
        s_s[u, :, 0:ctx_len] = lax.dot_general(
            q2, k_s[var, kc0:kc0 + ctx_len, :], nt, preferred_element_type=f32)
        if has_local:
            r0 = ctx_len + j * BLOCK
            s_loc = lax.dot_general(q2, k_s[var, r0:r0 + n_loc, :], nt,
                                    preferred_element_type=f32)
            c0 = ctx_len
            s_s[u, :, c0:c0 + BLOCK] = s_loc[:, :BLOCK] + bias_s[0 if j == 0 else 1]
            s_s[u, :, c0 + BLOCK:c0 + 2 * BLOCK] = s_loc[:, BLOCK:2 * BLOCK]
            s_s[u, :, c0 + 2 * BLOCK:] = s_loc[:, 2 * BLOCK:] + bias_s[3 if j == nblk - 1 else 2]

    low_half = lax.broadcasted_iota(jnp.int32, (BLOCK, LANES), 1) < HEAD_DIM

    def stage_b(j, kh):
        jk = (j % ring) * N_KV_HEADS + kh
        for s in range(SLABS_PER_KV):
            rs = slice(s * BLOCK, (s + 1) * BLOCK)
            z = []
            for e in range(HEADS_PER_SLAB):
                u = jk * HEADS_PER_SLAB + e
                sk = sink_ref[layer, (kh * SLABS_PER_KV + s) * HEADS_PER_SLAB + e] * LOG2E
                mx = s_s[u, rs, 0:LANES]
                for n in range(1, nk // LANES):
                    mx = jnp.maximum(mx, s_s[u, rs, n * LANES:(n + 1) * LANES])
                m = jnp.maximum(jnp.max(mx, axis=1, keepdims=True), sk)
                for n in range(nk // LANES):
                    col = e * nk + n * LANES
                    p_s[jk, rs, col:col + LANES] = jnp.exp2(
                        s_s[u, rs, n * LANES:(n + 1) * LANES] - m).astype(bf16)
                z.append(jnp.exp2(sk - m))
            z_s[jk, rs, :] = jnp.where(low_half, z[0], z[1])

    def stage_c(j, kh):
        jk = (j % ring) * N_KV_HEADS + kh
        res = jnp.dot(p_s[jk], vm_s[j % n_slot if has_local else seq_of(j), kh],
                      preferred_element_type=f32)
        attn = res[:, :LANES] / (res[:, LANES:] + z_s[jk])
        rows = slice(j * BLOCK, (j + 1) * BLOCK)
        for s in range(SLABS_PER_KV):
            c = kh * SLABS_PER_KV + s
            lanes = slice(c * LANES, (c + 1) * LANES)
            yb_s[rows, lanes] = (attn[s * BLOCK:(s + 1) * BLOCK]
                                 * zb_ref[rows, lanes].astype(f32)).astype(bf16)

    n_part = 4
    width = D_MODEL // n_part

    def conv_branch(n):
        cols = slice(n * width, (n + 1) * width)
        ya_s[:, cols] = (jnp.dot(ya_ref[...], wa_ref[:, cols], preferred_element_type=f32)
                         * ga_ref[:, cols].astype(f32))

    n_slots = (nblk + 2) * N_KV_HEADS
    conv_at = {(2 * n + 1) * n_slots // (2 * n_part): n for n in range(n_part)}
    for t in range(nblk + 2):
        for kh in range(N_KV_HEADS):
            if t * N_KV_HEADS + kh in conv_at:
                conv_branch(conv_at[t * N_KV_HEADS + kh])
            if has_local and 0 <= t - 1 < nblk:
                for b in range(3)[kh::N_KV_HEADS]:
                    fill_values(t - 1, b)
            if 0 <= t - 1 < nblk:
                stage_b(t - 1, kh)
            if 0 <= t - 2 < nblk:
                stage_c(t - 2, kh)
            if t < nblk:
                for e in range(HEADS_PER_SLAB):
                    stage_a(t, kh, e)

    for n in range(n_part):
        cols = slice(n * width, (n + 1) * width)
        y_s[:, cols] = (
            ya_s[:, cols]
            + jnp.dot(yb_s[...], wb_ref[:, cols], preferred_element_type=f32)
            * gb_ref[:, cols].astype(f32)).astype(bf16)
    y16 = y_s[...]
    for n in range(n_part):
        cols = slice(n * width, (n + 1) * width)
        o_ref[:, cols] = x_ref[:, cols] + gate_ref[:, cols] * jnp.dot(
            y16, wo_ref[:, cols], preferred_element_type=f32)
    if final:
        xn = o_ref[...]
        ms = jnp.mean(xn * xn, axis=-1, keepdims=True)
        o_ref[...] = xn * lax.rsqrt(ms + EPS) * fw_ref[...]


def _mixer(x2, layer, mod, sink, proj_out, kc, vc, wa, wb, wo, final_w, *,
           seq, tq, ctx_len, has_local, gate_row0, gate_per_batch):
    ya, q, k, v, zb, ga, gb = proj_out
    n_rows = x2.shape[0]
    seqs_per_tile = max(tq // seq, 1)
    assert kc.shape[0] * seq == n_rows * ctx_len and not (has_local and seqs_per_tile > 1)
    n_tile_b = n_rows // (seq * seqs_per_tile)
    n_i = max(seq // tq, 1)
    nblk = tq // BLOCK
    final = final_w is not None

    def row(w):
        return pl.BlockSpec((tq, w), lambda b, i: (b * n_i + i, 0))

    def halo(rows_blk, w, side):
        per_tile = tq // rows_blk
        per_seq = seq // rows_blk
        if side < 0:
            return pl.BlockSpec((rows_blk, w), lambda b, i: (
                b * per_seq + jnp.maximum(i * per_tile - 1, 0), 0))
        return pl.BlockSpec((rows_blk, w), lambda b, i: (
            b * per_seq + jnp.minimum((i + 1) * per_tile, per_seq - 1), 0))

    def whole(shape):
        return pl.BlockSpec(shape, lambda b, i: (0,) * len(shape))

    if gate_per_batch:
        gate_spec = _mod_spec(layer, lambda b, i: gate_row0 + b, 2)
    else:
        gate_spec = _mod_spec(layer, lambda b, i: gate_row0, 2)

    kvw = 2 * KV_WIDTH
    in_specs = [pl.BlockSpec(memory_space=pltpu.SMEM),
                row(D_MODEL), gate_spec, row(CONV_WIDTH),
                row(ATTN_WIDTH), row(ATTN_WIDTH), row(D_MODEL), row(D_MODEL)]
    args = [sink, x2, mod, ya, q, zb, ga, gb]
    if has_local:
        in_specs += [row(kvw), halo(BLOCK, kvw, -1), halo(BLOCK, kvw, +1)] * 2
        args += [k, k, k, v, v, v]
    ctx_spec = pl.BlockSpec((seqs_per_tile * ctx_len, kvw), lambda b, i: (b, 0))
    in_specs += [ctx_spec, ctx_spec, _layer_spec(layer, (CONV_WIDTH, D_MODEL)),
                 _layer_spec(layer, (ATTN_WIDTH, D_MODEL)), _layer_spec(layer, (D_MODEL, D_MODEL))]
    args += [kc, vc, wa, wb, wo]
    if final:
        in_specs.append(whole((1, D_MODEL)))
        args.append(final_w.reshape(1, D_MODEL))

    nk = ctx_len + (3 * BLOCK if has_local else 0)
    k_rows = ctx_len + tq + 2 * BLOCK if has_local else seqs_per_tile * ctx_len
    ring = min(nblk, 3)
    n_slot = nblk if has_local else seqs_per_tile
    rows_a = SLABS_PER_KV * BLOCK
    scratch = [pltpu.VMEM((tq, D_MODEL), bf16),
               pltpu.VMEM((tq, D_MODEL), f32),
               pltpu.VMEM((tq, ATTN_WIDTH), bf16),
               pltpu.VMEM((2 * N_KV_HEADS, k_rows, LANES), bf16),
               pltpu.VMEM((n_slot, N_KV_HEADS, HEADS_PER_SLAB * nk, 2 * LANES), bf16),
               pltpu.VMEM((ring * N_KV_HEADS * HEADS_PER_SLAB, rows_a, nk), f32),
               pltpu.VMEM((ring * N_KV_HEADS, rows_a, HEADS_PER_SLAB * nk), bf16),
               pltpu.VMEM((ring * N_KV_HEADS, rows_a, LANES), f32)]
    if has_local:
        scratch += [pltpu.VMEM((4, rows_a, BLOCK), f32)]

    return pl.pallas_call(
        functools.partial(_mix_kernel, layer=layer, tq=tq, ctx_len=ctx_len,
                          has_local=has_local, final=final),
        grid=(n_tile_b, n_i),
        in_specs=in_specs,
        out_specs=row(D_MODEL),
        out_shape=jax.ShapeDtypeStruct((n_rows, D_MODEL), f32),
        scratch_shapes=scratch,
        compiler_params=pltpu.CompilerParams(
            dimension_semantics=("arbitrary", "arbitrary"), vmem_limit_bytes=VMEM_LIMIT),
        name="mixer_latent" if has_local else "mixer_context",
    )(*args)


def _rope_tables(seq):
    n_rows = seq // GRID_W
    lane = np.arange(LANES)
    lane_freq = (jnp.asarray(ROPE_THETA, f32)
                 ** (-jnp.asarray(lane % ROPE_FREQS, f32) / ROPE_FREQS))[None, :]
    use_col = ((lane % HEAD_DIM) >= HEAD_DIM // 2)[None, None, :]
    upper = ((lane % (2 * ROPE_FREQS)) >= ROPE_FREQS)[None, None, :]
    ang_r = jnp.arange(n_rows, dtype=f32)[:, None] * lane_freq
    ang_c = jnp.arange(GRID_W, dtype=f32)[:, None] * lane_freq
    shape = (n_rows, GRID_W, LANES)
    cos = jnp.where(use_col, jnp.cos(ang_c)[None], jnp.cos(ang_r)[:, None]).reshape(seq, LANES)
    sin = jnp.broadcast_to(jnp.where(use_col, jnp.sin(ang_c)[None], jnp.sin(ang_r)[:, None]),
                           shape)
    zero = jnp.zeros(shape, f32)
    sin_lo = jnp.where(upper, zero, -sin).reshape(seq, LANES)
    sin_hi = jnp.where(upper, sin, zero).reshape(seq, LANES)
    return cos, sin_lo, sin_hi


def kernel(x, c, ctx, c_ctx, norm_w, w_mod, b_mod, w_in, conv_w, w_a_out, w_b_out,
           attn_sink, w_o, final_norm_w):
    batch, seq, _ = x.shape
    ctx_len = ctx.shape[1]
    depth = w_in.shape[0]
    assert batch + 1 <= MOD_ROWS and seq % 512 == 0 and ctx_len % BLOCK == 0

    c_all = jnp.zeros((MOD_ROWS, D_MODEL), f32).at[:batch].set(c).at[batch].set(c_ctx)
    mod = _modulation(c_all, w_mod, b_mod)
    rope_tabs = _rope_tables(seq)

    w_in_b = w_in.astype(bf16)
    wa_b, wb_b, wo_b = w_a_out.astype(bf16), w_b_out.astype(bf16), w_o.astype(bf16)

    x2 = x.reshape(batch * seq, D_MODEL)
    c2 = ctx.reshape(batch * ctx_len, D_MODEL)
    for l in range(depth):
        last = l == depth - 1
        if last:
            kc, vc = _kv_projection(c2, l, norm_w, mod, w_in_b, tm=4 * ctx_len, mod_row=batch)
        else:
            pc = _projection(c2, l, norm_w, mod, w_in_b, conv_w, None, subs=(ctx_len,) * 4,
                             seq=ctx_len, mod_row0=batch, mod_per_seq=False)
            kc, vc = pc[2], pc[3]
        px = _projection(x2, l, norm_w, mod, w_in_b, conv_w, rope_tabs,
                         subs=(512, 512), seq=seq, mod_row0=0, mod_per_seq=True)
        x2 = _mixer(x2, l, mod, attn_sink, px, kc, vc, wa_b, wb_b, wo_b,
                    final_norm_w if last else None,
                    seq=seq, tq=512, ctx_len=ctx_len, has_local=True, gate_row0=0,
                    gate_per_batch=True)
        if not last:
            c2 = _mixer(c2, l, mod, attn_sink, pc, kc, vc, wa_b, wb_b, wo_b, None,
                        seq=ctx_len, tq=ctx_len, ctx_len=ctx_len, has_local=False,
                        gate_row0=batch, gate_per_batch=False)
    return x2.reshape(batch, seq, D_MODEL)
```

```python
import functools
import math

import numpy as np
import jax
import jax.numpy as jnp
from jax import lax
from jax.experimental import pallas as pl
from jax.experimental.pallas import tpu as pltpu

D_MODEL = 1024
GRID_W = 64
CONV_WIDTH = 512
CONV_K = 3
N_HEADS = 8
N_KV_HEADS = 2
HEAD_DIM = 64
ATTN_WIDTH = N_HEADS * HEAD_DIM
KV_WIDTH = N_KV_HEADS * HEAD_DIM
BLOCK = 128
ROPE_THETA = 10000.0
ROPE_FREQS = HEAD_DIM // 4
EPS = 1e-6
NEG = -1e30
LOG2E = math.log2(math.e)

LANES = 128
SUBLANES = 8
MOD_ROWS = 24
VMEM_LIMIT = 56 * 1024 * 1024
N_SLAB = ATTN_WIDTH // LANES
HEADS_PER_SLAB = LANES // HEAD_DIM
SLABS_PER_KV = N_SLAB // N_KV_HEADS

_OFF_B = 0
_OFF_C = _OFF_B + CONV_WIDTH
_OFF_U = _OFF_C + CONV_WIDTH
_OFF_ZA = _OFF_U + CONV_WIDTH
_OFF_Q = _OFF_ZA + CONV_WIDTH
_OFF_K = _OFF_Q + ATTN_WIDTH
_OFF_V = _OFF_K + KV_WIDTH
_OFF_ZB = _OFF_V + KV_WIDTH
_OFF_GA = _OFF_ZB + ATTN_WIDTH
_OFF_GB = _OFF_GA + D_MODEL
IN_COLS = _OFF_GB + D_MODEL

f32 = jnp.float32
bf16 = jnp.bfloat16


def _sigmoid(x):
    return 0.5 * jnp.tanh(0.5 * x) + 0.5


def _silu(x):
    return x * _sigmoid(x)


def _mod_kernel(c_ref, w_ref, b_ref, o_ref):
    s = _silu(c_ref[...]).astype(bf16)
    res = jnp.dot(s, w_ref[...].astype(bf16), preferred_element_type=f32) + b_ref[...]
    for r in range(MOD_ROWS):
        o_ref[r] = res[r:r + 1, :]


def _modulation(c_all, w_mod, b_mod):
    depth = w_mod.shape[0]
    return pl.pallas_call(
        _mod_kernel,
        grid=(depth, 3),
        in_specs=[
            pl.BlockSpec((MOD_ROWS, D_MODEL), lambda l, j: (0, 0)),
            pl.BlockSpec((None, D_MODEL, D_MODEL), lambda l, j: (l, 0, j)),
            pl.BlockSpec((None, 1, D_MODEL), lambda l, j: (l, 0, j)),
        ],
        out_specs=pl.BlockSpec((None, MOD_ROWS, None, 1, D_MODEL), lambda l, j: (l, 0, j, 0, 0)),
        out_shape=jax.ShapeDtypeStruct((depth, MOD_ROWS, 3, 1, D_MODEL), f32),
        name="modulation",
    )(c_all, w_mod, b_mod.reshape(depth, 1, 3 * D_MODEL))


def _rope(p, cos, sin_lo, sin_hi):
    outs = []
    for j in range(p.shape[1] // LANES):
        x = p[:, j * LANES:(j + 1) * LANES]
        outs.append(x * cos
                    + pltpu.roll(x, LANES - ROPE_FREQS, 1) * sin_lo
                    + pltpu.roll(x, ROPE_FREQS, 1) * sin_hi)
    return outs[0] if len(outs) == 1 else jnp.concatenate(outs, axis=1)


def _with_swapped_halves(a):
    return jnp.concatenate([a, pltpu.roll(a, HEAD_DIM, 1)], axis=1)


def _norm_mod(x, gain, shift):
    ms = jnp.mean(x * x, axis=-1, keepdims=True)
    return (x * lax.rsqrt(ms + EPS) * gain + shift).astype(bf16)


def _kv_proj_kernel(x_ref, nw_ref, shift_ref, scale_ref, w_ref, k_ref, v_ref):
    gain, shift = nw_ref[...] * (1.0 + scale_ref[...]), shift_ref[...]
    kv = jnp.dot(_norm_mod(x_ref[...], gain, shift), w_ref[...], preferred_element_type=f32)
    k_ref[...] = _with_swapped_halves(kv[:, :KV_WIDTH]).astype(bf16)
    v_ref[...] = _with_swapped_halves(kv[:, KV_WIDTH:]).astype(bf16)


def _proj_kernel(*refs, rope, seq, subs):
    it = iter(refs)
    x_ref, xp_ref, xn_ref, nw_ref, shift_ref, scale_ref, w_ref, cw_ref = (
        next(it) for _ in range(8))
    if rope:
        cos_ref, slo_ref, shi_ref = (next(it) for _ in range(3))
    ya_ref, q_ref, k_ref, v_ref, zb_ref, ga_ref, gb_ref = (next(it) for _ in range(7))
    h_s, cu_s = next(it), next(it)
    tm = x_ref.shape[0]
    assert sum(subs) == tm and (seq % tm == 0 or tm % seq == 0)
    halo = 2 * SUBLANES
    gain, shift = nw_ref[...] * (1.0 + scale_ref[...]), shift_ref[...]
    tiles_per_seq = max(seq // tm, 1)
    t_in_seq = pl.program_id(0) % tiles_per_seq
    cw = cw_ref[...]
    zero = jnp.zeros((SUBLANES, CONV_WIDTH), f32)

    def buffers(n):
        s0 = sum(subs[:n]) + n * halo
        return h_s.at[s0:s0 + subs[n] + halo], cu_s.at[s0:s0 + subs[n] + halo]

    for n, sub in enumerate(subs):
        r0 = sum(subs[:n])
        rows = slice(r0, r0 + sub)
        h, cu_b = buffers(n)
        before = xp_ref[...] if r0 == 0 else x_ref[r0 - SUBLANES:r0, :]
        after = xn_ref[...] if r0 + sub == tm else x_ref[r0 + sub:r0 + sub + SUBLANES, :]
        h[0:sub, :] = _norm_mod(x_ref[rows, :], gain, shift)
        h[sub:sub + halo, :] = _norm_mod(jnp.concatenate([before, after], axis=0), gain, shift)

        def proj(off, width, n_rows=sub):
            return jnp.dot(h[0:n_rows, :], w_ref[:, off:off + width],
                           preferred_element_type=f32)

        cu = proj(_OFF_C, CONV_WIDTH, sub + halo) * proj(_OFF_U, CONV_WIDTH, sub + halo)
        cu_before, cu_after = cu[sub:sub + SUBLANES], cu[sub + SUBLANES:]
        if r0 % seq == 0:
            cu_before = jnp.where(t_in_seq > 0, cu_before, zero) if r0 == 0 else zero
        if (r0 + sub) % seq == 0 or r0 + sub == tm:
            cu_after = (jnp.where(t_in_seq < tiles_per_seq - 1, cu_after, zero)
                        if r0 + sub == tm else zero)
        cu_b[0:SUBLANES, :] = cu_before
        cu_b[SUBLANES:SUBLANES + sub, :] = cu[0:sub]
        cu_b[SUBLANES + sub:, :] = cu_after
        conv = (cu_b[SUBLANES - 1:SUBLANES - 1 + sub, :] * cw[0:1]
                + cu_b[SUBLANES:SUBLANES + sub, :] * cw[1:2]
                + cu_b[SUBLANES + 1:SUBLANES + 1 + sub, :] * cw[2:3])
        ya_ref[rows, :] = (proj(_OFF_B, CONV_WIDTH) * conv
                           * _silu(proj(_OFF_ZA, CONV_WIDTH))).astype(bf16)

        q = proj(_OFF_Q, ATTN_WIDTH)
        kv = proj(_OFF_K, 2 * KV_WIDTH)
        k = kv[:, :KV_WIDTH]
        if rope:
            cos, slo, shi = cos_ref[rows, :], slo_ref[rows, :], shi_ref[rows, :]
            q = _rope(q, cos, slo, shi)
            k = _rope(k, cos, slo, shi)
        q_ref[rows, :] = (q * (HEAD_DIM ** -0.5 * LOG2E)).astype(bf16)
        k_ref[rows, :] = _with_swapped_halves(k).astype(bf16)
        v_ref[rows, :] = _with_swapped_halves(kv[:, KV_WIDTH:]).astype(bf16)
        zb_ref[rows, :] = _silu(proj(_OFF_ZB, ATTN_WIDTH)).astype(bf16)
        part = 2 * LANES
        for c in range(D_MODEL // part):
            ga_ref[rows, c * part:(c + 1) * part] = _sigmoid(
                proj(_OFF_GA + c * part, part)).astype(bf16)
        for c in range(D_MODEL // part):
            gb_ref[rows, c * part:(c + 1) * part] = _sigmoid(
                proj(_OFF_GB + c * part, part)).astype(bf16)


def _mod_spec(layer, row_of, which):
    return pl.BlockSpec((None, None, None, 1, D_MODEL),
                        lambda *g: (layer, row_of(*g), which, 0, 0))


def _layer_spec(layer, shape, single_buffer=False):
    return pl.BlockSpec((None,) + tuple(shape), lambda *g: (layer,) + (0,) * len(shape),
                        pipeline_mode=pl.Buffered(1) if single_buffer else None)


def _projection(x2, layer, norm_w, mod, w_in, conv_w, rope_tabs, *, subs, seq, mod_row0,
                mod_per_seq):
    tm = sum(subs)
    scratch_rows = tm + len(subs) * 2 * SUBLANES
    n_rows = x2.shape[0]
    assert seq % tm == 0 or (tm % seq == 0 and not mod_per_seq and rope_tabs is None)
    tiles_per_seq = max(seq // tm, 1)
    halo_per_tile = tm // SUBLANES
    n_halo = n_rows // SUBLANES
    if mod_per_seq:
        mod_row = lambda t: mod_row0 + t // tiles_per_seq
    else:
        mod_row = lambda t: mod_row0
    row = lambda w: pl.BlockSpec((tm, w), lambda t: (t, 0))
    in_specs = [
        row(D_MODEL),
        pl.BlockSpec((SUBLANES, D_MODEL),
                     lambda t: (jnp.maximum(t * halo_per_tile - 1, 0), 0)),
        pl.BlockSpec((SUBLANES, D_MODEL),
                     lambda t: (jnp.minimum((t + 1) * halo_per_tile, n_halo - 1), 0)),
        _layer_spec(layer, (1, D_MODEL)),
        _mod_spec(layer, mod_row, 0),
        _mod_spec(layer, mod_row, 1),
        _layer_spec(layer, (D_MODEL, IN_COLS)),
        _layer_spec(layer, (CONV_K, CONV_WIDTH)),
    ]
    args = [x2, x2, x2, norm_w.reshape(-1, 1, D_MODEL), mod, mod, w_in, conv_w]
    if rope_tabs is not None:
        in_specs += [pl.BlockSpec((tm, LANES), lambda t: (t % tiles_per_seq, 0))] * 3
        args += list(rope_tabs)
    widths = (CONV_WIDTH, ATTN_WIDTH, 2 * KV_WIDTH, 2 * KV_WIDTH, ATTN_WIDTH, D_MODEL, D_MODEL)
    return pl.pallas_call(
        functools.partial(_proj_kernel, rope=rope_tabs is not None,
                          seq=seq, subs=tuple(subs)),
        grid=(n_rows // tm,),
        in_specs=in_specs,
        out_specs=[row(w) for w in widths],
        out_shape=[jax.ShapeDtypeStruct((n_rows, w), bf16) for w in widths],
        scratch_shapes=[pltpu.VMEM((scratch_rows, D_MODEL), bf16),
                        pltpu.VMEM((scratch_rows, CONV_WIDTH), f32)],
        compiler_params=pltpu.CompilerParams(
            dimension_semantics=("arbitrary",), vmem_limit_bytes=VMEM_LIMIT),
        name="projection_rope" if rope_tabs is not None else "projection",
    )(*args)


def _kv_projection(x2, layer, norm_w, mod, w_in, *, tm, mod_row):
    n_rows = x2.shape[0]
    kvw = 2 * KV_WIDTH
    assert _OFF_K % kvw == 0 and _OFF_V == _OFF_K + KV_WIDTH
    row = lambda w: pl.BlockSpec((tm, w), lambda t: (t, 0))
    return pl.pallas_call(
        _kv_proj_kernel,
        grid=(n_rows // tm,),
        in_specs=[row(D_MODEL),
                  _layer_spec(layer, (1, D_MODEL)),
                  _mod_spec(layer, lambda t: mod_row, 0),
                  _mod_spec(layer, lambda t: mod_row, 1),
                  pl.BlockSpec((None, D_MODEL, kvw), lambda t: (layer, 0, _OFF_K // kvw))],
        out_specs=[row(kvw), row(kvw)],
        out_shape=[jax.ShapeDtypeStruct((n_rows, kvw), bf16)] * 2,
        compiler_params=pltpu.CompilerParams(
            dimension_semantics=("arbitrary",), vmem_limit_bytes=VMEM_LIMIT),
        name="projection_kv",
    )(x2, norm_w.reshape(-1, 1, D_MODEL), mod, mod, w_in)


def _kv_variants(a2):
    a, sw = a2[:, :LANES], a2[:, LANES:]
    low = lax.broadcasted_iota(jnp.int32, a.shape, 1) < HEAD_DIM
    z = jnp.zeros_like(a)
    return [jnp.where(low, a, z), jnp.where(low, z, sw),
            jnp.where(low, sw, z), jnp.where(low, z, a)]


def _mix_kernel(*refs, layer, tq, ctx_len, has_local, final):
    it = iter(refs)
    sink_ref = next(it)
    x_ref, gate_ref, ya_ref, q_ref, zb_ref, ga_ref, gb_ref = (next(it) for _ in range(7))
    if has_local:
        k_ref, kp_ref, kn_ref, v_ref, vp_ref, vn_ref = (next(it) for _ in range(6))
    kc_ref, vc_ref = next(it), next(it)
    wa_ref, wb_ref, wo_ref = (next(it) for _ in range(3))
    if final:
        fw_ref = next(it)
    o_ref = next(it)
    y_s, yb_s, k_s, vm_s, s_s, p_s, z_s = (next(it) for _ in range(7))
    if has_local:
        bias_s = next(it)

    i = pl.program_id(1)
    n_i = pl.num_programs(1)
    nblk = tq // BLOCK
    n_ct = kc_ref.shape[0] // ctx_len
    assert has_local <= (n_ct == 1)

    def seq_of(j):
        return j // (nblk // n_ct)
    n_loc = 3 * BLOCK if has_local else 0
    nk = ctx_len + n_loc
    n_slot = vm_s.shape[0]
    ring = p_s.shape[0] // N_KV_HEADS
    rows_a = SLABS_PER_KV * BLOCK

    @pl.when(i == 0)
    def _():
        kvar = _kv_variants(kc_ref[...])
        vvar = _kv_variants(vc_ref[...])
        low = lax.broadcasted_iota(jnp.int32, (nk, LANES), 1) < HEAD_DIM
        ones_on = [jnp.where(low, 1.0, 0.0).astype(bf16), jnp.where(low, 0.0, 1.0).astype(bf16)]
        for n in range(4):
            k_s[n, 0:n_ct * ctx_len, :] = kvar[n]
        for slot in range(n_slot):
            c0 = 0 if has_local else slot * ctx_len
            for kh in range(N_KV_HEADS):
                for e in range(HEADS_PER_SLAB):
                    vm_s[slot, kh, e * nk:e * nk + ctx_len, 0:LANES] = (
                        vvar[2 * kh + e][c0:c0 + ctx_len])
                    vm_s[slot, kh, e * nk:(e + 1) * nk, LANES:] = ones_on[e]

    def local_block(ref_prev, ref_tile, ref_next, t):
        if t == 0:
            return ref_prev[...]
        if t == nblk + 1:
            return ref_next[...]
        return ref_tile[(t - 1) * BLOCK:t * BLOCK, :]

    slot_free = {}

    def fill_values(j, b):
        blk = local_block(vp_ref, v_ref, vn_ref, j + b)
        if j >= n_slot:
            after = sum(slot_free[(j - n_slot, kh)] for kh in range(N_KV_HEADS))
            blk = (blk.astype(f32) + after).astype(bf16)
        vvar = _kv_variants(blk)
        for kh in range(N_KV_HEADS):
            for e in range(HEADS_PER_SLAB):
                r = e * nk + ctx_len + b * BLOCK
                vm_s[j % n_slot, kh, r:r + BLOCK, 0:LANES] = vvar[2 * kh + e]

    def fill_keys(t):
        for n, a in enumerate(_kv_variants(local_block(kp_ref, k_ref, kn_ref, t))):
            k_s[n, ctx_len + t * BLOCK:ctx_len + (t + 1) * BLOCK, :] = a

    def fill_bias():
        qi = lax.broadcasted_iota(jnp.int32, (rows_a, BLOCK), 0) % BLOCK
        ci = lax.broadcasted_iota(jnp.int32, (rows_a, BLOCK), 1)
        tri_prev = jnp.where(ci >= qi, 0.0, NEG)
        tri_next = jnp.where(ci <= qi, 0.0, NEG)
        bias_s[0] = tri_prev + jnp.where(i == 0, NEG, 0.0)
        bias_s[1] = tri_prev
        bias_s[2] = tri_next
        bias_s[3] = tri_next + jnp.where(i == n_i - 1, NEG, 0.0)

    if has_local:
        for t in range(nblk + 2):
            fill_keys(t)
        fill_bias()

    nt = (((1,), (1,)), ((), ()))

    def stage_a(j, kh, e):
        u = ((j % ring) * N_KV_HEADS + kh) * HEADS_PER_SLAB + e
        rows = slice(j * BLOCK, (j + 1) * BLOCK)
        q2 = jnp.concatenate(
            [q_ref[rows, (kh * SLABS_PER_KV + s) * LANES:(kh * SLABS_PER_KV + s + 1) * LANES]
             for s in range(SLABS_PER_KV)], axis=0)
        var = 2 * kh + e
        kc0 = seq_of(j) * ctx_len
        s_s[u, :, 0:ctx_len] = lax.dot_general(
            q2, k_s[var, kc0:kc0 + ctx_len, :], nt, preferred_element_type=f32)
        if has_local:
            r0 = ctx_len + j * BLOCK
            s_loc = lax.dot_general(q2, k_s[var, r0:r0 + n_loc, :], nt,
                                    preferred_element_type=f32)
            c0 = ctx_len
            s_s[u, :, c0:c0 + BLOCK] = s_loc[:, :BLOCK] + bias_s[0 if j == 0 else 1]
            s_s[u, :, c0 + BLOCK:c0 + 2 * BLOCK] = s_loc[:, BLOCK:2 * BLOCK]
            s_s[u, :, c0 + 2 * BLOCK:] = s_loc[:, 2 * BLOCK:] + bias_s[3 if j == nblk - 1 else 2]

    low_half = lax.broadcasted_iota(jnp.int32, (BLOCK, LANES), 1) < HEAD_DIM

    def stage_b(j, kh):
        jk = (j % ring) * N_KV_HEADS + kh
        for s in range(SLABS_PER_KV):
            rs = slice(s * BLOCK, (s + 1) * BLOCK)
            z = []
            for e in range(HEADS_PER_SLAB):
                u = jk * HEADS_PER_SLAB + e
                sk = sink_ref[layer, (kh * SLABS_PER_KV + s) * HEADS_PER_SLAB + e] * LOG2E
                mx = s_s[u, rs, 0:LANES]
                for n in range(1, nk // LANES):
                    mx = jnp.maximum(mx, s_s[u, rs, n * LANES:(n + 1) * LANES])
                m = jnp.maximum(jnp.max(mx, axis=1, keepdims=True), sk)
                for n in range(nk // LANES):
                    col = e * nk + n * LANES
                    p_s[jk, rs, col:col + LANES] = jnp.exp2(
                        s_s[u, rs, n * LANES:(n + 1) * LANES] - m).astype(bf16)
                z.append(jnp.exp2(sk - m))
            z_s[jk, rs, :] = jnp.where(low_half, z[0], z[1])

    def stage_c(j, kh):
        jk = (j % ring) * N_KV_HEADS + kh
        res = jnp.dot(p_s[jk], vm_s[j % n_slot if has_local else seq_of(j), kh],
                      preferred_element_type=f32)
        slot_free[(j, kh)] = jnp.clip(res[0:1, 0:1], -1.0, 1.0) * 0.0
        attn = res[:, :LANES] / (res[:, LANES:] + z_s[jk])
        rows = slice(j * BLOCK, (j + 1) * BLOCK)
        for s in range(SLABS_PER_KV):
            c = kh * SLABS_PER_KV + s
            lanes = slice(c * LANES, (c + 1) * LANES)
            yb_s[rows, lanes] = (attn[s * BLOCK:(s + 1) * BLOCK]
                                 * zb_ref[rows, lanes].astype(f32)).astype(bf16)

    n_part = 4
    width = D_MODEL // n_part

    for t in range(nblk + 2):
        for kh in range(N_KV_HEADS):
            if has_local and 0 <= t - 1 < nblk:
                for b in range(3)[kh::N_KV_HEADS]:
                    fill_values(t - 1, b)
            if 0 <= t - 1 < nblk:
                stage_b(t - 1, kh)
            if 0 <= t - 2 < nblk:
                stage_c(t - 2, kh)
            if t < nblk:
                for e in range(HEADS_PER_SLAB):
                    stage_a(t, kh, e)

    for n in range(n_part):
        cols = slice(n * width, (n + 1) * width)
        y_s[:, cols] = (
            jnp.dot(ya_ref[...], wa_ref[:, cols], preferred_element_type=f32)
            * ga_ref[:, cols].astype(f32)
            + jnp.dot(yb_s[...], wb_ref[:, cols], preferred_element_type=f32)
            * gb_ref[:, cols].astype(f32)).astype(bf16)
    y16 = y_s[...]
    for n in range(n_part):
        cols = slice(n * width, (n + 1) * width)
        o_ref[:, cols] = x_ref[:, cols] + gate_ref[:, cols] * jnp.dot(
            y16, wo_ref[:, cols], preferred_element_type=f32)
    if final:
        xn = o_ref[...]
        ms = jnp.mean(xn * xn, axis=-1, keepdims=True)
        o_ref[...] = xn * lax.rsqrt(ms + EPS) * fw_ref[...]


def _mixer(x2, layer, mod, sink, proj_out, kc, vc, wa, wb, wo, final_w, *,
           seq, tq, ctx_len, has_local, gate_row0, gate_per_batch):
    ya, q, k, v, zb, ga, gb = proj_out
    n_rows = x2.shape[0]
    seqs_per_tile = max(tq // seq, 1)
    assert kc.shape[0] * seq == n_rows * ctx_len and not (has_local and seqs_per_tile > 1)
    n_tile_b = n_rows // (seq * seqs_per_tile)
    n_i = max(seq // tq, 1)
    nblk = tq // BLOCK
    final = final_w is not None

    def row(w):
        return pl.BlockSpec((tq, w), lambda b, i: (b * n_i + i, 0))

    def halo(rows_blk, w, side):
        per_tile = tq // rows_blk
        per_seq = seq // rows_blk
        if side < 0:
            return pl.BlockSpec((rows_blk, w), lambda b, i: (
                b * per_seq + jnp.maximum(i * per_tile - 1, 0), 0))
        return pl.BlockSpec((rows_blk, w), lambda b, i: (
            b * per_seq + jnp.minimum((i + 1) * per_tile, per_seq - 1), 0))

    def whole(shape):
        return pl.BlockSpec(shape, lambda b, i: (0,) * len(shape))

    if gate_per_batch:
        gate_spec = _mod_spec(layer, lambda b, i: gate_row0 + b, 2)
    else:
        gate_spec = _mod_spec(layer, lambda b, i: gate_row0, 2)

    kvw = 2 * KV_WIDTH
    in_specs = [pl.BlockSpec(memory_space=pltpu.SMEM),
                row(D_MODEL), gate_spec, row(CONV_WIDTH),
                row(ATTN_WIDTH), row(ATTN_WIDTH), row(D_MODEL), row(D_MODEL)]
    args = [sink, x2, mod, ya, q, zb, ga, gb]
    if has_local:
        in_specs += [row(kvw), halo(BLOCK, kvw, -1), halo(BLOCK, kvw, +1)] * 2
        args += [k, k, k, v, v, v]
    ctx_spec = pl.BlockSpec((seqs_per_tile * ctx_len, kvw), lambda b, i: (b, 0))
    in_specs += [ctx_spec, ctx_spec, _layer_spec(layer, (CONV_WIDTH, D_MODEL), True),
                 _layer_spec(layer, (ATTN_WIDTH, D_MODEL), True),
                 _layer_spec(layer, (D_MODEL, D_MODEL), True)]
    args += [kc, vc, wa, wb, wo]
    if final:
        in_specs.append(whole((1, D_MODEL)))
        args.append(final_w.reshape(1, D_MODEL))

    nk = ctx_len + (3 * BLOCK if has_local else 0)
    k_rows = ctx_len + tq + 2 * BLOCK if has_local else seqs_per_tile * ctx_len
    ring = min(nblk, 3)
    n_slot = ring if has_local else seqs_per_tile
    rows_a = SLABS_PER_KV * BLOCK
    scratch = [pltpu.VMEM((tq, D_MODEL), bf16),
               pltpu.VMEM((tq, ATTN_WIDTH), bf16),
               pltpu.VMEM((2 * N_KV_HEADS, k_rows, LANES), bf16),
               pltpu.VMEM((n_slot, N_KV_HEADS, HEADS_PER_SLAB * nk, 2 * LANES), bf16),
               pltpu.VMEM((ring * N_KV_HEADS * HEADS_PER_SLAB, rows_a, nk), f32),
               pltpu.VMEM((ring * N_KV_HEADS, rows_a, HEADS_PER_SLAB * nk), bf16),
               pltpu.VMEM((ring * N_KV_HEADS, rows_a, LANES), f32)]
    if has_local:
        scratch += [pltpu.VMEM((4, rows_a, BLOCK), f32)]

    return pl.pallas_call(
        functools.partial(_mix_kernel, layer=layer, tq=tq, ctx_len=ctx_len,
                          has_local=has_local, final=final),
        grid=(n_tile_b, n_i),
        in_specs=in_specs,
        out_specs=row(D_MODEL),
        out_shape=jax.ShapeDtypeStruct((n_rows, D_MODEL), f32),
        scratch_shapes=scratch,
        compiler_params=pltpu.CompilerParams(
            dimension_semantics=("arbitrary", "arbitrary"), vmem_limit_bytes=VMEM_LIMIT),
        name="mixer_latent" if has_local else "mixer_context",
    )(*args)


def _rope_tables(seq):
    n_rows = seq // GRID_W
    lane = np.arange(LANES)
    lane_freq = (jnp.asarray(ROPE_THETA, f32)
                 ** (-jnp.asarray(lane % ROPE_FREQS, f32) / ROPE_FREQS))[None, :]
    use_col = ((lane % HEAD_DIM) >= HEAD_DIM // 2)[None, None, :]
    upper = ((lane % (2 * ROPE_FREQS)) >= ROPE_FREQS)[None, None, :]
    ang_r = jnp.arange(n_rows, dtype=f32)[:, None] * lane_freq
    ang_c = jnp.arange(GRID_W, dtype=f32)[:, None] * lane_freq
    shape = (n_rows, GRID_W, LANES)
    cos = jnp.where(use_col, jnp.cos(ang_c)[None], jnp.cos(ang_r)[:, None]).reshape(seq, LANES)
    sin = jnp.broadcast_to(jnp.where(use_col, jnp.sin(ang_c)[None], jnp.sin(ang_r)[:, None]),
                           shape)
    zero = jnp.zeros(shape, f32)
    sin_lo = jnp.where(upper, zero, -sin).reshape(seq, LANES)
    sin_hi = jnp.where(upper, sin, zero).reshape(seq, LANES)
    return cos, sin_lo, sin_hi


def kernel(x, c, ctx, c_ctx, norm_w, w_mod, b_mod, w_in, conv_w, w_a_out, w_b_out,
           attn_sink, w_o, final_norm_w):
    batch, seq, _ = x.shape
    ctx_len = ctx.shape[1]
    depth = w_in.shape[0]
    assert batch + 1 <= MOD_ROWS and seq % 512 == 0 and ctx_len % BLOCK == 0

    c_all = jnp.zeros((MOD_ROWS, D_MODEL), f32).at[:batch].set(c).at[batch].set(c_ctx)
    mod = _modulation(c_all, w_mod, b_mod)
    rope_tabs = _rope_tables(seq)

    w_in_b = w_in.astype(bf16)
    wa_b, wb_b, wo_b = w_a_out.astype(bf16), w_b_out.astype(bf16), w_o.astype(bf16)

    x2 = x.reshape(batch * seq, D_MODEL)
    c2 = ctx.reshape(batch * ctx_len, D_MODEL)
    for l in range(depth):
        last = l == depth - 1
        if last:
            kc, vc = _kv_projection(c2, l, norm_w, mod, w_in_b, tm=4 * ctx_len, mod_row=batch)
        else:
            pc = _projection(c2, l, norm_w, mod, w_in_b, conv_w, None, subs=(ctx_len,) * 4,
                             seq=ctx_len, mod_row0=batch, mod_per_seq=False)
            kc, vc = pc[2], pc[3]
        px = _projection(x2, l, norm_w, mod, w_in_b, conv_w, rope_tabs,
                         subs=(512, 512), seq=seq, mod_row0=0, mod_per_seq=True)
        x2 = _mixer(x2, l, mod, attn_sink, px, kc, vc, wa_b, wb_b, wo_b,
                    final_norm_w if last else None,
                    seq=seq, tq=1024, ctx_len=ctx_len, has_local=True, gate_row0=0,
                    gate_per_batch=True)
        if not last:
            c2 = _mixer(c2, l, mod, attn_sink, pc, kc, vc, wa_b, wb_b, wo_b, None,
                        seq=ctx_len, tq=ctx_len, ctx_len=ctx_len, has_local=False,
                        gate_row0=batch, gate_per_batch=False)
    return x2.reshape(batch, seq, D_MODEL)
```

```python
import functools
import math

import numpy as np
import jax
import jax.numpy as jnp
from jax import lax
from jax.experimental import pallas as pl
from jax.experimental.pallas import tpu as pltpu

D_MODEL = 1024
GRID_W = 64
CONV_WIDTH = 512
CONV_K = 3
N_HEADS = 8
N_KV_HEADS = 2
HEAD_DIM = 64
ATTN_WIDTH = N_HEADS * HEAD_DIM
KV_WIDTH = N_KV_HEADS * HEAD_DIM
BLOCK = 128
ROPE_THETA = 10000.0
ROPE_FREQS = HEAD_DIM // 4
EPS = 1e-6
NEG = -1e30
LOG2E = math.log2(math.e)

LANES = 128
SUBLANES = 8
MOD_ROWS = 24
VMEM_LIMIT = 56 * 1024 * 1024
N_SLAB = ATTN_WIDTH // LANES
HEADS_PER_SLAB = LANES // HEAD_DIM
SLABS_PER_KV = N_SLAB // N_KV_HEADS

_OFF_B = 0
_OFF_C = _OFF_B + CONV_WIDTH
_OFF_U = _OFF_C + CONV_WIDTH
_OFF_ZA = _OFF_U + CONV_WIDTH
_OFF_Q = _OFF_ZA + CONV_WIDTH
_OFF_K = _OFF_Q + ATTN_WIDTH
_OFF_V = _OFF_K + KV_WIDTH
_OFF_ZB = _OFF_V + KV_WIDTH
_OFF_GA = _OFF_ZB + ATTN_WIDTH
_OFF_GB = _OFF_GA + D_MODEL
IN_COLS = _OFF_GB + D_MODEL

f32 = jnp.float32
bf16 = jnp.bfloat16


def _sigmoid(x):
    return 0.5 * jnp.tanh(0.5 * x) + 0.5


def _silu(x):
    return x * _sigmoid(x)


def _mod_kernel(c_ref, w_ref, b_ref, o_ref):
    s = _silu(c_ref[...]).astype(bf16)
    res = jnp.dot(s, w_ref[...].astype(bf16), preferred_element_type=f32) + b_ref[...]
    for r in range(MOD_ROWS):
        o_ref[r] = res[r:r + 1, :]


def _modulation(c_all, w_mod, b_mod):
    depth = w_mod.shape[0]
    return pl.pallas_call(
        _mod_kernel,
        grid=(depth, 3),
        in_specs=[
            pl.BlockSpec((MOD_ROWS, D_MODEL), lambda l, j: (0, 0)),
            pl.BlockSpec((None, D_MODEL, D_MODEL), lambda l, j: (l, 0, j)),
            pl.BlockSpec((None, 1, D_MODEL), lambda l, j: (l, 0, j)),
        ],
        out_specs=pl.BlockSpec((None, MOD_ROWS, None, 1, D_MODEL), lambda l, j: (l, 0, j, 0, 0)),
        out_shape=jax.ShapeDtypeStruct((depth, MOD_ROWS, 3, 1, D_MODEL), f32),
        name="modulation",
    )(c_all, w_mod, b_mod.reshape(depth, 1, 3 * D_MODEL))


def _rope(p, cos, sin_lo, sin_hi):
    outs = []
    for j in range(p.shape[1] // LANES):
        x = p[:, j * LANES:(j + 1) * LANES]
        outs.append(x * cos
                    + pltpu.roll(x, LANES - ROPE_FREQS, 1) * sin_lo
                    + pltpu.roll(x, ROPE_FREQS, 1) * sin_hi)
    return outs[0] if len(outs) == 1 else jnp.concatenate(outs, axis=1)


def _with_swapped_halves(a):
    return jnp.concatenate([a, pltpu.roll(a, HEAD_DIM, 1)], axis=1)


def _norm_mod(x, gain, shift):
    ms = jnp.mean(x * x, axis=-1, keepdims=True)
    return (x * lax.rsqrt(ms + EPS) * gain + shift).astype(bf16)


def _kv_proj_kernel(x_ref, nw_ref, shift_ref, scale_ref, w_ref, k_ref, v_ref):
    gain, shift = nw_ref[...] * (1.0 + scale_ref[...]), shift_ref[...]
    kv = jnp.dot(_norm_mod(x_ref[...], gain, shift), w_ref[...], preferred_element_type=f32)
    k_ref[...] = _with_swapped_halves(kv[:, :KV_WIDTH]).astype(bf16)
    v_ref[...] = _with_swapped_halves(kv[:, KV_WIDTH:]).astype(bf16)


def _proj_kernel(*refs, rope, seq, subs):
    it = iter(refs)
    x_ref, xp_ref, xn_ref, nw_ref, shift_ref, scale_ref, w_ref, cw_ref = (
        next(it) for _ in range(8))
    if rope:
        cos_ref, slo_ref, shi_ref = (next(it) for _ in range(3))
    ya_ref, q_ref, k_ref, v_ref, zb_ref, ga_ref, gb_ref = (next(it) for _ in range(7))
    h_s, cu_s = next(it), next(it)
    tm = x_ref.shape[0]
    assert sum(subs) == tm and (seq % tm == 0 or tm % seq == 0)
    halo = 2 * SUBLANES
    gain, shift = nw_ref[...] * (1.0 + scale_ref[...]), shift_ref[...]
    tiles_per_seq = max(seq // tm, 1)
    t_in_seq = pl.program_id(0) % tiles_per_seq
    cw = cw_ref[...]
    zero = jnp.zeros((SUBLANES, CONV_WIDTH), f32)

    def buffers(n):
        s0 = sum(subs[:n]) + n * halo
        return h_s.at[s0:s0 + subs[n] + halo], cu_s.at[s0:s0 + subs[n] + halo]

    for n, sub in enumerate(subs):
        r0 = sum(subs[:n])
        rows = slice(r0, r0 + sub)
        h, cu_b = buffers(n)
        before = xp_ref[...] if r0 == 0 else x_ref[r0 - SUBLANES:r0, :]
        after = xn_ref[...] if r0 + sub == tm else x_ref[r0 + sub:r0 + sub + SUBLANES, :]
        h[0:sub, :] = _norm_mod(x_ref[rows, :], gain, shift)
        h[sub:sub + halo, :] = _norm_mod(jnp.concatenate([before, after], axis=0), gain, shift)

        def proj(off, width, n_rows=sub):
            return jnp.dot(h[0:n_rows, :], w_ref[:, off:off + width],
                           preferred_element_type=f32)

        cu = proj(_OFF_C, CONV_WIDTH, sub + halo) * proj(_OFF_U, CONV_WIDTH, sub + halo)
        cu_before, cu_after = cu[sub:sub + SUBLANES], cu[sub + SUBLANES:]
        if r0 % seq == 0:
            cu_before = jnp.where(t_in_seq > 0, cu_before, zero) if r0 == 0 else zero
        if (r0 + sub) % seq == 0 or r0 + sub == tm:
            cu_after = (jnp.where(t_in_seq < tiles_per_seq - 1, cu_after, zero)
                        if r0 + sub == tm else zero)
        cu_b[0:SUBLANES, :] = cu_before
        cu_b[SUBLANES:SUBLANES + sub, :] = cu[0:sub]
        cu_b[SUBLANES + sub:, :] = cu_after
        conv = (cu_b[SUBLANES - 1:SUBLANES - 1 + sub, :] * cw[0:1]
                + cu_b[SUBLANES:SUBLANES + sub, :] * cw[1:2]
                + cu_b[SUBLANES + 1:SUBLANES + 1 + sub, :] * cw[2:3])
        ya_ref[rows, :] = (proj(_OFF_B, CONV_WIDTH) * conv
                           * _silu(proj(_OFF_ZA, CONV_WIDTH))).astype(bf16)

        q = proj(_OFF_Q, ATTN_WIDTH)
        kv = proj(_OFF_K, 2 * KV_WIDTH)
        k = kv[:, :KV_WIDTH]
        if rope:
            cos, slo, shi = cos_ref[rows, :], slo_ref[rows, :], shi_ref[rows, :]
            q = _rope(q, cos, slo, shi)
            k = _rope(k, cos, slo, shi)
        q_ref[rows, :] = (q * (HEAD_DIM ** -0.5 * LOG2E)).astype(bf16)
        k_ref[rows, :] = _with_swapped_halves(k).astype(bf16)
        v_ref[rows, :] = _with_swapped_halves(kv[:, KV_WIDTH:]).astype(bf16)
        zb_ref[rows, :] = _silu(proj(_OFF_ZB, ATTN_WIDTH)).astype(bf16)
        part = 2 * LANES
        for c in range(D_MODEL // part):
            ga_ref[rows, c * part:(c + 1) * part] = _sigmoid(
                proj(_OFF_GA + c * part, part)).astype(bf16)
        for c in range(D_MODEL // part):
            gb_ref[rows, c * part:(c + 1) * part] = _sigmoid(
                proj(_OFF_GB + c * part, part)).astype(bf16)


def _mod_spec(layer, row_of, which):
    return pl.BlockSpec((None, None, None, 1, D_MODEL),
                        lambda *g: (layer, row_of(*g), which, 0, 0))


def _layer_spec(layer, shape, single_buffer=False):
    return pl.BlockSpec((None,) + tuple(shape), lambda *g: (layer,) + (0,) * len(shape),
                        pipeline_mode=pl.Buffered(1) if single_buffer else None)


def _projection(x2, layer, norm_w, mod, w_in, conv_w, rope_tabs, *, subs, seq, mod_row0,
                mod_per_seq):
    tm = sum(subs)
    scratch_rows = tm + len(subs) * 2 * SUBLANES
    n_rows = x2.shape[0]
    assert seq % tm == 0 or (tm % seq == 0 and not mod_per_seq and rope_tabs is None)
    tiles_per_seq = max(seq // tm, 1)
    halo_per_tile = tm // SUBLANES
    n_halo = n_rows // SUBLANES
    if mod_per_seq:
        mod_row = lambda t: mod_row0 + t // tiles_per_seq
    else:
        mod_row = lambda t: mod_row0
    row = lambda w: pl.BlockSpec((tm, w), lambda t: (t, 0))
    in_specs = [
        row(D_MODEL),
        pl.BlockSpec((SUBLANES, D_MODEL),
                     lambda t: (jnp.maximum(t * halo_per_tile - 1, 0), 0)),
        pl.BlockSpec((SUBLANES, D_MODEL),
                     lambda t: (jnp.minimum((t + 1) * halo_per_tile, n_halo - 1), 0)),
        _layer_spec(layer, (1, D_MODEL)),
        _mod_spec(layer, mod_row, 0),
        _mod_spec(layer, mod_row, 1),
        _layer_spec(layer, (D_MODEL, IN_COLS)),
        _layer_spec(layer, (CONV_K, CONV_WIDTH)),
    ]
    args = [x2, x2, x2, norm_w.reshape(-1, 1, D_MODEL), mod, mod, w_in, conv_w]
    if rope_tabs is not None:
        in_specs += [pl.BlockSpec((tm, LANES), lambda t: (t % tiles_per_seq, 0))] * 3
        args += list(rope_tabs)
    widths = (CONV_WIDTH, ATTN_WIDTH, 2 * KV_WIDTH, 2 * KV_WIDTH, ATTN_WIDTH, D_MODEL, D_MODEL)
    return pl.pallas_call(
        functools.partial(_proj_kernel, rope=rope_tabs is not None,
                          seq=seq, subs=tuple(subs)),
        grid=(n_rows // tm,),
        in_specs=in_specs,
        out_specs=[row(w) for w in widths],
        out_shape=[jax.ShapeDtypeStruct((n_rows, w), bf16) for w in widths],
        scratch_shapes=[pltpu.VMEM((scratch_rows, D_MODEL), bf16),
                        pltpu.VMEM((scratch_rows, CONV_WIDTH), f32)],
        compiler_params=pltpu.CompilerParams(
            dimension_semantics=("arbitrary",), vmem_limit_bytes=VMEM_LIMIT),
        name="projection_rope" if rope_tabs is not None else "projection",
    )(*args)


def _kv_projection(x2, layer, norm_w, mod, w_in, *, tm, mod_row):
    n_rows = x2.shape[0]
    kvw = 2 * KV_WIDTH
    assert _OFF_K % kvw == 0 and _OFF_V == _OFF_K + KV_WIDTH
    row = lambda w: pl.BlockSpec((tm, w), lambda t: (t, 0))
    return pl.pallas_call(
        _kv_proj_kernel,
        grid=(n_rows // tm,),
        in_specs=[row(D_MODEL),
                  _layer_spec(layer, (1, D_MODEL)),
                  _mod_spec(layer, lambda t: mod_row, 0),
                  _mod_spec(layer, lambda t: mod_row, 1),
                  pl.BlockSpec((None, D_MODEL, kvw), lambda t: (layer, 0, _OFF_K // kvw))],
        out_specs=[row(kvw), row(kvw)],
        out_shape=[jax.ShapeDtypeStruct((n_rows, kvw), bf16)] * 2,
        compiler_params=pltpu.CompilerParams(
            dimension_semantics=("arbitrary",), vmem_limit_bytes=VMEM_LIMIT),
        name="projection_kv",
    )(x2, norm_w.reshape(-1, 1, D_MODEL), mod, mod, w_in)


def _kv_variants(a2):
    a, sw = a2[:, :LANES], a2[:, LANES:]
    low = lax.broadcasted_iota(jnp.int32, a.shape, 1) < HEAD_DIM
    z = jnp.zeros_like(a)
    return [jnp.where(low, a, z), jnp.where(low, z, sw),
            jnp.where(low, sw, z), jnp.where(low, z, a)]


def _mix_kernel(*refs, layer, tq, ctx_len, has_local, final):
    it = iter(refs)
    sink_ref = next(it)
    x_ref, gate_ref, ya_ref, q_ref, zb_ref, ga_ref, gb_ref = (next(it) for _ in range(7))
    if has_local:
        k_ref, kp_ref, kn_ref, v_ref, vp_ref, vn_ref = (next(it) for _ in range(6))
    kc_ref, vc_ref = next(it), next(it)
    wa_ref, wb_ref, wo_ref = (next(it) for _ in range(3))
    if final:
        fw_ref = next(it)
    o_ref = next(it)
    y_s, yb_s, k_s, vm_s, s_s, p_s, z_s = (next(it) for _ in range(7))
    if has_local:
        bias_s, vmc_s = next(it), next(it)

    i = pl.program_id(1)
    n_i = pl.num_programs(1)
    nblk = tq // BLOCK
    n_ct = kc_ref.shape[0] // ctx_len
    assert has_local <= (n_ct == 1)

    def seq_of(j):
        return j // (nblk // n_ct)
    n_loc = 3 * BLOCK if has_local else 0
    nk = ctx_len + n_loc
    n_slot = vm_s.shape[0]
    ring = p_s.shape[0] // N_KV_HEADS
    rows_a = SLABS_PER_KV * BLOCK

    @pl.when(i == 0)
    def _():
        kvar = _kv_variants(kc_ref[...])
        vvar = _kv_variants(vc_ref[...])
        def ones_on(e, n_rows):
            low = lax.broadcasted_iota(jnp.int32, (n_rows, LANES), 1) < HEAD_DIM
            return jnp.where(low == (e == 0), 1.0, 0.0).astype(bf16)

        for n in range(4):
            k_s[n, 0:n_ct * ctx_len, :] = kvar[n]
        for kh in range(N_KV_HEADS):
            for e in range(HEADS_PER_SLAB):
                if has_local:
                    rows = slice(e * ctx_len, (e + 1) * ctx_len)
                    vmc_s[kh, rows, 0:LANES] = vvar[2 * kh + e]
                    vmc_s[kh, rows, LANES:] = ones_on(e, ctx_len)
                    for slot in range(n_slot):
                        vm_s[slot, kh, e * n_loc:(e + 1) * n_loc, LANES:] = ones_on(e, n_loc)
                else:
                    for slot in range(n_slot):
                        rows = slice(e * ctx_len, (e + 1) * ctx_len)
                        vm_s[slot, kh, rows, 0:LANES] = (
                            vvar[2 * kh + e][slot * ctx_len:(slot + 1) * ctx_len])
                        vm_s[slot, kh, rows, LANES:] = ones_on(e, ctx_len)

    def local_block(ref_prev, ref_tile, ref_next, t):
        if t == 0:
            return ref_prev[...]
        if t == nblk + 1:
            return ref_next[...]
        return ref_tile[(t - 1) * BLOCK:t * BLOCK, :]

    def fill_values(j, b):
        vvar = _kv_variants(local_block(vp_ref, v_ref, vn_ref, j + b))
        for kh in range(N_KV_HEADS):
            for e in range(HEADS_PER_SLAB):
                r = e * n_loc + b * BLOCK
                vm_s[j, kh, r:r + BLOCK, 0:LANES] = vvar[2 * kh + e]

    def fill_keys(t):
        for n, a in enumerate(_kv_variants(local_block(kp_ref, k_ref, kn_ref, t))):
            k_s[n, ctx_len + t * BLOCK:ctx_len + (t + 1) * BLOCK, :] = a

    def fill_bias():
        qi = lax.broadcasted_iota(jnp.int32, (rows_a, BLOCK), 0) % BLOCK
        ci = lax.broadcasted_iota(jnp.int32, (rows_a, BLOCK), 1)
        tri_prev = jnp.where(ci >= qi, 0.0, NEG)
        tri_next = jnp.where(ci <= qi, 0.0, NEG)
        bias_s[0] = tri_prev + jnp.where(i == 0, NEG, 0.0)
        bias_s[1] = tri_prev
        bias_s[2] = tri_next
        bias_s[3] = tri_next + jnp.where(i == n_i - 1, NEG, 0.0)

    if has_local:
        for t in range(nblk + 2):
            fill_keys(t)
        fill_bias()

    nt = (((1,), (1,)), ((), ()))

    def stage_a(j, kh, e):
        u = ((j % ring) * N_KV_HEADS + kh) * HEADS_PER_SLAB + e
        rows = slice(j * BLOCK, (j + 1) * BLOCK)
        q2 = jnp.concatenate(
            [q_ref[rows, (kh * SLABS_PER_KV + s) * LANES:(kh * SLABS_PER_KV + s + 1) * LANES]
             for s in range(SLABS_PER_KV)], axis=0)
        var = 2 * kh + e
        kc0 = seq_of(j) * ctx_len
        s_s[u, :, 0:ctx_len] = lax.dot_general(
            q2, k_s[var, kc0:kc0 + ctx_len, :], nt, preferred_element_type=f32)
        if has_local:
            r0 = ctx_len + j * BLOCK
            s_loc = lax.dot_general(q2, k_s[var, r0:r0 + n_loc, :], nt,
                                    preferred_element_type=f32)
            c0 = ctx_len
            s_s[u, :, c0:c0 + BLOCK] = s_loc[:, :BLOCK] + bias_s[0 if j == 0 else 1]
            s_s[u, :, c0 + BLOCK:c0 + 2 * BLOCK] = s_loc[:, BLOCK:2 * BLOCK]
            s_s[u, :, c0 + 2 * BLOCK:] = s_loc[:, 2 * BLOCK:] + bias_s[3 if j == nblk - 1 else 2]

    low_half = lax.broadcasted_iota(jnp.int32, (BLOCK, LANES), 1) < HEAD_DIM

    def stage_b(j, kh):
        jk = (j % ring) * N_KV_HEADS + kh
        for s in range(SLABS_PER_KV):
            rs = slice(s * BLOCK, (s + 1) * BLOCK)
            z = []
            for e in range(HEADS_PER_SLAB):
                u = jk * HEADS_PER_SLAB + e
                sk = sink_ref[layer, (kh * SLABS_PER_KV + s) * HEADS_PER_SLAB + e] * LOG2E
                mx = s_s[u, rs, 0:LANES]
                for n in range(1, nk // LANES):
                    mx = jnp.maximum(mx, s_s[u, rs, n * LANES:(n + 1) * LANES])
                m = jnp.maximum(jnp.max(mx, axis=1, keepdims=True), sk)
                for n in range(nk // LANES):
                    ctx_chunk = n * LANES < ctx_len
                    col = (e * ctx_len + n * LANES if ctx_chunk else
                           HEADS_PER_SLAB * ctx_len + e * n_loc + n * LANES - ctx_len)
                    p_s[jk, rs, col:col + LANES] = jnp.exp2(
                        s_s[u, rs, n * LANES:(n + 1) * LANES] - m).astype(bf16)
                z.append(jnp.exp2(sk - m))
            z_s[jk, rs, :] = jnp.where(low_half, z[0], z[1])

    def stage_c(j, kh):
        jk = (j % ring) * N_KV_HEADS + kh
        n_c = HEADS_PER_SLAB * ctx_len
        if has_local:
            res = (jnp.dot(p_s[jk, :, 0:n_c], vmc_s[kh], preferred_element_type=f32)
                   + jnp.dot(p_s[jk, :, n_c:], vm_s[j, kh], preferred_element_type=f32))
        else:
            res = jnp.dot(p_s[jk], vm_s[seq_of(j), kh], preferred_element_type=f32)
        attn = res[:, :LANES] / (res[:, LANES:] + z_s[jk])
        rows = slice(j * BLOCK, (j + 1) * BLOCK)
        for s in range(SLABS_PER_KV):
            c = kh * SLABS_PER_KV + s
            lanes = slice(c * LANES, (c + 1) * LANES)
            yb_s[rows, lanes] = (attn[s * BLOCK:(s + 1) * BLOCK]
                                 * zb_ref[rows, lanes].astype(f32)).astype(bf16)

    n_part = 4
    width = D_MODEL // n_part

    for t in range(nblk + 2):
        for kh in range(N_KV_HEADS):
            if has_local and 0 <= t - 1 < nblk:
                for b in range(3)[kh::N_KV_HEADS]:
                    fill_values(t - 1, b)
            if 0 <= t - 1 < nblk:
                stage_b(t - 1, kh)
            if 0 <= t - 2 < nblk:
                stage_c(t - 2, kh)
            if t < nblk:
                for e in range(HEADS_PER_SLAB):
                    stage_a(t, kh, e)

    for n in range(n_part):
        cols = slice(n * width, (n + 1) * width)
        y_s[:, cols] = (
            jnp.dot(ya_ref[...], wa_ref[:, cols], preferred_element_type=f32)
            * ga_ref[:, cols].astype(f32)
            + jnp.dot(yb_s[...], wb_ref[:, cols], preferred_element_type=f32)
            * gb_ref[:, cols].astype(f32)).astype(bf16)
    y16 = y_s[...]
    for n in range(n_part):
        cols = slice(n * width, (n + 1) * width)
        o_ref[:, cols] = x_ref[:, cols] + gate_ref[:, cols] * jnp.dot(
            y16, wo_ref[:, cols], preferred_element_type=f32)
    if final:
        xn = o_ref[...]
        ms = jnp.mean(xn * xn, axis=-1, keepdims=True)
        o_ref[...] = xn * lax.rsqrt(ms + EPS) * fw_ref[...]


def _mixer(x2, layer, mod, sink, proj_out, kc, vc, wa, wb, wo, final_w, *,
           seq, tq, ctx_len, has_local, gate_row0, gate_per_batch):
    ya, q, k, v, zb, ga, gb = proj_out
    n_rows = x2.shape[0]
    seqs_per_tile = max(tq // seq, 1)
    assert kc.shape[0] * seq == n_rows * ctx_len and not (has_local and seqs_per_tile > 1)
    n_tile_b = n_rows // (seq * seqs_per_tile)
    n_i = max(seq // tq, 1)
    nblk = tq // BLOCK
    final = final_w is not None

    def row(w):
        return pl.BlockSpec((tq, w), lambda b, i: (b * n_i + i, 0))

    def halo(rows_blk, w, side):
        per_tile = tq // rows_blk
        per_seq = seq // rows_blk
        if side < 0:
            return pl.BlockSpec((rows_blk, w), lambda b, i: (
                b * per_seq + jnp.maximum(i * per_tile - 1, 0), 0))
        return pl.BlockSpec((rows_blk, w), lambda b, i: (
            b * per_seq + jnp.minimum((i + 1) * per_tile, per_seq - 1), 0))

    def whole(shape):
        return pl.BlockSpec(shape, lambda b, i: (0,) * len(shape))

    if gate_per_batch:
        gate_spec = _mod_spec(layer, lambda b, i: gate_row0 + b, 2)
    else:
        gate_spec = _mod_spec(layer, lambda b, i: gate_row0, 2)

    kvw = 2 * KV_WIDTH
    in_specs = [pl.BlockSpec(memory_space=pltpu.SMEM),
                row(D_MODEL), gate_spec, row(CONV_WIDTH),
                row(ATTN_WIDTH), row(ATTN_WIDTH), row(D_MODEL), row(D_MODEL)]
    args = [sink, x2, mod, ya, q, zb, ga, gb]
    if has_local:
        in_specs += [row(kvw), halo(BLOCK, kvw, -1), halo(BLOCK, kvw, +1)] * 2
        args += [k, k, k, v, v, v]
    ctx_spec = pl.BlockSpec((seqs_per_tile * ctx_len, kvw), lambda b, i: (b, 0))
    in_specs += [ctx_spec, ctx_spec, _layer_spec(layer, (CONV_WIDTH, D_MODEL), True),
                 _layer_spec(layer, (ATTN_WIDTH, D_MODEL), True),
                 _layer_spec(layer, (D_MODEL, D_MODEL), True)]
    args += [kc, vc, wa, wb, wo]
    if final:
        in_specs.append(whole((1, D_MODEL)))
        args.append(final_w.reshape(1, D_MODEL))

    nk = ctx_len + (3 * BLOCK if has_local else 0)
    k_rows = ctx_len + tq + 2 * BLOCK if has_local else seqs_per_tile * ctx_len
    ring = min(nblk, 3)
    n_slot = nblk if has_local else seqs_per_tile
    vm_rows = HEADS_PER_SLAB * (nk - ctx_len if has_local else ctx_len)
    rows_a = SLABS_PER_KV * BLOCK
    scratch = [pltpu.VMEM((tq, D_MODEL), bf16),
               pltpu.VMEM((tq, ATTN_WIDTH), bf16),
               pltpu.VMEM((2 * N_KV_HEADS, k_rows, LANES), bf16),
               pltpu.VMEM((n_slot, N_KV_HEADS, vm_rows, 2 * LANES), bf16),
               pltpu.VMEM((ring * N_KV_HEADS * HEADS_PER_SLAB, rows_a, nk), f32),
               pltpu.VMEM((ring * N_KV_HEADS, rows_a, HEADS_PER_SLAB * nk), bf16),
               pltpu.VMEM((ring * N_KV_HEADS, rows_a, LANES), f32)]
    if has_local:
        scratch += [pltpu.VMEM((4, rows_a, BLOCK), f32),
                    pltpu.VMEM((N_KV_HEADS, HEADS_PER_SLAB * ctx_len, 2 * LANES), bf16)]

    return pl.pallas_call(
        functools.partial(_mix_kernel, layer=layer, tq=tq, ctx_len=ctx_len,
                          has_local=has_local, final=final),
        grid=(n_tile_b, n_i),
        in_specs=in_specs,
        out_specs=row(D_MODEL),
        out_shape=jax.ShapeDtypeStruct((n_rows, D_MODEL), f32),
        scratch_shapes=scratch,
        compiler_params=pltpu.CompilerParams(
            dimension_semantics=("arbitrary", "arbitrary"), vmem_limit_bytes=VMEM_LIMIT),
        name="mixer_latent" if has_local else "mixer_context",
    )(*args)


def _rope_tables(seq):
    n_rows = seq // GRID_W
    lane = np.arange(LANES)
    lane_freq = (jnp.asarray(ROPE_THETA, f32)
                 ** (-jnp.asarray(lane % ROPE_FREQS, f32) / ROPE_FREQS))[None, :]
    use_col = ((lane % HEAD_DIM) >= HEAD_DIM // 2)[None, None, :]
    upper = ((lane % (2 * ROPE_FREQS)) >= ROPE_FREQS)[None, None, :]
    ang_r = jnp.arange(n_rows, dtype=f32)[:, None] * lane_freq
    ang_c = jnp.arange(GRID_W, dtype=f32)[:, None] * lane_freq
    shape = (n_rows, GRID_W, LANES)
    cos = jnp.where(use_col, jnp.cos(ang_c)[None], jnp.cos(ang_r)[:, None]).reshape(seq, LANES)
    sin = jnp.broadcast_to(jnp.where(use_col, jnp.sin(ang_c)[None], jnp.sin(ang_r)[:, None]),
                           shape)
    zero = jnp.zeros(shape, f32)
    sin_lo = jnp.where(upper, zero, -sin).reshape(seq, LANES)
    sin_hi = jnp.where(upper, sin, zero).reshape(seq, LANES)
    return cos, sin_lo, sin_hi


def kernel(x, c, ctx, c_ctx, norm_w, w_mod, b_mod, w_in, conv_w, w_a_out, w_b_out,
           attn_sink, w_o, final_norm_w):
    batch, seq, _ = x.shape
    ctx_len = ctx.shape[1]
    depth = w_in.shape[0]
    assert batch + 1 <= MOD_ROWS and seq % 512 == 0 and ctx_len % BLOCK == 0

    c_all = jnp.zeros((MOD_ROWS, D_MODEL), f32).at[:batch].set(c).at[batch].set(c_ctx)
    mod = _modulation(c_all, w_mod, b_mod)
    rope_tabs = _rope_tables(seq)

    w_in_b = w_in.astype(bf16)
    wa_b, wb_b, wo_b = w_a_out.astype(bf16), w_b_out.astype(bf16), w_o.astype(bf16)

    x2 = x.reshape(batch * seq, D_MODEL)
    c2 = ctx.reshape(batch * ctx_len, D_MODEL)
    for l in range(depth):
        last = l == depth - 1
        if last:
            kc, vc = _kv_projection(c2, l, norm_w, mod, w_in_b, tm=4 * ctx_len, mod_row=batch)
        else:
            pc = _projection(c2, l, norm_w, mod, w_in_b, conv_w, None, subs=(ctx_len,) * 4,
                             seq=ctx_len, mod_row0=batch, mod_per_seq=False)
            kc, vc = pc[2], pc[3]
        px = _projection(x2, l, norm_w, mod, w_in_b, conv_w, rope_tabs,
                         subs=(512, 512), seq=seq, mod_row0=0, mod_per_seq=True)
        x2 = _mixer(x2, l, mod, attn_sink, px, kc, vc, wa_b, wb_b, wo_b,
                    final_norm_w if last else None,
                    seq=seq, tq=1024, ctx_len=ctx_len, has_local=True, gate_row0=0,
                    gate_per_batch=True)
        if not last:
            c2 = _mixer(c2, l, mod, attn_sink, pc, kc, vc, wa_b, wb_b, wo_b, None,
                        seq=ctx_len, tq=ctx_len, ctx_len=ctx_len, has_local=False,
                        gate_row0=batch, gate_per_batch=False)
    return x2.reshape(batch, seq, D_MODEL)
```

```python
import functools
import math

import numpy as np
import jax
import jax.numpy as jnp
from jax import lax
from jax.experimental import pallas as pl
from jax.experimental.pallas import tpu as pltpu

D_MODEL = 1024
GRID_W = 64
CONV_WIDTH = 512
CONV_K = 3
N_HEADS = 8
N_KV_HEADS = 2
HEAD_DIM = 64
ATTN_WIDTH = N_HEADS * HEAD_DIM
KV_WIDTH = N_KV_HEADS * HEAD_DIM
BLOCK = 128
ROPE_THETA = 10000.0
ROPE_FREQS = HEAD_DIM // 4
EPS = 1e-6
NEG = -1e30
LOG2E = math.log2(math.e)

LANES = 128
SUBLANES = 8
MOD_ROWS = 24
VMEM_LIMIT = 56 * 1024 * 1024
N_SLAB = ATTN_WIDTH // LANES
HEADS_PER_SLAB = LANES // HEAD_DIM
SLABS_PER_KV = N_SLAB // N_KV_HEADS

_OFF_B = 0
_OFF_C = _OFF_B + CONV_WIDTH
_OFF_U = _OFF_C + CONV_WIDTH
_OFF_ZA = _OFF_U + CONV_WIDTH
_OFF_Q = _OFF_ZA + CONV_WIDTH
_OFF_K = _OFF_Q + ATTN_WIDTH
_OFF_V = _OFF_K + KV_WIDTH
_OFF_ZB = _OFF_V + KV_WIDTH
_OFF_GA = _OFF_ZB + ATTN_WIDTH
_OFF_GB = _OFF_GA + D_MODEL
IN_COLS = _OFF_GB + D_MODEL

f32 = jnp.float32
bf16 = jnp.bfloat16


def _sigmoid(x):
    return 0.5 * jnp.tanh(0.5 * x) + 0.5


def _silu(x):
    return x * _sigmoid(x)


def _mod_kernel(c_ref, w_ref, b_ref, o_ref):
    s = _silu(c_ref[...]).astype(bf16)
    res = jnp.dot(s, w_ref[...].astype(bf16), preferred_element_type=f32) + b_ref[...]
    for r in range(MOD_ROWS):
        o_ref[r] = res[r:r + 1, :]


def _modulation(c_all, w_mod, b_mod):
    depth = w_mod.shape[0]
    return pl.pallas_call(
        _mod_kernel,
        grid=(depth, 3),
        in_specs=[
            pl.BlockSpec((MOD_ROWS, D_MODEL), lambda l, j: (0, 0)),
            pl.BlockSpec((None, D_MODEL, D_MODEL), lambda l, j: (l, 0, j)),
            pl.BlockSpec((None, 1, D_MODEL), lambda l, j: (l, 0, j)),
        ],
        out_specs=pl.BlockSpec((None, MOD_ROWS, None, 1, D_MODEL), lambda l, j: (l, 0, j, 0, 0)),
        out_shape=jax.ShapeDtypeStruct((depth, MOD_ROWS, 3, 1, D_MODEL), f32),
        name="modulation",
    )(c_all, w_mod, b_mod.reshape(depth, 1, 3 * D_MODEL))


def _rope(p, cos, sin_lo, sin_hi):
    outs = []
    for j in range(p.shape[1] // LANES):
        x = p[:, j * LANES:(j + 1) * LANES]
        outs.append(x * cos
                    + pltpu.roll(x, LANES - ROPE_FREQS, 1) * sin_lo
                    + pltpu.roll(x, ROPE_FREQS, 1) * sin_hi)
    return outs[0] if len(outs) == 1 else jnp.concatenate(outs, axis=1)


def _with_swapped_halves(a):
    return jnp.concatenate([a, pltpu.roll(a, HEAD_DIM, 1)], axis=1)


def _norm_mod(x, gain, shift):
    ms = jnp.mean(x * x, axis=-1, keepdims=True)
    return (x * lax.rsqrt(ms + EPS) * gain + shift).astype(bf16)


def _kv_proj_kernel(x_ref, nw_ref, shift_ref, scale_ref, w_ref, k_ref, v_ref):
    gain, shift = nw_ref[...] * (1.0 + scale_ref[...]), shift_ref[...]
    kv = jnp.dot(_norm_mod(x_ref[...], gain, shift), w_ref[...], preferred_element_type=f32)
    k_ref[...] = _with_swapped_halves(kv[:, :KV_WIDTH]).astype(bf16)
    v_ref[...] = _with_swapped_halves(kv[:, KV_WIDTH:]).astype(bf16)


def _proj_kernel(*refs, rope, seq, subs):
    it = iter(refs)
    x_ref, xp_ref, xn_ref, nw_ref, shift_ref, scale_ref, w_ref, cw_ref = (
        next(it) for _ in range(8))
    if rope:
        cos_ref, slo_ref, shi_ref = (next(it) for _ in range(3))
    ya_ref, q_ref, k_ref, v_ref, zb_ref, ga_ref, gb_ref = (next(it) for _ in range(7))
    h_s, cu_s = next(it), next(it)
    tm = x_ref.shape[0]
    assert sum(subs) == tm and (seq % tm == 0 or tm % seq == 0)
    halo = 2 * SUBLANES
    gain, shift = nw_ref[...] * (1.0 + scale_ref[...]), shift_ref[...]
    tiles_per_seq = max(seq // tm, 1)
    t_in_seq = pl.program_id(0) % tiles_per_seq
    cw = cw_ref[...]
    zero = jnp.zeros((SUBLANES, CONV_WIDTH), f32)

    def buffers(n):
        s0 = sum(subs[:n]) + n * halo
        return h_s.at[s0:s0 + subs[n] + halo], cu_s.at[s0:s0 + subs[n] + halo]

    for n, sub in enumerate(subs):
        r0 = sum(subs[:n])
        rows = slice(r0, r0 + sub)
        h, cu_b = buffers(n)
        before = xp_ref[...] if r0 == 0 else x_ref[r0 - SUBLANES:r0, :]
        after = xn_ref[...] if r0 + sub == tm else x_ref[r0 + sub:r0 + sub + SUBLANES, :]
        h[0:sub, :] = _norm_mod(x_ref[rows, :], gain, shift)
        h[sub:sub + halo, :] = _norm_mod(jnp.concatenate([before, after], axis=0), gain, shift)

        def proj(off, width, n_rows=sub):
            return jnp.dot(h[0:n_rows, :], w_ref[:, off:off + width],
                           preferred_element_type=f32)

        cu = proj(_OFF_C, CONV_WIDTH, sub + halo) * proj(_OFF_U, CONV_WIDTH, sub + halo)
        cu_before, cu_after = cu[sub:sub + SUBLANES], cu[sub + SUBLANES:]
        if r0 % seq == 0:
            cu_before = jnp.where(t_in_seq > 0, cu_before, zero) if r0 == 0 else zero
        if (r0 + sub) % seq == 0 or r0 + sub == tm:
            cu_after = (jnp.where(t_in_seq < tiles_per_seq - 1, cu_after, zero)
                        if r0 + sub == tm else zero)
        cu_b[0:SUBLANES, :] = cu_before
        cu_b[SUBLANES:SUBLANES + sub, :] = cu[0:sub]
        cu_b[SUBLANES + sub:, :] = cu_after
        tap_prev = cu_b[SUBLANES - 1:SUBLANES - 1 + sub, :]
        tap_next = cu_b[SUBLANES + 1:SUBLANES + 1 + sub, :]
        if sub > seq:
            assert sub % seq == 0 and r0 % seq == 0
            pos = lax.broadcasted_iota(jnp.int32, (sub, 1), 0) % seq
            tap_prev = jnp.where(pos == 0, 0.0, tap_prev)
            tap_next = jnp.where(pos == seq - 1, 0.0, tap_next)
        conv = (tap_prev * cw[0:1] + cu_b[SUBLANES:SUBLANES + sub, :] * cw[1:2]
                + tap_next * cw[2:3])
        ya_ref[rows, :] = (proj(_OFF_B, CONV_WIDTH) * conv
                           * _silu(proj(_OFF_ZA, CONV_WIDTH))).astype(bf16)

        q = proj(_OFF_Q, ATTN_WIDTH)
        kv = proj(_OFF_K, 2 * KV_WIDTH)
        k = kv[:, :KV_WIDTH]
        if rope:
            cos, slo, shi = cos_ref[rows, :], slo_ref[rows, :], shi_ref[rows, :]
            q = _rope(q, cos, slo, shi)
            k = _rope(k, cos, slo, shi)
        q_ref[rows, :] = (q * (HEAD_DIM ** -0.5 * LOG2E)).astype(bf16)
        k_ref[rows, :] = _with_swapped_halves(k).astype(bf16)
        v_ref[rows, :] = _with_swapped_halves(kv[:, KV_WIDTH:]).astype(bf16)
        zb_ref[rows, :] = _silu(proj(_OFF_ZB, ATTN_WIDTH)).astype(bf16)
        part = 2 * LANES
        for c in range(D_MODEL // part):
            ga_ref[rows, c * part:(c + 1) * part] = _sigmoid(
                proj(_OFF_GA + c * part, part)).astype(bf16)
        for c in range(D_MODEL // part):
            gb_ref[rows, c * part:(c + 1) * part] = _sigmoid(
                proj(_OFF_GB + c * part, part)).astype(bf16)


def _mod_spec(layer, row_of, which):
    return pl.BlockSpec((None, None, None, 1, D_MODEL),
                        lambda *g: (layer, row_of(*g), which, 0, 0))


def _layer_spec(layer, shape, single_buffer=False):
    return pl.BlockSpec((None,) + tuple(shape), lambda *g: (layer,) + (0,) * len(shape),
                        pipeline_mode=pl.Buffered(1) if single_buffer else None)


def _projection(x2, layer, norm_w, mod, w_in, conv_w, rope_tabs, *, subs, seq, mod_row0,
                mod_per_seq):
    tm = sum(subs)
    scratch_rows = tm + len(subs) * 2 * SUBLANES
    n_rows = x2.shape[0]
    assert seq % tm == 0 or (tm % seq == 0 and not mod_per_seq and rope_tabs is None)
    tiles_per_seq = max(seq // tm, 1)
    halo_per_tile = tm // SUBLANES
    n_halo = n_rows // SUBLANES
    if mod_per_seq:
        mod_row = lambda t: mod_row0 + t // tiles_per_seq
    else:
        mod_row = lambda t: mod_row0
    row = lambda w: pl.BlockSpec((tm, w), lambda t: (t, 0))
    in_specs = [
        row(D_MODEL),
        pl.BlockSpec((SUBLANES, D_MODEL),
                     lambda t: (jnp.maximum(t * halo_per_tile - 1, 0), 0)),
        pl.BlockSpec((SUBLANES, D_MODEL),
                     lambda t: (jnp.minimum((t + 1) * halo_per_tile, n_halo - 1), 0)),
        _layer_spec(layer, (1, D_MODEL)),
        _mod_spec(layer, mod_row, 0),
        _mod_spec(layer, mod_row, 1),
        _layer_spec(layer, (D_MODEL, IN_COLS)),
        _layer_spec(layer, (CONV_K, CONV_WIDTH)),
    ]
    args = [x2, x2, x2, norm_w.reshape(-1, 1, D_MODEL), mod, mod, w_in, conv_w]
    if rope_tabs is not None:
        in_specs += [pl.BlockSpec((tm, LANES), lambda t: (t % tiles_per_seq, 0))] * 3
        args += list(rope_tabs)
    widths = (CONV_WIDTH, ATTN_WIDTH, 2 * KV_WIDTH, 2 * KV_WIDTH, ATTN_WIDTH, D_MODEL, D_MODEL)
    return pl.pallas_call(
        functools.partial(_proj_kernel, rope=rope_tabs is not None,
                          seq=seq, subs=tuple(subs)),
        grid=(n_rows // tm,),
        in_specs=in_specs,
        out_specs=[row(w) for w in widths],
        out_shape=[jax.ShapeDtypeStruct((n_rows, w), bf16) for w in widths],
        scratch_shapes=[pltpu.VMEM((scratch_rows, D_MODEL), bf16),
                        pltpu.VMEM((scratch_rows, CONV_WIDTH), f32)],
        compiler_params=pltpu.CompilerParams(
            dimension_semantics=("arbitrary",), vmem_limit_bytes=VMEM_LIMIT),
        name="projection_rope" if rope_tabs is not None else "projection",
    )(*args)


def _kv_projection(x2, layer, norm_w, mod, w_in, *, tm, mod_row):
    n_rows = x2.shape[0]
    kvw = 2 * KV_WIDTH
    assert _OFF_K % kvw == 0 and _OFF_V == _OFF_K + KV_WIDTH
    row = lambda w: pl.BlockSpec((tm, w), lambda t: (t, 0))
    return pl.pallas_call(
        _kv_proj_kernel,
        grid=(n_rows // tm,),
        in_specs=[row(D_MODEL),
                  _layer_spec(layer, (1, D_MODEL)),
                  _mod_spec(layer, lambda t: mod_row, 0),
                  _mod_spec(layer, lambda t: mod_row, 1),
                  pl.BlockSpec((None, D_MODEL, kvw), lambda t: (layer, 0, _OFF_K // kvw))],
        out_specs=[row(kvw), row(kvw)],
        out_shape=[jax.ShapeDtypeStruct((n_rows, kvw), bf16)] * 2,
        compiler_params=pltpu.CompilerParams(
            dimension_semantics=("arbitrary",), vmem_limit_bytes=VMEM_LIMIT),
        name="projection_kv",
    )(x2, norm_w.reshape(-1, 1, D_MODEL), mod, mod, w_in)


def _kv_variants(a2):
    a, sw = a2[:, :LANES], a2[:, LANES:]
    low = lax.broadcasted_iota(jnp.int32, a.shape, 1) < HEAD_DIM
    z = jnp.zeros_like(a)
    return [jnp.where(low, a, z), jnp.where(low, z, sw),
            jnp.where(low, sw, z), jnp.where(low, z, a)]


def _mix_kernel(*refs, layer, tq, ctx_len, has_local, final):
    it = iter(refs)
    sink_ref = next(it)
    x_ref, gate_ref, ya_ref, q_ref, zb_ref, ga_ref, gb_ref = (next(it) for _ in range(7))
    if has_local:
        k_ref, kp_ref, kn_ref, v_ref, vp_ref, vn_ref = (next(it) for _ in range(6))
    kc_ref, vc_ref = next(it), next(it)
    wa_ref, wb_ref, wo_ref = (next(it) for _ in range(3))
    if final:
        fw_ref = next(it)
    o_ref = next(it)
    y_s, yb_s, k_s, vm_s, s_s, p_s, z_s = (next(it) for _ in range(7))
    if has_local:
        bias_s, vmc_s = next(it), next(it)

    i = pl.program_id(1)
    n_i = pl.num_programs(1)
    nblk = tq // BLOCK
    n_ct = kc_ref.shape[0] // ctx_len
    assert has_local <= (n_ct == 1)

    def seq_of(j):
        return j // (nblk // n_ct)
    n_loc = 3 * BLOCK if has_local else 0
    nk = ctx_len + n_loc
    n_slot = vm_s.shape[0]
    ring = p_s.shape[0] // N_KV_HEADS
    rows_a = SLABS_PER_KV * BLOCK

    @pl.when(i == 0)
    def _():
        kvar = _kv_variants(kc_ref[...])
        vvar = _kv_variants(vc_ref[...])
        def ones_on(e, n_rows):
            low = lax.broadcasted_iota(jnp.int32, (n_rows, LANES), 1) < HEAD_DIM
            return jnp.where(low == (e == 0), 1.0, 0.0).astype(bf16)

        for n in range(4):
            k_s[n, 0:n_ct * ctx_len, :] = kvar[n]
        for kh in range(N_KV_HEADS):
            for e in range(HEADS_PER_SLAB):
                if has_local:
                    rows = slice(e * ctx_len, (e + 1) * ctx_len)
                    vmc_s[kh, rows, 0:LANES] = vvar[2 * kh + e]
                    vmc_s[kh, rows, LANES:] = ones_on(e, ctx_len)
                    for slot in range(n_slot):
                        vm_s[slot, kh, e * n_loc:(e + 1) * n_loc, LANES:] = ones_on(e, n_loc)
                else:
                    for slot in range(n_slot):
                        rows = slice(e * ctx_len, (e + 1) * ctx_len)
                        vm_s[slot, kh, rows, 0:LANES] = (
                            vvar[2 * kh + e][slot * ctx_len:(slot + 1) * ctx_len])
                        vm_s[slot, kh, rows, LANES:] = ones_on(e, ctx_len)

    def local_block(ref_prev, ref_tile, ref_next, t):
        if t == 0:
            return ref_prev[...]
        if t == nblk + 1:
            return ref_next[...]
        return ref_tile[(t - 1) * BLOCK:t * BLOCK, :]

    def fill_values(j, b):
        vvar = _kv_variants(local_block(vp_ref, v_ref, vn_ref, j + b))
        for kh in range(N_KV_HEADS):
            for e in range(HEADS_PER_SLAB):
                r = e * n_loc + b * BLOCK
                vm_s[j, kh, r:r + BLOCK, 0:LANES] = vvar[2 * kh + e]

    def fill_keys(t):
        for n, a in enumerate(_kv_variants(local_block(kp_ref, k_ref, kn_ref, t))):
            k_s[n, ctx_len + t * BLOCK:ctx_len + (t + 1) * BLOCK, :] = a

    def fill_bias():
        qi = lax.broadcasted_iota(jnp.int32, (rows_a, BLOCK), 0) % BLOCK
        ci = lax.broadcasted_iota(jnp.int32, (rows_a, BLOCK), 1)
        tri_prev = jnp.where(ci >= qi, 0.0, NEG)
        tri_next = jnp.where(ci <= qi, 0.0, NEG)
        bias_s[0] = tri_prev + jnp.where(i == 0, NEG, 0.0)
        bias_s[1] = tri_prev
        bias_s[2] = tri_next
        bias_s[3] = tri_next + jnp.where(i == n_i - 1, NEG, 0.0)

    if has_local:
        for t in range(nblk + 2):
            fill_keys(t)
        fill_bias()

    nt = (((1,), (1,)), ((), ()))

    def stage_a(j, kh, e):
        u = ((j % ring) * N_KV_HEADS + kh) * HEADS_PER_SLAB + e
        rows = slice(j * BLOCK, (j + 1) * BLOCK)
        q2 = jnp.concatenate(
            [q_ref[rows, (kh * SLABS_PER_KV + s) * LANES:(kh * SLABS_PER_KV + s + 1) * LANES]
             for s in range(SLABS_PER_KV)], axis=0)
        var = 2 * kh + e
        kc0 = seq_of(j) * ctx_len
        s_s[u, :, 0:ctx_len] = lax.dot_general(
            q2, k_s[var, kc0:kc0 + ctx_len, :], nt, preferred_element_type=f32)
        if has_local:
            r0 = ctx_len + j * BLOCK
            s_loc = lax.dot_general(q2, k_s[var, r0:r0 + n_loc, :], nt,
                                    preferred_element_type=f32)
            c0 = ctx_len
            s_s[u, :, c0:c0 + BLOCK] = s_loc[:, :BLOCK] + bias_s[0 if j == 0 else 1]
            s_s[u, :, c0 + BLOCK:c0 + 2 * BLOCK] = s_loc[:, BLOCK:2 * BLOCK]
            s_s[u, :, c0 + 2 * BLOCK:] = s_loc[:, 2 * BLOCK:] + bias_s[3 if j == nblk - 1 else 2]

    low_half = lax.broadcasted_iota(jnp.int32, (BLOCK, LANES), 1) < HEAD_DIM

    def stage_b(j, kh):
        jk = (j % ring) * N_KV_HEADS + kh
        for s in range(SLABS_PER_KV):
            rs = slice(s * BLOCK, (s + 1) * BLOCK)
            z = []
            for e in range(HEADS_PER_SLAB):
                u = jk * HEADS_PER_SLAB + e
                sk = sink_ref[layer, (kh * SLABS_PER_KV + s) * HEADS_PER_SLAB + e] * LOG2E
                mx = s_s[u, rs, 0:LANES]
                for n in range(1, nk // LANES):
                    mx = jnp.maximum(mx, s_s[u, rs, n * LANES:(n + 1) * LANES])
                m = jnp.maximum(jnp.max(mx, axis=1, keepdims=True), sk)
                for n in range(nk // LANES):
                    ctx_chunk = n * LANES < ctx_len
                    col = (e * ctx_len + n * LANES if ctx_chunk else
                           HEADS_PER_SLAB * ctx_len + e * n_loc + n * LANES - ctx_len)
                    p_s[jk, rs, col:col + LANES] = jnp.exp2(
                        s_s[u, rs, n * LANES:(n + 1) * LANES] - m).astype(bf16)
                z.append(jnp.exp2(sk - m))
            z_s[jk, rs, :] = jnp.where(low_half, z[0], z[1])

    def stage_c(j, kh):
        jk = (j % ring) * N_KV_HEADS + kh
        n_c = HEADS_PER_SLAB * ctx_len
        if has_local:
            res = (jnp.dot(p_s[jk, :, 0:n_c], vmc_s[kh], preferred_element_type=f32)
                   + jnp.dot(p_s[jk, :, n_c:], vm_s[j, kh], preferred_element_type=f32))
        else:
            res = jnp.dot(p_s[jk], vm_s[seq_of(j), kh], preferred_element_type=f32)
        attn = res[:, :LANES] / (res[:, LANES:] + z_s[jk])
        rows = slice(j * BLOCK, (j + 1) * BLOCK)
        for s in range(SLABS_PER_KV):
            c = kh * SLABS_PER_KV + s
            lanes = slice(c * LANES, (c + 1) * LANES)
            yb_s[rows, lanes] = (attn[s * BLOCK:(s + 1) * BLOCK]
                                 * zb_ref[rows, lanes].astype(f32)).astype(bf16)

    n_part = 4
    width = D_MODEL // n_part

    for t in range(nblk + 2):
        for kh in range(N_KV_HEADS):
            if has_local and 0 <= t - 1 < nblk:
                for b in range(3)[kh::N_KV_HEADS]:
                    fill_values(t - 1, b)
            if 0 <= t - 1 < nblk:
                stage_b(t - 1, kh)
            if 0 <= t - 2 < nblk:
                stage_c(t - 2, kh)
            if t < nblk:
                for e in range(HEADS_PER_SLAB):
                    stage_a(t, kh, e)

    for n in range(n_part):
        cols = slice(n * width, (n + 1) * width)
        y_s[:, cols] = (
            jnp.dot(ya_ref[...], wa_ref[:, cols], preferred_element_type=f32)
            * ga_ref[:, cols].astype(f32)
            + jnp.dot(yb_s[...], wb_ref[:, cols], preferred_element_type=f32)
            * gb_ref[:, cols].astype(f32)).astype(bf16)
    y16 = y_s[...]
    for n in range(n_part):
        cols = slice(n * width, (n + 1) * width)
        o_ref[:, cols] = x_ref[:, cols] + gate_ref[:, cols] * jnp.dot(
            y16, wo_ref[:, cols], preferred_element_type=f32)
    if final:
        xn = o_ref[...]
        ms = jnp.mean(xn * xn, axis=-1, keepdims=True)
        o_ref[...] = xn * lax.rsqrt(ms + EPS) * fw_ref[...]


def _mixer(x2, layer, mod, sink, proj_out, kc, vc, wa, wb, wo, final_w, *,
           seq, tq, ctx_len, has_local, gate_row0, gate_per_batch):
    ya, q, k, v, zb, ga, gb = proj_out
    n_rows = x2.shape[0]
    seqs_per_tile = max(tq // seq, 1)
    assert kc.shape[0] * seq == n_rows * ctx_len and not (has_local and seqs_per_tile > 1)
    n_tile_b = n_rows // (seq * seqs_per_tile)
    n_i = max(seq // tq, 1)
    nblk = tq // BLOCK
    final = final_w is not None

    def row(w):
        return pl.BlockSpec((tq, w), lambda b, i: (b * n_i + i, 0))

    def halo(rows_blk, w, side):
        per_tile = tq // rows_blk
        per_seq = seq // rows_blk
        if side < 0:
            return pl.BlockSpec((rows_blk, w), lambda b, i: (
                b * per_seq + jnp.maximum(i * per_tile - 1, 0), 0))
        return pl.BlockSpec((rows_blk, w), lambda b, i: (
            b * per_seq + jnp.minimum((i + 1) * per_tile, per_seq - 1), 0))

    def whole(shape):
        return pl.BlockSpec(shape, lambda b, i: (0,) * len(shape))

    if gate_per_batch:
        gate_spec = _mod_spec(layer, lambda b, i: gate_row0 + b, 2)
    else:
        gate_spec = _mod_spec(layer, lambda b, i: gate_row0, 2)

    kvw = 2 * KV_WIDTH
    in_specs = [pl.BlockSpec(memory_space=pltpu.SMEM),
                row(D_MODEL), gate_spec, row(CONV_WIDTH),
                row(ATTN_WIDTH), row(ATTN_WIDTH), row(D_MODEL), row(D_MODEL)]
    args = [sink, x2, mod, ya, q, zb, ga, gb]
    if has_local:
        in_specs += [row(kvw), halo(BLOCK, kvw, -1), halo(BLOCK, kvw, +1)] * 2
        args += [k, k, k, v, v, v]
    ctx_spec = pl.BlockSpec((seqs_per_tile * ctx_len, kvw), lambda b, i: (b, 0))
    in_specs += [ctx_spec, ctx_spec, _layer_spec(layer, (CONV_WIDTH, D_MODEL), True),
                 _layer_spec(layer, (ATTN_WIDTH, D_MODEL), True),
                 _layer_spec(layer, (D_MODEL, D_MODEL), True)]
    args += [kc, vc, wa, wb, wo]
    if final:
        in_specs.append(whole((1, D_MODEL)))
        args.append(final_w.reshape(1, D_MODEL))

    nk = ctx_len + (3 * BLOCK if has_local else 0)
    k_rows = ctx_len + tq + 2 * BLOCK if has_local else seqs_per_tile * ctx_len
    ring = min(nblk, 3)
    n_slot = nblk if has_local else seqs_per_tile
    vm_rows = HEADS_PER_SLAB * (nk - ctx_len if has_local else ctx_len)
    rows_a = SLABS_PER_KV * BLOCK
    scratch = [pltpu.VMEM((tq, D_MODEL), bf16),
               pltpu.VMEM((tq, ATTN_WIDTH), bf16),
               pltpu.VMEM((2 * N_KV_HEADS, k_rows, LANES), bf16),
               pltpu.VMEM((n_slot, N_KV_HEADS, vm_rows, 2 * LANES), bf16),
               pltpu.VMEM((ring * N_KV_HEADS * HEADS_PER_SLAB, rows_a, nk), f32),
               pltpu.VMEM((ring * N_KV_HEADS, rows_a, HEADS_PER_SLAB * nk), bf16),
               pltpu.VMEM((ring * N_KV_HEADS, rows_a, LANES), f32)]
    if has_local:
        scratch += [pltpu.VMEM((4, rows_a, BLOCK), f32),
                    pltpu.VMEM((N_KV_HEADS, HEADS_PER_SLAB * ctx_len, 2 * LANES), bf16)]

    return pl.pallas_call(
        functools.partial(_mix_kernel, layer=layer, tq=tq, ctx_len=ctx_len,
                          has_local=has_local, final=final),
        grid=(n_tile_b, n_i),
        in_specs=in_specs,
        out_specs=row(D_MODEL),
        out_shape=jax.ShapeDtypeStruct((n_rows, D_MODEL), f32),
        scratch_shapes=scratch,
        compiler_params=pltpu.CompilerParams(
            dimension_semantics=("arbitrary", "arbitrary"), vmem_limit_bytes=VMEM_LIMIT),
        name="mixer_latent" if has_local else "mixer_context",
    )(*args)


def _rope_tables(seq):
    n_rows = seq // GRID_W
    lane = np.arange(LANES)
    lane_freq = (jnp.asarray(ROPE_THETA, f32)
                 ** (-jnp.asarray(lane % ROPE_FREQS, f32) / ROPE_FREQS))[None, :]
    use_col = ((lane % HEAD_DIM) >= HEAD_DIM // 2)[None, None, :]
    upper = ((lane % (2 * ROPE_FREQS)) >= ROPE_FREQS)[None, None, :]
    ang_r = jnp.arange(n_rows, dtype=f32)[:, None] * lane_freq
    ang_c = jnp.arange(GRID_W, dtype=f32)[:, None] * lane_freq
    shape = (n_rows, GRID_W, LANES)
    cos = jnp.where(use_col, jnp.cos(ang_c)[None], jnp.cos(ang_r)[:, None]).reshape(seq, LANES)
    sin = jnp.broadcast_to(jnp.where(use_col, jnp.sin(ang_c)[None], jnp.sin(ang_r)[:, None]),
                           shape)
    zero = jnp.zeros(shape, f32)
    sin_lo = jnp.where(upper, zero, -sin).reshape(seq, LANES)
    sin_hi = jnp.where(upper, sin, zero).reshape(seq, LANES)
    return cos, sin_lo, sin_hi


def kernel(x, c, ctx, c_ctx, norm_w, w_mod, b_mod, w_in, conv_w, w_a_out, w_b_out,
           attn_sink, w_o, final_norm_w):
    batch, seq, _ = x.shape
    ctx_len = ctx.shape[1]
    depth = w_in.shape[0]
    assert batch + 1 <= MOD_ROWS and seq % 512 == 0 and ctx_len % BLOCK == 0

    c_all = jnp.zeros((MOD_ROWS, D_MODEL), f32).at[:batch].set(c).at[batch].set(c_ctx)
    mod = _modulation(c_all, w_mod, b_mod)
    rope_tabs = _rope_tables(seq)

    w_in_b = w_in.astype(bf16)
    wa_b, wb_b, wo_b = w_a_out.astype(bf16), w_b_out.astype(bf16), w_o.astype(bf16)

    x2 = x.reshape(batch * seq, D_MODEL)
    c2 = ctx.reshape(batch * ctx_len, D_MODEL)
    for l in range(depth):
        last = l == depth - 1
        if last:
            kc, vc = _kv_projection(c2, l, norm_w, mod, w_in_b, tm=4 * ctx_len, mod_row=batch)
        else:
            pc = _projection(c2, l, norm_w, mod, w_in_b, conv_w, None, subs=(2 * ctx_len,) * 2,
                             seq=ctx_len, mod_row0=batch, mod_per_seq=False)
            kc, vc = pc[2], pc[3]
        px = _projection(x2, l, norm_w, mod, w_in_b, conv_w, rope_tabs,
                         subs=(512, 512), seq=seq, mod_row0=0, mod_per_seq=True)
        x2 = _mixer(x2, l, mod, attn_sink, px, kc, vc, wa_b, wb_b, wo_b,
                    final_norm_w if last else None,
                    seq=seq, tq=1024, ctx_len=ctx_len, has_local=True, gate_row0=0,
                    gate_per_batch=True)
        if not last:
            c2 = _mixer(c2, l, mod, attn_sink, pc, kc, vc, wa_b, wb_b, wo_b, None,
                        seq=ctx_len, tq=ctx_len, ctx_len=ctx_len, has_local=False,
                        gate_row0=batch, gate_per_batch=False)
    return x2.reshape(batch, seq, D_MODEL)
```
